```python
import math
import jax, jax.numpy as jnp
from jax import lax
import numpy as np

D_MODEL = 1024
BATCH = 16
SEQ = 2048
DEPTH = 4

GRID_W = 64
CTX_LEN = 256
N_EVEN = (DEPTH + 1) // 2
N_ODD = DEPTH // 2
MIX_WIDTH = D_MODEL

RW_HEAD_DIM = 64
RW_WIDTH = MIX_WIDTH // 2
RW_HEADS = RW_WIDTH // RW_HEAD_DIM
DECAY_LORA = 64
ICLR_LORA = 64
GATE_LORA = 128
RW_SPLITS = [RW_WIDTH, 2 * RW_WIDTH, 3 * RW_WIDTH, 3 * RW_WIDTH + 2 * DECAY_LORA,
             3 * RW_WIDTH + 2 * DECAY_LORA + 2 * ICLR_LORA]
RW_COLS = 3 * RW_WIDTH + 2 * DECAY_LORA + 2 * ICLR_LORA + GATE_LORA
RW_GN_EPS = 64e-5

DF_HEAD_DIM = 64
DF_V_DIM = 2 * DF_HEAD_DIM
DF_WIDTH = MIX_WIDTH - RW_WIDTH
DF_HEADS = DF_WIDTH // DF_V_DIM
DF_QK = DF_HEADS * 2 * DF_HEAD_DIM
DF_COLS = 2 * DF_QK + DF_WIDTH
Q_BLOCK = 128
ROPE_THETA = 10000.0
EVEN_COLS = RW_COLS + DF_COLS

SGU_CHUNK = 128
SGU_WIDTH = D_MODEL
SGU_GROUPS = 8
SGU_GROUP_DIM = SGU_WIDTH // SGU_GROUPS

N_EXPERTS = 32
TOP_K = 4
D_EXPERT = D_MODEL
SWIGLU_LIMIT = 7.0
SWIGLU_ALPHA = 1.702

RMS_EPS = 1e-6
LN_EPS = 1e-5

kernel_name = "hybrid_rwkv7_diffattn_sgu_moe_dit"


def rms_norm(x, g):
    xf = x.astype(jnp.float32)
    y = xf * lax.rsqrt(jnp.mean(xf * xf, axis=-1, keepdims=True) + RMS_EPS)
    return (y * g.astype(jnp.float32)).astype(x.dtype)


def layer_norm(x, g, b):
    xf = x.astype(jnp.float32)
    mu = jnp.mean(xf, axis=-1, keepdims=True)
    var = jnp.mean(jnp.square(xf - mu), axis=-1, keepdims=True)
    return ((xf - mu) * lax.rsqrt(var + LN_EPS) * g + b).astype(x.dtype)


def modulate(xn, shift, scale):
    return xn * (1.0 + scale) + shift


def axial_rope_tables(n_tokens):
    rows = n_tokens // GRID_W
    row = jnp.repeat(jnp.arange(rows, dtype=jnp.float32), GRID_W)
    col = jnp.tile(jnp.arange(GRID_W, dtype=jnp.float32), rows)
    axis_dim = DF_HEAD_DIM // 2
    inv_freq = ROPE_THETA ** (-jnp.arange(0, axis_dim, 2, dtype=jnp.float32) / axis_dim)
    ar = row[:, None] * inv_freq
    ac = col[:, None] * inv_freq
    ang = jnp.concatenate([ar, ar, ac, ac], axis=-1)
    return jnp.cos(ang), jnp.sin(ang)


def apply_axial_rope(x, cos, sin):
    xf = x.astype(jnp.float32)
    xr = xf.reshape(xf.shape[:-1] + (2, 2, DF_HEAD_DIM // 4))
    rot = jnp.stack([-xr[..., 1, :], xr[..., 0, :]], axis=-2).reshape(xf.shape)
    return (xf * cos[None, :, None, None, :] + rot * sin[None, :, None, None, :]).astype(x.dtype)


def token_shift(z, mu):
    zp = jnp.pad(z, ((0, 0), (1, 0), (0, 0)))[:, :-1]
    zn = jnp.pad(z, ((0, 0), (0, 1), (0, 0)))[:, 1:]
    return z + mu[0] * (zp - z) + mu[1] * (zn - z)


def rwkv_features(za, p):
    B, T, _ = za.shape
    r, k, v, wl, al, gl = jnp.split(za.astype(jnp.float32), RW_SPLITS, axis=-1)
    wl = wl.reshape(B, T, 2, DECAY_LORA)
    al = al.reshape(B, T, 2, ICLR_LORA)
    w = -jax.nn.softplus(-(p["w0"] + jnp.einsum('btdr,drc->btdc', jnp.tanh(wl), p["w2"]))) - 0.5
    decay = jnp.exp(-jnp.exp(w))
    a = jax.nn.sigmoid(p["a0"] + jnp.einsum('btdr,drc->btdc', al, p["a2"]))
    g = jax.nn.sigmoid(gl) @ p["g2"]
    kk = (k * p["k_k"]).reshape(B, T, RW_HEADS, RW_HEAD_DIM)
    kk = kk / jnp.maximum(jnp.sqrt(jnp.sum(kk * kk, axis=-1, keepdims=True)), 1e-12)
    kd = k[:, :, None] * (1.0 + (a - 1.0) * p["k_a"])
    hd = lambda t: t.reshape(t.shape[:-1] + (RW_HEADS, RW_HEAD_DIM))
    return {"r": hd(r), "v": hd(v), "kk": kk, "k": hd(kd), "w": hd(decay), "a": hd(a), "g": g}


def wkv7_scan(state0, r, w, k, v, a, b, reverse):
    def step(S, inp):
        r_t, w_t, k_t, v_t, a_t, b_t = inp
        sa = jnp.einsum('bhvk,bhk->bhv', S, a_t)
        S = S * w_t[:, :, None, :] + sa[..., None] * b_t[:, :, None, :] + v_t[..., None] * k_t[:, :, None, :]
        return S, jnp.einsum('bhvk,bhk->bhv', S, r_t)
    return lax.scan(step, state0, (r, w, k, v, a, b), reverse=reverse)


def rwkv_scans(f, s0_fwd, s0_bwd):
    tm = lambda t: jnp.swapaxes(t, 0, 1)
    r, v, kk = tm(f["r"]), tm(f["v"]), tm(f["kk"])
    s_fwd, y_fwd = wkv7_scan(s0_fwd, r, tm(f["w"][:, :, 0]), tm(f["k"][:, :, 0]), v, -kk,
                             kk * tm(f["a"][:, :, 0]), False)
    s_bwd, y_bwd = wkv7_scan(s0_bwd, r, tm(f["w"][:, :, 1]), tm(f["k"][:, :, 1]), v, -kk,
                             kk * tm(f["a"][:, :, 1]), True)
    return s_fwd, s_bwd, tm(y_fwd + y_bwd)


def rwkv_output(f, y, p):
    B, T = y.shape[:2]
    mu = jnp.mean(y, axis=-1, keepdims=True)
    var = jnp.mean(jnp.square(y - mu), axis=-1, keepdims=True)
    yn = ((y - mu) * lax.rsqrt(var + RW_GN_EPS)).reshape(B, T, RW_WIDTH) * p["gn_w"] + p["gn_b"]
    bonus = jnp.sum(f["r"][:, :, None] * f["k"] * p["r_k"], axis=-1, keepdims=True) * f["v"][:, :, None]
    return (yn + jnp.sum(bonus, axis=2).reshape(B, T, RW_WIDTH)) * f["g"]


def diff_qkv(zb, p, rope):
    B, T, _ = zb.shape
    q, k, v = jnp.split(zb, [DF_QK, 2 * DF_QK], axis=-1)
    q = rms_norm(q.reshape(B, T, DF_HEADS, 2, DF_HEAD_DIM), p["qn"])
    k = rms_norm(k.reshape(B, T, DF_HEADS, 2, DF_HEAD_DIM), p["kn"])
    if rope is not None:
        q = apply_axial_rope(q, rope[0], rope[1])
        k = apply_axial_rope(k, rope[0], rope[1])
    return q, k, v.reshape(B, T, DF_HEADS, DF_V_DIM)


def diff_attend(q, K, V, lam):
    s = jnp.einsum('bqhme,bshme->bhmqs', q, K).astype(jnp.float32) * (DF_HEAD_DIM ** -0.5)
    pr = jax.nn.softmax(s, axis=-1)
    attn = pr[:, :, 0] - lam * pr[:, :, 1]
    return jnp.einsum('bhqs,bshe->bqhe', attn.astype(V.dtype), V)


def diff_attend_blocks(q, K, V, lam):
    B, T = q.shape[:2]
    qb = jnp.swapaxes(q.reshape((B, T // Q_BLOCK, Q_BLOCK) + q.shape[2:]), 0, 1)
    out = lax.map(lambda qq: diff_attend(qq, K, V, lam), qb)
    return jnp.swapaxes(out, 0, 1).reshape((B, T) + out.shape[3:])


def even_mixer(hl, hc, p, li, rope, want_ctx):
    B = hl.shape[0]
    zl = hl @ p["w_in"]
    zc = hc @ p["w_in"]
    fl = rwkv_features(token_shift(zl[..., :RW_COLS], p["mu"]), p)
    fc = rwkv_features(token_shift(zc[..., :RW_COLS], p["mu"]), p)
    zero = jnp.zeros((B, RW_HEADS, RW_HEAD_DIM, RW_HEAD_DIM), jnp.float32)
    s_fwd, s_bwd, yc_scan = rwkv_scans(fc, zero, zero)
    _, _, yl_scan = rwkv_scans(fl, s_fwd, s_bwd)
    lam_init = 0.8 - 0.6 * math.exp(-0.3 * li)
    lp = p["lam"].astype(jnp.float32)
    lam = jnp.exp(jnp.sum(lp[0] * lp[1])) - jnp.exp(jnp.sum(lp[2] * lp[3])) + lam_init
    ql, kl, vl = diff_qkv(zl[..., RW_COLS:], p, rope)
    qc, kc, vc = diff_qkv(zc[..., RW_COLS:], p, None)
    K = jnp.concatenate([kc, kl], axis=1)
    V = jnp.concatenate([vc, vl], axis=1)
    ol = diff_attend_blocks(ql, K, V, lam)

    def merge(f, y_scan, o):
        ya = rwkv_output(f, y_scan, p).astype(hl.dtype)
        ob = (rms_norm(o, p["subln"]) * (1.0 - lam_init)).reshape(o.shape[0], o.shape[1], DF_WIDTH)
        return jnp.concatenate([ya, ob.astype(hl.dtype)], axis=-1) @ p["w_out"]

    yl = merge(fl, yl_scan, ol)
    yc = merge(fc, yc_scan, diff_attend(qc, kc, vc, lam)) if want_ctx else None
    return yl, yc


def sgu_mixer(hn, p):
    B, T, _ = hn.shape
    z = jax.nn.gelu(hn @ p["w_in"] + p["b_in"], approximate=False)
    u, v = jnp.split(z, 2, axis=-1)
    v = layer_norm(v, p["ln_g"], p["ln_b"])
    v = v.reshape(B, T // SGU_CHUNK, SGU_CHUNK, SGU_GROUPS, SGU_GROUP_DIM)
    sv = jnp.einsum('gpq,bnqgc->bnpgc', p["ws"], v) + p["bs"].T[:, :, None]
    return (u * sv.reshape(B, T, SGU_WIDTH)) @ p["w_out"]


def moe(xn, rt_w, rt_b, w1, b1, w2, b2):
    shp = xn.shape
    t = xn.reshape(-1, shp[-1])
    logits = (t @ rt_w + rt_b).astype(jnp.float32)
    top_v, top_i = lax.top_k(logits, TOP_K)
    wts = jax.nn.softmax(top_v, axis=-1)
    gates = jnp.einsum('tk,tke->te', wts, jax.nn.one_hot(top_i, N_EXPERTS, dtype=jnp.float32)).astype(t.dtype)
    out = jnp.zeros_like(t)
    for e in range(N_EXPERTS):
        glu, lin = jnp.split(t @ w1[e] + b1[e], 2, axis=-1)
        glu = jnp.minimum(glu, SWIGLU_LIMIT)
        lin = jnp.clip(lin, -SWIGLU_LIMIT, SWIGLU_LIMIT)
        act = glu * jax.nn.sigmoid(SWIGLU_ALPHA * glu) * (lin + 1.0)
        out = out + gates[:, e:e + 1] * (act @ w2[e] + b2[e])
    return out.reshape(shp)


def setup_inputs(seed: int = 0) -> dict:
    key = jax.random.key(seed)
    ks = iter(jax.random.split(key, 48))
    nrm = lambda shape, std: std * jax.random.normal(next(ks), shape, jnp.float32)
    D, E, F = D_MODEL, N_EXPERTS, D_EXPERT
    return {
        "x": nrm((BATCH, SEQ, D), 1.0),
        "c": nrm((BATCH, D), 1.0),
        "ctx": nrm((BATCH, CTX_LEN, D), 1.0),
        "c_ctx": nrm((D,), 1.0),
        "norm1_g": 1.0 + nrm((DEPTH, D), 0.02),
        "norm2_g": 1.0 + nrm((DEPTH, D), 0.02),
        "ada_w": nrm((DEPTH, D, 6 * D), 0.5 * D ** -0.5),
        "ada_b": nrm((DEPTH, 6 * D), 0.02),
        "ev_w_in": nrm((N_EVEN, D, EVEN_COLS), D ** -0.5),
        "ev_w_out": nrm((N_EVEN, MIX_WIDTH, D), MIX_WIDTH ** -0.5),
        "rw_mu": jax.random.uniform(next(ks), (N_EVEN, 2, RW_COLS), jnp.float32, 0.0, 0.5),
        "rw_w0": jnp.linspace(-6.5, -1.5, RW_WIDTH, dtype=jnp.float32) + nrm((N_EVEN, 2, RW_WIDTH), 0.1),
        "rw_w2": nrm((N_EVEN, 2, DECAY_LORA, RW_WIDTH), 0.1 * DECAY_LORA ** -0.5),
        "rw_a0": nrm((N_EVEN, 2, RW_WIDTH), 0.1),
        "rw_a2": nrm((N_EVEN, 2, ICLR_LORA, RW_WIDTH), 0.1 * ICLR_LORA ** -0.5),
        "rw_g2": nrm((N_EVEN, GATE_LORA, RW_WIDTH), GATE_LORA ** -0.5),
        "rw_kk": 0.85 + nrm((N_EVEN, RW_WIDTH), 0.02),
        "rw_ka": 1.0 + nrm((N_EVEN, RW_WIDTH), 0.02),
        "rw_rk": nrm((N_EVEN, RW_HEADS, RW_HEAD_DIM), 0.1),
        "rw_gn_w": 1.0 + nrm((N_EVEN, RW_WIDTH), 0.02),
        "rw_gn_b": nrm((N_EVEN, RW_WIDTH), 0.02),
        "df_qn": 1.0 + nrm((N_EVEN, DF_HEAD_DIM), 0.02),
        "df_kn": 1.0 + nrm((N_EVEN, DF_HEAD_DIM), 0.02),
        "df_lam": nrm((N_EVEN, 4, DF_HEAD_DIM), 0.1),
        "df_subln": 1.0 + nrm((N_EVEN, DF_V_DIM), 0.02),
        "sg_w_in": nrm((N_ODD, D, 2 * SGU_WIDTH), D ** -0.5),
        "sg_b_in": nrm((N_ODD, 2 * SGU_WIDTH), 0.02),
        "sg_ln_g": 1.0 + nrm((N_ODD, SGU_WIDTH), 0.02),
        "sg_ln_b": nrm((N_ODD, SGU_WIDTH), 0.02),
        "sg_ws": nrm((N_ODD, SGU_GROUPS, SGU_CHUNK, SGU_CHUNK), SGU_CHUNK ** -0.5),
        "sg_bs": 1.0 + nrm((N_ODD, SGU_GROUPS, SGU_CHUNK), 0.1),
        "sg_w_out": nrm((N_ODD, SGU_WIDTH, D), SGU_WIDTH ** -0.5),
        "rt_w": nrm((DEPTH, D, E), D ** -0.5),
        "rt_b": nrm((DEPTH, E), 0.01),
        "ex_w1": nrm((DEPTH, E, D, 2 * F), D ** -0.5),
        "ex_b1": nrm((DEPTH, E, 2 * F), 0.01),
        "ex_w2": nrm((DEPTH, E, F, D), F ** -0.5),
        "ex_b2": nrm((DEPTH, E, D), 0.01),
    }


def reference(x, c, ctx, c_ctx, norm1_g, norm2_g, ada_w, ada_b, ev_w_in, ev_w_out,
              rw_mu, rw_w0, rw_w2, rw_a0, rw_a2, rw_g2, rw_kk, rw_ka, rw_rk, rw_gn_w, rw_gn_b,
              df_qn, df_kn, df_lam, df_subln,
              sg_w_in, sg_b_in, sg_ln_g, sg_ln_b, sg_ws, sg_bs, sg_w_out,
              rt_w, rt_b, ex_w1, ex_b1, ex_w2, ex_b2):
    B, T, D = x.shape
    rope = axial_rope_tables(T)
    silu_c = jax.nn.silu(c)
    silu_cc = jax.nn.silu(c_ctx)
    h_lat, h_ctx = x, ctx
    for l in range(DEPTH):
        ctx_out = any(m % 2 == 0 for m in range(l + 1, DEPTH))
        ctx_in = (l % 2 == 0) or ctx_out
        j = l // 2
        mod_l = (silu_c @ ada_w[l] + ada_b[l]).reshape(B, 6, 1, D)
        a_l = modulate(rms_norm(h_lat, norm1_g[l]), mod_l[:, 0], mod_l[:, 1])
        a_c = None
        if ctx_in:
            mod_c = (silu_cc @ ada_w[l] + ada_b[l]).reshape(6, D)
            a_c = modulate(rms_norm(h_ctx, norm1_g[l]), mod_c[0], mod_c[1])
        if l % 2 == 0:
            p = {"w_in": ev_w_in[j], "w_out": ev_w_out[j], "mu": rw_mu[j], "w0": rw_w0[j], "w2": rw_w2[j],
                 "a0": rw_a0[j], "a2": rw_a2[j], "g2": rw_g2[j], "k_k": rw_kk[j], "k_a": rw_ka[j],
                 "r_k": rw_rk[j], "gn_w": rw_gn_w[j], "gn_b": rw_gn_b[j], "qn": df_qn[j], "kn": df_kn[j],
                 "lam": df_lam[j], "subln": df_subln[j]}
            y_l, y_c = even_mixer(a_l, a_c, p, l, rope, ctx_out)
        else:
            p = {"w_in": sg_w_in[j], "b_in": sg_b_in[j], "ln_g": sg_ln_g[j], "ln_b": sg_ln_b[j],
                 "ws": sg_ws[j], "bs": sg_bs[j], "w_out": sg_w_out[j]}
            y_l = sgu_mixer(a_l, p)
            y_c = sgu_mixer(a_c, p) if ctx_out else None
        h_lat = h_lat + mod_l[:, 2] * y_l.astype(h_lat.dtype)
        f_l = modulate(rms_norm(h_lat, norm2_g[l]), mod_l[:, 3], mod_l[:, 4])
        if ctx_out:
            h_ctx = h_ctx + mod_c[2] * y_c.astype(h_ctx.dtype)
            f_c = modulate(rms_norm(h_ctx, norm2_g[l]), mod_c[3], mod_c[4])
            m = moe(jnp.concatenate([f_c, f_l], axis=1), rt_w[l], rt_b[l], ex_w1[l], ex_b1[l], ex_w2[l], ex_b2[l])
            n_ctx = f_c.shape[1]
            h_ctx = h_ctx + mod_c[5] * m[:, :n_ctx]
            h_lat = h_lat + mod_l[:, 5] * m[:, n_ctx:]
        else:
            m = moe(f_l, rt_w[l], rt_b[l], ex_w1[l], ex_b1[l], ex_w2[l], ex_b2[l])
            h_lat = h_lat + mod_l[:, 5] * m
    return h_lat
```

```python
import functools
import math

import jax
import jax.numpy as jnp
from jax import lax
from jax.experimental import pallas as pl
from jax.experimental.pallas import tpu as pltpu

F32 = jnp.float32
BF16 = jnp.bfloat16
HIGHEST = lax.Precision.HIGHEST

D_MODEL = 1024
GRID_W = 64
RMS_EPS = 1e-6
LN_EPS = 1e-5

RW_HEAD_DIM = 64
RW_WIDTH = 512
RW_HEADS = RW_WIDTH // RW_HEAD_DIM
DECAY_LORA = 64
ICLR_LORA = 64
GATE_LORA = 128
RW_COLS = 3 * RW_WIDTH + 2 * DECAY_LORA + 2 * ICLR_LORA + GATE_LORA
RW_GN_EPS = 64e-5

DF_HEAD_DIM = 64
DF_V_DIM = 128
DF_WIDTH = 512
DF_HEADS = DF_WIDTH // DF_V_DIM
DF_QK = DF_HEADS * 2 * DF_HEAD_DIM
ROPE_THETA = 10000.0
EVEN_COLS = RW_COLS + 2 * DF_QK + DF_WIDTH

SGU_CHUNK = 128
SGU_WIDTH = 1024
SGU_GROUPS = 8
SGU_GROUP_DIM = SGU_WIDTH // SGU_GROUPS

N_EXPERTS = 32
TOP_K = 4
SWIGLU_LIMIT = 7.0
SWIGLU_ALPHA = 1.702

LANES = 128
ROW_TILE = 256
SCAN_CHUNK = 64
EXPERT_ROW_TILE = 512
VMEM_LIMIT = 56 * 1024 * 1024


def _cparams(sem):
    return pltpu.CompilerParams(dimension_semantics=sem, vmem_limit_bytes=VMEM_LIMIT)


def _dot(a, b, precision=None):
    return jnp.dot(a, b, preferred_element_type=F32, precision=precision)


def _dot_nt(a, b, precision=None):
    return lax.dot_general(a, b, (((1,), (1,)), ((), ())), preferred_element_type=F32, precision=precision)


def _dot_tn(a, b, precision=None):
    return lax.dot_general(a, b, (((0,), (0,)), ((), ())), preferred_element_type=F32, precision=precision)


def _norm_mod(x, g, scale, shift):
    ms = jnp.mean(x * x, axis=-1, keepdims=True)
    return (x * lax.rsqrt(ms + RMS_EPS) * g) * (1.0 + scale) + shift


def _ada_kernel(x_ref, w_ref, b_ref, o_ref):
    x = x_ref[...]
    s = x * jax.nn.sigmoid(x)
    o_ref[...] = _dot(s, w_ref[...], HIGHEST) + b_ref[...]


def _ada_table(cs, ada_w, ada_b):
    depth, d, n = ada_w.shape
    r = cs.shape[0]
    tn = 1024
    return pl.pallas_call(
        _ada_kernel,
        grid=(depth, n // tn),
        in_specs=[pl.BlockSpec((r, d), lambda l, j: (0, 0)),
                  pl.BlockSpec((None, d, tn), lambda l, j: (l, 0, j)),
                  pl.BlockSpec((None, 1, tn), lambda l, j: (l, 0, j))],
        out_specs=pl.BlockSpec((None, r, tn), lambda l, j: (l, 0, j)),
        out_shape=jax.ShapeDtypeStruct((depth, r, n), F32),
        compiler_params=_cparams(("parallel", "parallel")),
        name="ada_table",
    )(cs, ada_w, ada_b.reshape(depth, 1, n))


def _kind(i, n_ctx):
    return (i * ROW_TILE >= n_ctx).astype(jnp.int32) if n_ctx > 0 else 1


def _nm_mm_kernel(h_ref, mod_ref, g_ref, w_ref, o_ref):
    y = _norm_mod(h_ref[...], g_ref[...], mod_ref[1:2, :], mod_ref[0:1, :])
    o_ref[...] = _dot(y.astype(BF16), w_ref[...]).astype(o_ref.dtype)


def _nm_mm(h, mods, g, w, n_ctx, out_dtype=F32):
    b, tt, d = h.shape
    n = w.shape[1]
    tm = ROW_TILE
    return pl.pallas_call(
        _nm_mm_kernel,
        grid=(b, tt // tm),
        in_specs=[pl.BlockSpec((None, tm, d), lambda bi, i: (bi, i, 0)),
                  pl.BlockSpec((None, None, 6, d), lambda bi, i: (bi, _kind(i, n_ctx), 0, 0)),
                  pl.BlockSpec((1, d), lambda bi, i: (0, 0)),
                  pl.BlockSpec((d, n), lambda bi, i: (0, 0))],
        out_specs=pl.BlockSpec((None, tm, n), lambda bi, i: (bi, i, 0)),
        out_shape=jax.ShapeDtypeStruct((b, tt, n), out_dtype),
        compiler_params=_cparams(("parallel", "parallel")),
        name="norm_mod_proj",
    )(h, mods, g.reshape(1, d), w)


def _mm_res_kernel(y_ref, w_ref, h_ref, mod_ref, o_ref):
    o_ref[...] = h_ref[...] + mod_ref[2:3, :] * _dot(y_ref[...].astype(BF16), w_ref[...])


def _mm_res(y, w, h, mods, n_ctx):
    b, tt, d = h.shape
    k = y.shape[-1]
    tm = ROW_TILE
    return pl.pallas_call(
        _mm_res_kernel,
        grid=(b, tt // tm),
        in_specs=[pl.BlockSpec((None, tm, k), lambda bi, i: (bi, i, 0)),
                  pl.BlockSpec((k, d), lambda bi, i: (0, 0)),
                  pl.BlockSpec((None, tm, d), lambda bi, i: (bi, i, 0)),
                  pl.BlockSpec((None, None, 6, d), lambda bi, i: (bi, _kind(i, n_ctx), 0, 0))],
        out_specs=pl.BlockSpec((None, tm, d), lambda bi, i: (bi, i, 0)),
        out_shape=jax.ShapeDtypeStruct((b, tt, d), F32),
        compiler_params=_cparams(("parallel", "parallel")),
        name="out_proj_residual",
    )(y, w, h, mods)


def _softplus(x):
    return jnp.maximum(x, 0.0) + jnp.log1p(jnp.exp(-jnp.abs(x)))


def _rw_feat_kernel(za_ref, w0_ref, w2_ref, a0_ref, a2_ref, g2_ref, kk_ref, ka_ref, seg_ref,
                    lw0_ref, lw1_ref, a0o_ref, a1o_ref, g_ref, kko_ref, kd0_ref, kd1_ref):
    c = RW_WIDTH
    k = za_ref[:, c:2 * c]
    wl = za_ref[:, 3 * c:3 * c + 2 * DECAY_LORA]
    al = za_ref[:, 3 * c + 2 * DECAY_LORA:3 * c + 2 * DECAY_LORA + 2 * ICLR_LORA]
    gl = za_ref[:, 3 * c + 2 * DECAY_LORA + 2 * ICLR_LORA:]
    twl = jnp.tanh(wl)
    lw_refs = (lw0_ref, lw1_ref)
    a_refs = (a0o_ref, a1o_ref)
    kd_refs = (kd0_ref, kd1_ref)
    for d in range(2):
        wd = w0_ref[d:d + 1, :] + _dot(twl[:, d * DECAY_LORA:(d + 1) * DECAY_LORA], w2_ref[d], HIGHEST)
        w = -_softplus(-wd) - 0.5
        lw_refs[d][...] = -jnp.exp(w)
        a = jax.nn.sigmoid(a0_ref[d:d + 1, :] + _dot(al[:, d * ICLR_LORA:(d + 1) * ICLR_LORA], a2_ref[d], HIGHEST))
        a_refs[d][...] = a
        kd_refs[d][...] = k * (1.0 + (a - 1.0) * ka_ref[...])
    g_ref[...] = _dot(jax.nn.sigmoid(gl), g2_ref[...], HIGHEST)
    kkr = k * kk_ref[...]
    ss = _dot(kkr * kkr, seg_ref[...], HIGHEST)
    kko_ref[...] = kkr / jnp.maximum(jnp.sqrt(ss), 1e-12)


def _rw_features(za, p):
    b, tt, _ = za.shape
    c = RW_WIDTH
    tm = ROW_TILE
    head = jnp.arange(c) // RW_HEAD_DIM
    seg = (head[:, None] == head[None, :]).astype(F32)
    full = lambda shape: pl.BlockSpec(shape, lambda bi, i: (0,) * len(shape))
    out_spec = pl.BlockSpec((None, tm, c), lambda bi, i: (bi, i, 0))
    return pl.pallas_call(
        _rw_feat_kernel,
        grid=(b, tt // tm),
        in_specs=[pl.BlockSpec((None, tm, RW_COLS), lambda bi, i: (bi, i, 0)),
                  full((2, c)), full((2, DECAY_LORA, c)), full((2, c)), full((2, ICLR_LORA, c)),
                  full((GATE_LORA, c)), full((1, c)), full((1, c)), full((c, c))],
        out_specs=[out_spec] * 8,
        out_shape=[jax.ShapeDtypeStruct((b, tt, c), F32)] * 8,
        compiler_params=_cparams(("parallel", "parallel")),
        name="rwkv_features",
    )(za, p["w0"], p["w2"], p["a0"], p["a2"], p["g2"], p["k_k"].reshape(1, c), p["k_a"].reshape(1, c), seg)


def _scan_kernel(r_ref, v_ref, kk_ref, kd_ref, lw_ref, a_ref, y_ref, s_ref, *, reverse, n_heads):
    L = SCAN_CHUNK
    n = RW_HEAD_DIM

    @pl.when(pl.program_id(2) == 0)
    def _():
        s_ref[...] = jnp.zeros_like(s_ref)

    row = lax.broadcasted_iota(jnp.int32, (L, L), 0)
    col = lax.broadcasted_iota(jnp.int32, (L, L), 1)
    if reverse:
        incl = col >= row
        strict = col > row
    else:
        incl = col <= row
        strict = col < row
    lw = lw_ref[...]
    cum = _dot(incl.astype(F32), lw, HIGHEST)
    ctot = cum[0:1, :] if reverse else cum[L - 1:L, :]
    g_prev = jnp.exp(cum - lw)
    g_inv = jnp.exp(-cum)
    g_cur = jnp.exp(cum)
    g_end = jnp.exp(ctot - cum)
    g_tot = jnp.exp(ctot)
    kk = kk_ref[...]
    kd = kd_ref[...]
    bb = kk * a_ref[...]
    a_t = (-kk * g_prev).astype(BF16)
    b_t = (bb * g_inv).astype(BF16)
    k_t = (kd * g_inv).astype(BF16)
    r_t = (r_ref[...] * g_cur).astype(BF16)
    b_e = (bb * g_end).astype(BF16)
    k_e = (kd * g_end).astype(BF16)
    v_all = v_ref[...].astype(BF16)
    n_double = int(math.log2(L))
    for h in range(n_heads):
        sl = slice(h * n, (h + 1) * n)
        A, Bm, K, R, V = a_t[:, sl], b_t[:, sl], k_t[:, sl], r_t[:, sl], v_all[:, sl]
        n_ab = jnp.where(strict, _dot_nt(A, Bm), 0.0)
        n_ak = jnp.where(strict, _dot_nt(A, K), 0.0).astype(BF16)
        m_rb = jnp.where(incl, _dot_nt(R, Bm), 0.0).astype(BF16)
        m_rk = jnp.where(incl, _dot_nt(R, K), 0.0).astype(BF16)
        s0 = s_ref[h]
        s0b = s0.astype(BF16)
        u = _dot_nt(A, s0b) + _dot(n_ak, V)
        x = n_ab
        for i in range(n_double):
            u = u + _dot(x, u, HIGHEST)
            if i + 1 < n_double:
                x = _dot(x, x, HIGHEST)
        ub = u.astype(BF16)
        y = _dot_nt(R, s0b) + _dot(m_rb, ub) + _dot(m_rk, V)
        y_ref[:, sl] = y
        s_ref[h] = s0 * g_tot[:, sl] + _dot_tn(ub, b_e[:, sl]) + _dot_tn(V, k_e[:, sl])


def _rw_scan(za, kk, kd, lw, a, n_ctx, reverse):
    b, tt, _ = za.shape
    c = RW_WIDTH
    L = SCAN_CHUNK
    nc = tt // L
    ncc = n_ctx // L
    if reverse:
        def cidx(i):
            return jnp.where(i < ncc, ncc - 1 - i, nc - 1 - (i - ncc))
    else:
        def cidx(i):
            return i
    spec = lambda colblk: pl.BlockSpec((None, L, c), lambda bi, hi, i: (bi, cidx(i), colblk))
    return pl.pallas_call(
        functools.partial(_scan_kernel, reverse=reverse, n_heads=RW_HEADS),
        grid=(b, 1, nc),
        in_specs=[spec(0), spec(2), spec(0), spec(0), spec(0), spec(0)],
        out_specs=spec(0),
        out_shape=jax.ShapeDtypeStruct((b, tt, c), F32),
        scratch_shapes=[pltpu.VMEM((RW_HEADS, RW_HEAD_DIM, RW_HEAD_DIM), F32)],
        compiler_params=_cparams(("parallel", "arbitrary", "arbitrary")),
        name="rwkv_scan_bwd" if reverse else "rwkv_scan_fwd",
    )(za, za, kk, kd, lw, a)


def _attn_kernel(lam_ref, q_ref, k_ref, v_ref, o_ref, *, n_ctx, tq):
    e = DF_HEAD_DIM
    tt = k_ref.shape[0]
    qi = pl.program_id(2)
    q = q_ref[...]
    k = k_ref[...]
    v = v_ref[...]
    key_pos = lax.broadcasted_iota(jnp.int32, (tq, tt), 1)
    kv_len = jnp.where(qi * tq < n_ctx, n_ctx, tt)
    visible = key_pos < kv_len
    outs = []
    for m in range(2):
        s = _dot_nt(q[:, m * e:(m + 1) * e], k[:, m * e:(m + 1) * e]) * (DF_HEAD_DIM ** -0.5)
        s = jnp.where(visible, s, -jnp.inf)
        p = jnp.exp(s - jnp.max(s, axis=-1, keepdims=True))
        l = jnp.sum(p, axis=-1, keepdims=True)
        outs.append(_dot(p.astype(BF16), v) / l)
    o_ref[...] = outs[0] - lam_ref[0] * outs[1]


def _diff_attention(q, k, v, lam, n_ctx):
    b, tt, _ = q.shape
    tq = ROW_TILE
    w = DF_V_DIM
    return pl.pallas_call(
        functools.partial(_attn_kernel, n_ctx=n_ctx, tq=tq),
        grid=(b, DF_HEADS, tt // tq),
        in_specs=[pl.BlockSpec(memory_space=pltpu.SMEM),
                  pl.BlockSpec((None, tq, w), lambda bi, hi, i: (bi, i, hi)),
                  pl.BlockSpec((None, tt, w), lambda bi, hi, i: (bi, 0, hi)),
                  pl.BlockSpec((None, tt, w), lambda bi, hi, i: (bi, 0, hi))],
        out_specs=pl.BlockSpec((None, tq, w), lambda bi, hi, i: (bi, i, hi)),
        out_shape=jax.ShapeDtypeStruct((b, tt, DF_WIDTH), F32),
        compiler_params=_cparams(("parallel", "parallel", "parallel")),
        name="diff_attention",
    )(lam.reshape(1).astype(F32), q, k, v)


def _sgu_kernel(h_ref, mod_ref, g_ref, w_in_ref, b_in_ref, lng_ref, lnb_ref, ws_ref, bst_ref, w_out_ref, o_ref):
    x = h_ref[...]
    tm = x.shape[0]
    y = _norm_mod(x, g_ref[...], mod_ref[1:2, :], mod_ref[0:1, :])
    z = _dot(y.astype(BF16), w_in_ref[...]) + b_in_ref[...]
    z = 0.5 * z * (1.0 + lax.erf(z * (2.0 ** -0.5)))
    u = z[:, :SGU_WIDTH]
    v = z[:, SGU_WIDTH:]
    mu = jnp.mean(v, axis=-1, keepdims=True)
    var = jnp.mean(jnp.square(v - mu), axis=-1, keepdims=True)
    v = ((v - mu) * lax.rsqrt(var + LN_EPS) * lng_ref[...] + lnb_ref[...]).astype(BF16)
    rows = []
    for ci in range(tm // SGU_CHUNK):
        cols = []
        for gi in range(SGU_GROUPS):
            vb = v[ci * SGU_CHUNK:(ci + 1) * SGU_CHUNK, gi * SGU_GROUP_DIM:(gi + 1) * SGU_GROUP_DIM]
            cols.append(_dot(ws_ref[gi], vb) + bst_ref[:, gi:gi + 1])
        rows.append(jnp.concatenate(cols, axis=1))
    sv = jnp.concatenate(rows, axis=0)
    o_ref[...] = x + mod_ref[2:3, :] * _dot((u * sv).astype(BF16), w_out_ref[...])


def _sgu_layer(h, mods, g, p, n_ctx):
    b, tt, d = h.shape
    tm = ROW_TILE
    full = lambda shape: pl.BlockSpec(shape, lambda bi, i: (0,) * len(shape))
    return pl.pallas_call(
        _sgu_kernel,
        grid=(b, tt // tm),
        in_specs=[pl.BlockSpec((None, tm, d), lambda bi, i: (bi, i, 0)),
                  pl.BlockSpec((None, None, 6, d), lambda bi, i: (bi, _kind(i, n_ctx), 0, 0)),
                  full((1, d)), full((d, 2 * SGU_WIDTH)), full((1, 2 * SGU_WIDTH)),
                  full((1, SGU_WIDTH)), full((1, SGU_WIDTH)),
                  full((SGU_GROUPS, SGU_CHUNK, SGU_CHUNK)), full((SGU_CHUNK, SGU_GROUPS)),
                  full((SGU_WIDTH, d))],
        out_specs=pl.BlockSpec((None, tm, d), lambda bi, i: (bi, i, 0)),
        out_shape=jax.ShapeDtypeStruct((b, tt, d), F32),
        compiler_params=_cparams(("parallel", "parallel")),
        name="sgu_layer",
    )(h, mods, g.reshape(1, d), p["w_in"].astype(BF16), p["b_in"].reshape(1, -1),
      p["ln_g"].reshape(1, -1), p["ln_b"].reshape(1, -1), p["ws"].astype(BF16), p["bs"].T,
      p["w_out"].astype(BF16))


def _router_kernel(h_ref, mod_ref, g_ref, rw_ref, rb_ref, f_ref, idx_ref, wt_ref, rank_ref, cnt_ref, base_ref):
    first = jnp.logical_and(pl.program_id(0) == 0, pl.program_id(1) == 0)

    @pl.when(first)
    def _():
        base_ref[...] = jnp.zeros_like(base_ref)

    x = h_ref[...]
    tm = x.shape[0]
    f = _norm_mod(x, g_ref[...], mod_ref[4:5, :], mod_ref[3:4, :])
    f_ref[...] = f.astype(f_ref.dtype)
    logits = _dot(f, rw_ref[...], HIGHEST) + rb_ref[...]
    lane = lax.broadcasted_iota(jnp.int32, (tm, LANES), 1)
    tri = (lax.broadcasted_iota(jnp.int32, (tm, tm), 1) < lax.broadcasted_iota(jnp.int32, (tm, tm), 0))
    vals, sels = [], []
    idx_out = jnp.zeros((tm, LANES), jnp.int32)
    l = logits
    for kk in range(TOP_K):
        m = jnp.max(l, axis=-1, keepdims=True)
        idx = jnp.min(jnp.where(l == m, lane, LANES), axis=-1, keepdims=True)
        sel = lane == idx
        l = jnp.where(sel, -jnp.inf, l)
        vals.append(m)
        sels.append(sel)
        idx_out = jnp.where(lane == kk, idx, idx_out)
    es = [jnp.exp(vv - vals[0]) for vv in vals]
    den = es[0] + es[1] + es[2] + es[3]
    wt_out = jnp.zeros((tm, LANES), F32)
    for kk in range(TOP_K):
        wt_out = jnp.where(lane == kk, es[kk] / den, wt_out)
    onehot = jnp.zeros((tm, LANES), F32)
    for sel in sels:
        onehot = onehot + sel.astype(F32)
    ahead = _dot(tri.astype(BF16), onehot.astype(BF16)) + base_ref[...]
    rank_out = jnp.zeros((tm, LANES), jnp.int32)
    for kk in range(TOP_K):
        rk = jnp.sum(jnp.where(sels[kk], ahead, 0.0), axis=-1, keepdims=True).astype(jnp.int32)
        rank_out = jnp.where(lane == kk, rk, rank_out)
    idx_ref[...] = idx_out
    wt_ref[...] = wt_out
    rank_ref[...] = rank_out
    base_ref[...] = base_ref[...] + jnp.sum(onehot, axis=0, keepdims=True)
    cnt_ref[...] = base_ref[...]


def _router(h, mods, g, rt_w, rt_b, n_ctx):
    b, tt, d = h.shape
    tm = ROW_TILE
    rw = jnp.zeros((d, LANES), F32).at[:, :N_EXPERTS].set(rt_w)
    rb = jnp.full((1, LANES), -jnp.inf, F32).at[0, :N_EXPERTS].set(rt_b)
    tok = lambda width: pl.BlockSpec((None, tm, width), lambda bi, i: (bi, i, 0))
    return pl.pallas_call(
        _router_kernel,
        grid=(b, tt // tm),
        in_specs=[tok(d),
                  pl.BlockSpec((None, None, 6, d), lambda bi, i: (bi, _kind(i, n_ctx), 0, 0)),
                  pl.BlockSpec((1, d), lambda bi, i: (0, 0)),
                  pl.BlockSpec((d, LANES), lambda bi, i: (0, 0)),
                  pl.BlockSpec((1, LANES), lambda bi, i: (0, 0))],
        out_specs=[tok(d), tok(LANES), tok(LANES), tok(LANES), pl.BlockSpec((1, LANES), lambda bi, i: (0, 0))],
        out_shape=[jax.ShapeDtypeStruct((b, tt, d), BF16),
                   jax.ShapeDtypeStruct((b, tt, LANES), jnp.int32),
                   jax.ShapeDtypeStruct((b, tt, LANES), F32),
                   jax.ShapeDtypeStruct((b, tt, LANES), jnp.int32),
                   jax.ShapeDtypeStruct((1, LANES), F32)],
        scratch_shapes=[pltpu.VMEM((1, LANES), F32)],
        compiler_params=_cparams(("arbitrary", "arbitrary")),
        name="moe_router",
    )(h, mods, g.reshape(1, d), rw, rb)


def _expert_kernel(te_ref, nv_ref, x_ref, gate_ref, w1_ref, b1_ref, w2_ref, b2_ref, o_ref):
    j = pl.program_id(0)

    @pl.when(j < nv_ref[0])
    def _():
        f = w2_ref.shape[0]
        hcat = _dot(x_ref[...], w1_ref[...]) + b1_ref[...]
        glu = jnp.minimum(hcat[:, :f], SWIGLU_LIMIT)
        lin = jnp.clip(hcat[:, f:], -SWIGLU_LIMIT, SWIGLU_LIMIT)
        act = glu * jax.nn.sigmoid(SWIGLU_ALPHA * glu) * (lin + 1.0)
        y = _dot(act.astype(BF16), w2_ref[...]) + b2_ref[...]
        o_ref[...] = (gate_ref[...] * y).astype(o_ref.dtype)

    @pl.when(j >= nv_ref[0])
    def _():
        o_ref[...] = jnp.zeros_like(o_ref)


def _experts(xs, row_gate, tile_expert, n_valid, w1, b1, w2, b2):
    s, d = xs.shape
    e, _, f2 = w1.shape
    f = f2 // 2
    tm = EXPERT_ROW_TILE
    grid_spec = pltpu.PrefetchScalarGridSpec(
        num_scalar_prefetch=2,
        grid=(s // tm,),
        in_specs=[pl.BlockSpec((tm, d), lambda j, te, nv: (j, 0)),
                  pl.BlockSpec((tm, 1), lambda j, te, nv: (j, 0)),
                  pl.BlockSpec((None, d, f2), lambda j, te, nv: (te[j], 0, 0)),
                  pl.BlockSpec((None, 1, f2), lambda j, te, nv: (te[j], 0, 0)),
                  pl.BlockSpec((None, f, d), lambda j, te, nv: (te[j], 0, 0)),
                  pl.BlockSpec((None, 1, d), lambda j, te, nv: (te[j], 0, 0))],
        out_specs=pl.BlockSpec((tm, d), lambda j, te, nv: (j, 0)),
    )
    return pl.pallas_call(
        _expert_kernel,
        grid_spec=grid_spec,
        out_shape=jax.ShapeDtypeStruct((s, d), F32),
        compiler_params=_cparams(("arbitrary",)),
        name="moe_experts",
    )(tile_expert, n_valid, xs, row_gate, w1, b1.reshape(e, 1, f2), w2, b2.reshape(e, 1, d))


def _moe(h, mods, g, rt_w, rt_b, w1, b1, w2, b2, n_ctx):
    b, tt, d = h.shape
    n = b * tt
    tm = EXPERT_ROW_TILE
    f, idx, wt, rank, cnt = _router(h, mods, g, rt_w, rt_b, n_ctx)
    top_i = idx.reshape(n, LANES)[:, :TOP_K]
    top_w = wt.reshape(n, LANES)[:, :TOP_K]
    rank = rank.reshape(n, LANES)[:, :TOP_K]
    counts = cnt[0, :N_EXPERTS].astype(jnp.int32)
    padded = ((counts + tm - 1) // tm) * tm
    p_end = jnp.cumsum(padded)
    p_off = p_end - padded
    u_off = jnp.cumsum(counts) - counts
    slot = (p_off[top_i] + rank).reshape(-1)
    n_tiles = (n * TOP_K) // tm + N_EXPERTS
    s_rows = n_tiles * tm
    _, sorted_pair = lax.sort_key_val(slot, jnp.arange(n * TOP_K, dtype=jnp.int32))
    srow = jnp.arange(s_rows, dtype=jnp.int32)
    row_e = jnp.minimum(jnp.searchsorted(p_end, srow, side="right"), N_EXPERTS - 1).astype(jnp.int32)
    within = srow - p_off[row_e]
    valid = within < counts[row_e]
    src = jnp.clip(u_off[row_e] + within, 0, n * TOP_K - 1)
    pair = sorted_pair[src]
    row_token = jnp.where(valid, pair // TOP_K, 0)
    row_gate = jnp.where(valid, top_w.reshape(-1)[pair], 0.0).reshape(s_rows, 1)
    n_valid = (p_end[-1] // tm).astype(jnp.int32).reshape(1)
    tile_start = jnp.arange(n_tiles, dtype=jnp.int32) * tm
    tile_expert = jnp.minimum(jnp.searchsorted(p_end, tile_start, side="right"), N_EXPERTS - 1).astype(jnp.int32)
    last_e = tile_expert[jnp.maximum(n_valid[0] - 1, 0)]
    tile_expert = jnp.where(tile_start < p_end[-1], tile_expert, last_e)
    xs = jnp.take(f.reshape(n, d), row_token, axis=0)
    ys = _experts(xs, row_gate, tile_expert, n_valid, w1.astype(BF16), b1, w2.astype(BF16), b2)
    m = jnp.take(ys, slot, axis=0).reshape(n, TOP_K, d).sum(axis=1).reshape(b, tt, d)
    gate = jnp.where((jnp.arange(tt) >= n_ctx)[None, :, None], mods[:, 1, 5][:, None, :], mods[:, 0, 5][:, None, :])
    return h + gate * m


def _token_shift(z, mu, n_ctx):
    tt = z.shape[1]
    pos = jnp.arange(tt)
    zp = jnp.pad(z, ((0, 0), (1, 0), (0, 0)))[:, :-1]
    zn = jnp.pad(z, ((0, 0), (0, 1), (0, 0)))[:, 1:]
    first = ((pos == 0) | (pos == n_ctx))[None, :, None]
    last = ((pos == tt - 1) | (pos == n_ctx - 1))[None, :, None]
    zp = jnp.where(first, 0.0, zp)
    zn = jnp.where(last, 0.0, zn)
    return z + mu[0] * (zp - z) + mu[1] * (zn - z)


def _rope_tables(n_tokens, n_ctx):
    rows = n_tokens // GRID_W
    row = jnp.repeat(jnp.arange(rows, dtype=F32), GRID_W)
    col = jnp.tile(jnp.arange(GRID_W, dtype=F32), rows)
    axis_dim = DF_HEAD_DIM // 2
    inv_freq = ROPE_THETA ** (-jnp.arange(0, axis_dim, 2, dtype=F32) / axis_dim)
    ar = row[:, None] * inv_freq
    ac = col[:, None] * inv_freq
    ang = jnp.concatenate([ar, ar, ac, ac], axis=-1)
    cos = jnp.concatenate([jnp.ones((n_ctx, DF_HEAD_DIM), F32), jnp.cos(ang)], axis=0)
    sin = jnp.concatenate([jnp.zeros((n_ctx, DF_HEAD_DIM), F32), jnp.sin(ang)], axis=0)
    return cos, sin


def _qk_prep(x, gain, cos, sin):
    b, tt, _ = x.shape
    xr = x.reshape(b, tt, DF_HEADS * 2, DF_HEAD_DIM)
    xr = xr * lax.rsqrt(jnp.mean(xr * xr, axis=-1, keepdims=True) + RMS_EPS) * gain
    x5 = xr.reshape(b, tt, DF_HEADS * 2, 2, 2, DF_HEAD_DIM // 4)
    rot = jnp.stack([-x5[..., 1, :], x5[..., 0, :]], axis=-2).reshape(xr.shape)
    out = xr * cos[None, :, None, :] + rot * sin[None, :, None, :]
    return out.reshape(b, tt, DF_QK)


def _even_layer(h, mods, g1, p, li, n_ctx, rope):
    b, tt, d = h.shape
    z = _nm_mm(h, mods, g1, p["w_in"].astype(BF16), n_ctx)
    za = _token_shift(z[..., :RW_COLS], p["mu"], n_ctx)
    lw0, lw1, a0, a1, gate, kk, kd0, kd1 = _rw_features(za, p)
    y_scan = (_rw_scan(za, kk, kd0, lw0, a0, n_ctx, False) + _rw_scan(za, kk, kd1, lw1, a1, n_ctx, True))
    r = za[..., :RW_WIDTH].reshape(b, tt, RW_HEADS, RW_HEAD_DIM)
    v = za[..., 2 * RW_WIDTH:3 * RW_WIDTH].reshape(b, tt, RW_HEADS, RW_HEAD_DIM)
    yh = y_scan.reshape(b, tt, RW_HEADS, RW_HEAD_DIM)
    mu = jnp.mean(yh, axis=-1, keepdims=True)
    var = jnp.mean(jnp.square(yh - mu), axis=-1, keepdims=True)
    yn = ((yh - mu) * lax.rsqrt(var + RW_GN_EPS)).reshape(b, tt, RW_WIDTH) * p["gn_w"] + p["gn_b"]
    kds = kd0.reshape(yh.shape) + kd1.reshape(yh.shape)
    bonus = jnp.sum(r * kds * p["r_k"], axis=-1, keepdims=True) * v
    ya = (yn + bonus.reshape(b, tt, RW_WIDTH)) * gate
    lam_init = 0.8 - 0.6 * math.exp(-0.3 * li)
    lp = p["lam"].astype(F32)
    lam = jnp.exp(jnp.sum(lp[0] * lp[1])) - jnp.exp(jnp.sum(lp[2] * lp[3])) + lam_init
    zb = z[..., RW_COLS:]
    q = _qk_prep(zb[..., :DF_QK], p["qn"], rope[0], rope[1]).astype(BF16)
    k = _qk_prep(zb[..., DF_QK:2 * DF_QK], p["kn"], rope[0], rope[1]).astype(BF16)
    vv = zb[..., 2 * DF_QK:].astype(BF16)
    o = _diff_attention(q, k, vv, lam, n_ctx).reshape(b, tt, DF_HEADS, DF_V_DIM)
    ob = o * lax.rsqrt(jnp.mean(o * o, axis=-1, keepdims=True) + RMS_EPS) * p["subln"] * (1.0 - lam_init)
    ycat = jnp.concatenate([ya, ob.reshape(b, tt, DF_WIDTH)], axis=-1)
    return _mm_res(ycat, p["w_out"].astype(BF16), h, mods, n_ctx)


def kernel(x, c, ctx, c_ctx, norm1_g, norm2_g, ada_w, ada_b, ev_w_in, ev_w_out, rw_mu, rw_w0, rw_w2, rw_a0, rw_a2, rw_g2, rw_kk, rw_ka, rw_rk, rw_gn_w, rw_gn_b, df_qn, df_kn, df_lam, df_subln, sg_w_in, sg_b_in, sg_ln_g, sg_ln_b, sg_ws, sg_bs, sg_w_out, rt_w, rt_b, ex_w1, ex_b1, ex_w2, ex_b2):
    b, t, d = x.shape
    n_ctx = ctx.shape[1]
    depth = ada_w.shape[0]
    assert d == D_MODEL and t % ROW_TILE == 0 and n_ctx % ROW_TILE == 0 and t % GRID_W == 0
    rope = _rope_tables(t, n_ctx)
    r_pad = -(-(b + 1) // 8) * 8
    cs = jnp.zeros((r_pad, d), F32).at[:b].set(c).at[b].set(c_ctx)
    ada = _ada_table(cs, ada_w, ada_b).reshape(depth, r_pad, 6, d)
    h = jnp.concatenate([ctx, x], axis=1)
    for l in range(depth):
        ctx_out = any(m % 2 == 0 for m in range(l + 1, depth))
        j = l // 2
        mods = jnp.stack([jnp.broadcast_to(ada[l, b], (b, 6, d)), ada[l, :b]], axis=1)
        if l % 2 == 0:
            p = {"w_in": ev_w_in[j], "w_out": ev_w_out[j], "mu": rw_mu[j], "w0": rw_w0[j], "w2": rw_w2[j],
                 "a0": rw_a0[j], "a2": rw_a2[j], "g2": rw_g2[j], "k_k": rw_kk[j], "k_a": rw_ka[j],
                 "r_k": rw_rk[j], "gn_w": rw_gn_w[j], "gn_b": rw_gn_b[j], "qn": df_qn[j], "kn": df_kn[j],
                 "lam": df_lam[j], "subln": df_subln[j]}
            h = _even_layer(h, mods, norm1_g[l], p, l, n_ctx, rope)
        else:
            p = {"w_in": sg_w_in[j], "b_in": sg_b_in[j], "ln_g": sg_ln_g[j], "ln_b": sg_ln_b[j],
                 "ws": sg_ws[j], "bs": sg_bs[j], "w_out": sg_w_out[j]}
            h = _sgu_layer(h, mods, norm1_g[l], p, n_ctx)
        if n_ctx > 0 and not ctx_out:
            h = h[:, n_ctx:]
            n_ctx = 0
        h = _moe(h, mods, norm2_g[l], rt_w[l], rt_b[l], ex_w1[l], ex_b1[l], ex_w2[l], ex_b2[l], n_ctx)
    return h[:, n_ctx:]
```

```python
import functools
import math

import jax
import jax.numpy as jnp
from jax import lax
from jax.experimental import pallas as pl
from jax.experimental.pallas import tpu as pltpu

F32 = jnp.float32
BF16 = jnp.bfloat16
HIGHEST = lax.Precision.HIGHEST

D_MODEL = 1024
GRID_W = 64
RMS_EPS = 1e-6
LN_EPS = 1e-5

RW_HEAD_DIM = 64
RW_WIDTH = 512
RW_HEADS = RW_WIDTH // RW_HEAD_DIM
DECAY_LORA = 64
ICLR_LORA = 64
GATE_LORA = 128
RW_COLS = 3 * RW_WIDTH + 2 * DECAY_LORA + 2 * ICLR_LORA + GATE_LORA
RW_GN_EPS = 64e-5

DF_HEAD_DIM = 64
DF_V_DIM = 128
DF_WIDTH = 512
DF_HEADS = DF_WIDTH // DF_V_DIM
DF_QK = DF_HEADS * 2 * DF_HEAD_DIM
ROPE_THETA = 10000.0
EVEN_COLS = RW_COLS + 2 * DF_QK + DF_WIDTH

SGU_CHUNK = 128
SGU_WIDTH = 1024
SGU_GROUPS = 8
SGU_GROUP_DIM = SGU_WIDTH // SGU_GROUPS

N_EXPERTS = 32
TOP_K = 4
SWIGLU_LIMIT = 7.0
SWIGLU_ALPHA = 1.702

LANES = 128
ROW_TILE = 256
SCAN_CHUNK = 64
EXPERT_ROW_TILE = 512
VMEM_LIMIT = 56 * 1024 * 1024


def _cparams(sem):
    return pltpu.CompilerParams(dimension_semantics=sem, vmem_limit_bytes=VMEM_LIMIT)


def _dot(a, b, precision=None):
    return jnp.dot(a, b, preferred_element_type=F32, precision=precision)


def _dot_nt(a, b, precision=None):
    return lax.dot_general(a, b, (((1,), (1,)), ((), ())), preferred_element_type=F32, precision=precision)


def _dot_tn(a, b, precision=None):
    return lax.dot_general(a, b, (((0,), (0,)), ((), ())), preferred_element_type=F32, precision=precision)


def _norm_mod(x, g, scale, shift):
    ms = jnp.mean(x * x, axis=-1, keepdims=True)
    return (x * lax.rsqrt(ms + RMS_EPS) * g) * (1.0 + scale) + shift


def _ada_kernel(x_ref, w_ref, b_ref, o_ref):
    x = x_ref[...]
    s = x * jax.nn.sigmoid(x)
    o_ref[...] = _dot(s, w_ref[...], HIGHEST) + b_ref[...]


def _ada_table(cs, ada_w, ada_b):
    depth, d, n = ada_w.shape
    r = cs.shape[0]
    tn = 1024
    return pl.pallas_call(
        _ada_kernel,
        grid=(depth, n // tn),
        in_specs=[pl.BlockSpec((r, d), lambda l, j: (0, 0)),
                  pl.BlockSpec((None, d, tn), lambda l, j: (l, 0, j)),
                  pl.BlockSpec((None, 1, tn), lambda l, j: (l, 0, j))],
        out_specs=pl.BlockSpec((None, r, tn), lambda l, j: (l, 0, j)),
        out_shape=jax.ShapeDtypeStruct((depth, r, n), F32),
        compiler_params=_cparams(("parallel", "parallel")),
        name="ada_table",
    )(cs, ada_w, ada_b.reshape(depth, 1, n))


def _kind(i, n_ctx):
    return (i * ROW_TILE >= n_ctx).astype(jnp.int32) if n_ctx > 0 else 1


def _nm_mm_kernel(h_ref, mod_ref, g_ref, w_ref, o_ref):
    y = _norm_mod(h_ref[...], g_ref[...], mod_ref[1:2, :], mod_ref[0:1, :])
    o_ref[...] = _dot(y.astype(BF16), w_ref[...]).astype(o_ref.dtype)


def _nm_mm(h, mods, g, w, n_ctx, out_dtype=F32):
    b, tt, d = h.shape
    n = w.shape[1]
    tm = ROW_TILE
    return pl.pallas_call(
        _nm_mm_kernel,
        grid=(b, tt // tm),
        in_specs=[pl.BlockSpec((None, tm, d), lambda bi, i: (bi, i, 0)),
                  pl.BlockSpec((None, None, 6, d), lambda bi, i: (bi, _kind(i, n_ctx), 0, 0)),
                  pl.BlockSpec((1, d), lambda bi, i: (0, 0)),
                  pl.BlockSpec((d, n), lambda bi, i: (0, 0))],
        out_specs=pl.BlockSpec((None, tm, n), lambda bi, i: (bi, i, 0)),
        out_shape=jax.ShapeDtypeStruct((b, tt, n), out_dtype),
        compiler_params=_cparams(("parallel", "parallel")),
        name="norm_mod_proj",
    )(h, mods, g.reshape(1, d), w)


def _group_ones(width, group):
    idx = jnp.arange(width) // group
    return (idx[:, None] == idx[None, :]).astype(BF16)


def _group_sum(x, ones_bf16):
    hi = x.astype(BF16)
    lo = (x - hi.astype(F32)).astype(BF16)
    return _dot(hi, ones_bf16) + _dot(lo, ones_bf16)


def _softplus(x):
    return jnp.maximum(x, 0.0) + jnp.log1p(jnp.exp(-jnp.abs(x)))


def _rw_feat_kernel(z_ref, zprev_ref, znext_ref, mu_ref, w0_ref, w2_ref, a0_ref, a2_ref, g2_ref, kk_ref, ka_ref,
                    rk_ref, seg_ref,
                    r_ref, v_ref, kko_ref, kd0_ref, kd1_ref, lw0_ref, lw1_ref, bb0_ref, bb1_ref, g_ref, bonus_ref,
                    *, n_ctx, tt):
    c = RW_WIDTH
    tm = z_ref.shape[0]
    i = pl.program_id(1)
    z = z_ref[...]
    seq_first = jnp.logical_or(i * tm == 0, i * tm == n_ctx)
    seq_last = jnp.logical_or((i + 1) * tm == n_ctx, (i + 1) * tm == tt)
    prev_row = jnp.where(seq_first, 0.0, zprev_ref[7:8, :])
    next_row = jnp.where(seq_last, 0.0, znext_ref[0:1, :])
    rows = lax.broadcasted_iota(jnp.int32, z.shape, 0)
    zp = jnp.where(rows == 0, prev_row, pltpu.roll(z, 1, 0))
    zn = jnp.where(rows == tm - 1, next_row, pltpu.roll(z, tm - 1, 0))
    za = z + mu_ref[0:1, :] * (zp - z) + mu_ref[1:2, :] * (zn - z)
    r = za[:, :c]
    k = za[:, c:2 * c]
    v = za[:, 2 * c:3 * c]
    wl = za[:, 3 * c:3 * c + 2 * DECAY_LORA]
    al = za[:, 3 * c + 2 * DECAY_LORA:3 * c + 2 * DECAY_LORA + 2 * ICLR_LORA]
    gl = za[:, 3 * c + 2 * DECAY_LORA + 2 * ICLR_LORA:]
    r_ref[...] = r
    v_ref[...] = v
    twl = jnp.tanh(wl)
    seg = seg_ref[...]
    kkr = k * kk_ref[...]
    kk = kkr / jnp.maximum(jnp.sqrt(_group_sum(kkr * kkr, seg)), 1e-12)
    kko_ref[...] = kk
    lw_refs = (lw0_ref, lw1_ref)
    bb_refs = (bb0_ref, bb1_ref)
    kd_refs = (kd0_ref, kd1_ref)
    kd_sum = None
    for d in range(2):
        wd = w0_ref[d:d + 1, :] + _dot(twl[:, d * DECAY_LORA:(d + 1) * DECAY_LORA], w2_ref[d], HIGHEST)
        w = -_softplus(-wd) - 0.5
        lw_refs[d][...] = -jnp.exp(w)
        a = jax.nn.sigmoid(a0_ref[d:d + 1, :] + _dot(al[:, d * ICLR_LORA:(d + 1) * ICLR_LORA], a2_ref[d], HIGHEST))
        bb_refs[d][...] = kk * a
        kd = k * (1.0 + (a - 1.0) * ka_ref[...])
        kd_refs[d][...] = kd
        kd_sum = kd if kd_sum is None else kd_sum + kd
    g_ref[...] = _dot(jax.nn.sigmoid(gl), g2_ref[...], HIGHEST)
    bonus_ref[...] = _group_sum(r * kd_sum * rk_ref[...], seg) * v


def _rw_features(z, p, n_ctx):
    b, tt, _ = z.shape
    c = RW_WIDTH
    tm = ROW_TILE
    hb = tm // 8
    last_halo = tt // 8 - 1
    full = lambda shape: pl.BlockSpec(shape, lambda bi, i: (0,) * len(shape))
    out_spec = pl.BlockSpec((None, tm, c), lambda bi, i: (bi, i, 0))
    return pl.pallas_call(
        functools.partial(_rw_feat_kernel, n_ctx=n_ctx, tt=tt),
        grid=(b, tt // tm),
        in_specs=[pl.BlockSpec((None, tm, RW_COLS), lambda bi, i: (bi, i, 0)),
                  pl.BlockSpec((None, 8, RW_COLS), lambda bi, i: (bi, jnp.maximum(i * hb - 1, 0), 0)),
                  pl.BlockSpec((None, 8, RW_COLS), lambda bi, i: (bi, jnp.minimum((i + 1) * hb, last_halo), 0)),
                  full((2, RW_COLS)), full((2, c)), full((2, DECAY_LORA, c)), full((2, c)), full((2, ICLR_LORA, c)),
                  full((GATE_LORA, c)), full((1, c)), full((1, c)), full((1, c)), full((c, c))],
        out_specs=[out_spec] * 11,
        out_shape=[jax.ShapeDtypeStruct((b, tt, c), F32)] * 11,
        compiler_params=_cparams(("parallel", "parallel")),
        name="rwkv_features",
    )(z, z, z, p["mu"], p["w0"], p["w2"], p["a0"], p["a2"], p["g2"], p["k_k"].reshape(1, c), p["k_a"].reshape(1, c),
      p["r_k"].reshape(1, c), _group_ones(c, RW_HEAD_DIM))


def _scan_kernel(r_ref, v_ref, kk_ref, kd_ref, lw_ref, bb_ref, y_ref, s_ref, *, reverse, n_heads):
    L = SCAN_CHUNK
    n = RW_HEAD_DIM

    @pl.when(pl.program_id(1) == 0)
    def _():
        s_ref[...] = jnp.zeros_like(s_ref)

    row = lax.broadcasted_iota(jnp.int32, (L, L), 0)
    col = lax.broadcasted_iota(jnp.int32, (L, L), 1)
    if reverse:
        incl = col >= row
        strict = col > row
    else:
        incl = col <= row
        strict = col < row
    eye = row == col
    lw = lw_ref[...]
    cum = _dot(incl.astype(F32), lw, HIGHEST)
    ctot = cum[0:1, :] if reverse else cum[L - 1:L, :]
    g_prev = jnp.exp(cum - lw)
    g_inv = jnp.exp(-cum)
    g_cur = jnp.exp(cum)
    g_end = jnp.exp(ctot - cum)
    g_tot = jnp.exp(ctot)
    kd = kd_ref[...]
    bb = bb_ref[...]
    a_t = (-kk_ref[...] * g_prev).astype(BF16)
    b_t = (bb * g_inv).astype(BF16)
    k_t = (kd * g_inv).astype(BF16)
    r_f = r_ref[...] * g_cur
    r_t = r_f.astype(BF16)
    b_e = (bb * g_end).astype(BF16)
    k_e = (kd * g_end).astype(BF16)
    v_all = v_ref[...].astype(BF16)
    level_masks = []
    for lg in range(int(math.log2(L))):
        same_pair = jnp.right_shift(row, lg + 1) == jnp.right_shift(col, lg + 1)
        other_half = jnp.right_shift(row, lg) != jnp.right_shift(col, lg)
        level_masks.append(jnp.logical_and(jnp.logical_and(same_pair, other_half), strict))
    for h in range(n_heads):
        sl = slice(h * n, (h + 1) * n)
        A, V = a_t[:, sl], v_all[:, sl]
        gram = _dot_nt(jnp.concatenate([A, r_t[:, sl]], axis=0), jnp.concatenate([b_t[:, sl], k_t[:, sl]], axis=0))
        n_ab = gram[:L, :L]
        lower = jnp.concatenate([jnp.where(strict, gram[:L, L:], 0.0), jnp.where(incl, gram[L:, L:], 0.0)], axis=0)
        m_rb = jnp.where(incl, gram[L:, :L], 0.0).astype(BF16)
        t_inv = jnp.where(eye, 1.0, jnp.where(level_masks[0], n_ab, 0.0))
        for mask in level_masks[1:]:
            tb = t_inv.astype(BF16)
            t_inv = t_inv + _dot(_dot(tb, jnp.where(mask, n_ab, 0.0).astype(BF16)).astype(BF16), tb)
        nv = _dot(lower.astype(BF16), V)
        au = _dot(t_inv.astype(BF16), jnp.concatenate([A, nv[:L].astype(BF16)], axis=1))
        aub = au.astype(BF16)
        ry = _dot(m_rb, aub)
        r_hat = (r_f[:, sl] + ry[:, :n]).astype(BF16)
        y_hat = ry[:, n:] + nv[L:]
        be = b_e[:, sl]
        pq = _dot_tn(aub, be)
        p_mat = jnp.where(eye, g_tot[:, sl], 0.0) + pq[:n]
        q_mat = pq[n:] + _dot_tn(V, k_e[:, sl])
        s0b = s_ref[h].astype(BF16)
        y_ref[:, sl] = _dot_nt(r_hat, s0b) + y_hat
        s_ref[h] = _dot(s0b, p_mat.astype(BF16)) + q_mat


def _rw_scan(r, v, kk, kd, lw, bb, n_ctx, reverse):
    b, tt, c = r.shape
    L = SCAN_CHUNK
    nc = tt // L
    ncc = n_ctx // L
    if reverse:
        def cidx(i):
            return jnp.where(i < ncc, ncc - 1 - i, nc - 1 - (i - ncc))
    else:
        def cidx(i):
            return i
    spec = pl.BlockSpec((None, L, c), lambda bi, i: (bi, cidx(i), 0))
    return pl.pallas_call(
        functools.partial(_scan_kernel, reverse=reverse, n_heads=RW_HEADS),
        grid=(b, nc),
        in_specs=[spec] * 6,
        out_specs=spec,
        out_shape=jax.ShapeDtypeStruct((b, tt, c), F32),
        scratch_shapes=[pltpu.VMEM((RW_HEADS, RW_HEAD_DIM, RW_HEAD_DIM), F32)],
        compiler_params=_cparams(("parallel", "arbitrary")),
        name="rwkv_scan_bwd" if reverse else "rwkv_scan_fwd",
    )(r, v, kk, kd, lw, bb)


def _qk_prep(x, gain, cos, sin, ones):
    xn = x * lax.rsqrt(_group_sum(x * x, ones) * (1.0 / DF_HEAD_DIM) + RMS_EPS) * gain
    lane = lax.broadcasted_iota(jnp.int32, x.shape, 1)
    quarter = DF_HEAD_DIM // 4
    rot = jnp.where(jnp.bitwise_and(lane, 2 * quarter - 1) < quarter, -pltpu.roll(xn, x.shape[1] - quarter, 1), pltpu.roll(xn, quarter, 1))
    return xn * cos + rot * sin


def _attn_kernel(lam_ref, q_ref, k_ref, v_ref, cosq_ref, sinq_ref, cosk_ref, sink_ref, qn_ref, kn_ref, sub_ref, ones_ref,
                 o_ref, kb_ref, vb_ref, *, n_ctx, tq, out_scale):
    e = DF_HEAD_DIM
    tt = k_ref.shape[0]
    qi = pl.program_id(2)
    ones = ones_ref[...]

    @pl.when(qi == 0)
    def _():
        kb_ref[...] = _qk_prep(k_ref[...], kn_ref[...], cosk_ref[...], sink_ref[...], ones).astype(BF16)
        vb_ref[...] = v_ref[...].astype(BF16)

    q = (_qk_prep(q_ref[...], qn_ref[...], cosq_ref[...], sinq_ref[...], ones) * (DF_HEAD_DIM ** -0.5)).astype(BF16)
    k = kb_ref[...]
    v = vb_ref[...]
    key_pos = lax.broadcasted_iota(jnp.int32, (tq, tt), 1)
    kv_len = jnp.where(qi * tq < n_ctx, n_ctx, tt)
    visible = key_pos < kv_len
    outs = []
    for m in range(2):
        s = _dot_nt(q[:, m * e:(m + 1) * e], k[:, m * e:(m + 1) * e])
        s = jnp.where(visible, s, -jnp.inf)
        p = jnp.exp(s - jnp.max(s, axis=-1, keepdims=True))
        l = jnp.sum(p, axis=-1, keepdims=True)
        outs.append(_dot(p.astype(BF16), v) / l)
    o = outs[0] - lam_ref[0] * outs[1]
    o = o * lax.rsqrt(jnp.mean(o * o, axis=-1, keepdims=True) + RMS_EPS) * sub_ref[...] * out_scale
    o_ref[...] = o.astype(o_ref.dtype)


def _diff_attention(z, p, lam, lam_init, cos, sin, n_ctx):
    b, tt, _ = z.shape
    tq = ROW_TILE
    w = DF_V_DIM
    q0 = RW_COLS // w
    k0 = q0 + DF_HEADS
    v0 = k0 + DF_HEADS
    two = lambda g: jnp.tile(g, 2).reshape(1, w)
    full = lambda shape: pl.BlockSpec(shape, lambda bi, hi, i: (0,) * len(shape))
    return pl.pallas_call(
        functools.partial(_attn_kernel, n_ctx=n_ctx, tq=tq, out_scale=1.0 - lam_init),
        grid=(b, DF_HEADS, tt // tq),
        in_specs=[pl.BlockSpec(memory_space=pltpu.SMEM),
                  pl.BlockSpec((None, tq, w), lambda bi, hi, i: (bi, i, q0 + hi)),
                  pl.BlockSpec((None, tt, w), lambda bi, hi, i: (bi, 0, k0 + hi)),
                  pl.BlockSpec((None, tt, w), lambda bi, hi, i: (bi, 0, v0 + hi)),
                  pl.BlockSpec((tq, w), lambda bi, hi, i: (i, 0)),
                  pl.BlockSpec((tq, w), lambda bi, hi, i: (i, 0)),
                  full((tt, w)), full((tt, w)), full((1, w)), full((1, w)), full((1, w)), full((w, w))],
        out_specs=pl.BlockSpec((None, tq, w), lambda bi, hi, i: (bi, i, hi)),
        out_shape=jax.ShapeDtypeStruct((b, tt, DF_WIDTH), BF16),
        scratch_shapes=[pltpu.VMEM((tt, w), BF16), pltpu.VMEM((tt, w), BF16)],
        compiler_params=_cparams(("parallel", "parallel", "arbitrary")),
        name="diff_attention",
    )(lam.reshape(1).astype(F32), z, z, z, cos, sin, cos, sin, two(p["qn"]), two(p["kn"]),
      p["subln"].reshape(1, w), _group_ones(w, DF_HEAD_DIM))


def _mix_out_kernel(yf_ref, yb_ref, bonus_ref, gate_ref, ob_ref, gnw_ref, gnb_ref, seg_ref, w_ref, h_ref, mod_ref, o_ref):
    seg = seg_ref[...]
    y = yf_ref[...] + yb_ref[...]
    mu = _group_sum(y, seg) * (1.0 / RW_HEAD_DIM)
    d = y - mu
    var = _group_sum(d * d, seg) * (1.0 / RW_HEAD_DIM)
    yn = d * lax.rsqrt(var + RW_GN_EPS) * gnw_ref[...] + gnb_ref[...]
    ya = (yn + bonus_ref[...]) * gate_ref[...]
    mix = _dot(ya.astype(BF16), w_ref[:RW_WIDTH, :]) + _dot(ob_ref[...], w_ref[RW_WIDTH:, :])
    o_ref[...] = h_ref[...] + mod_ref[2:3, :] * mix


def _mix_out(yf, yb, bonus, gate, ob, p, h, mods, n_ctx):
    b, tt, d = h.shape
    c = RW_WIDTH
    tm = ROW_TILE
    tok = lambda width: pl.BlockSpec((None, tm, width), lambda bi, i: (bi, i, 0))
    full = lambda shape: pl.BlockSpec(shape, lambda bi, i: (0,) * len(shape))
    return pl.pallas_call(
        _mix_out_kernel,
        grid=(b, tt // tm),
        in_specs=[tok(c), tok(c), tok(c), tok(c), tok(DF_WIDTH), full((1, c)), full((1, c)), full((c, c)),
                  full((c + DF_WIDTH, d)), tok(d),
                  pl.BlockSpec((None, None, 6, d), lambda bi, i: (bi, _kind(i, n_ctx), 0, 0))],
        out_specs=tok(d),
        out_shape=jax.ShapeDtypeStruct((b, tt, d), F32),
        compiler_params=_cparams(("parallel", "parallel")),
        name="mix_out_residual",
    )(yf, yb, bonus, gate, ob, p["gn_w"].reshape(1, c), p["gn_b"].reshape(1, c), _group_ones(c, RW_HEAD_DIM),
      p["w_out"].astype(BF16), h, mods)


def _sgu_kernel(h_ref, mod_ref, g_ref, w_in_ref, b_in_ref, lng_ref, lnb_ref, ws_ref, bst_ref, w_out_ref, o_ref):
    x = h_ref[...]
    tm = x.shape[0]
    y = _norm_mod(x, g_ref[...], mod_ref[1:2, :], mod_ref[0:1, :])
    z = _dot(y.astype(BF16), w_in_ref[...]) + b_in_ref[...]
    z = 0.5 * z * (1.0 + lax.erf(z * (2.0 ** -0.5)))
    u = z[:, :SGU_WIDTH]
    v = z[:, SGU_WIDTH:]
    mu = jnp.mean(v, axis=-1, keepdims=True)
    var = jnp.mean(jnp.square(v - mu), axis=-1, keepdims=True)
    v = ((v - mu) * lax.rsqrt(var + LN_EPS) * lng_ref[...] + lnb_ref[...]).astype(BF16)
    rows = []
    for ci in range(tm // SGU_CHUNK):
        cols = []
        for gi in range(SGU_GROUPS):
            vb = v[ci * SGU_CHUNK:(ci + 1) * SGU_CHUNK, gi * SGU_GROUP_DIM:(gi + 1) * SGU_GROUP_DIM]
            cols.append(_dot(ws_ref[gi], vb) + bst_ref[:, gi:gi + 1])
        rows.append(jnp.concatenate(cols, axis=1))
    sv = jnp.concatenate(rows, axis=0)
    o_ref[...] = x + mod_ref[2:3, :] * _dot((u * sv).astype(BF16), w_out_ref[...])


def _sgu_layer(h, mods, g, p, n_ctx):
    b, tt, d = h.shape
    tm = ROW_TILE
    full = lambda shape: pl.BlockSpec(shape, lambda bi, i: (0,) * len(shape))
    return pl.pallas_call(
        _sgu_kernel,
        grid=(b, tt // tm),
        in_specs=[pl.BlockSpec((None, tm, d), lambda bi, i: (bi, i, 0)),
                  pl.BlockSpec((None, None, 6, d), lambda bi, i: (bi, _kind(i, n_ctx), 0, 0)),
                  full((1, d)), full((d, 2 * SGU_WIDTH)), full((1, 2 * SGU_WIDTH)),
                  full((1, SGU_WIDTH)), full((1, SGU_WIDTH)),
                  full((SGU_GROUPS, SGU_CHUNK, SGU_CHUNK)), full((SGU_CHUNK, SGU_GROUPS)),
                  full((SGU_WIDTH, d))],
        out_specs=pl.BlockSpec((None, tm, d), lambda bi, i: (bi, i, 0)),
        out_shape=jax.ShapeDtypeStruct((b, tt, d), F32),
        compiler_params=_cparams(("parallel", "parallel")),
        name="sgu_layer",
    )(h, mods, g.reshape(1, d), p["w_in"].astype(BF16), p["b_in"].reshape(1, -1),
      p["ln_g"].reshape(1, -1), p["ln_b"].reshape(1, -1), p["ws"].astype(BF16), p["bs"].T,
      p["w_out"].astype(BF16))


def _router_kernel(h_ref, mod_ref, g_ref, rw_ref, rb_ref, f_ref, idx_ref, wt_ref, rank_ref, cnt_ref, base_ref):
    first = jnp.logical_and(pl.program_id(0) == 0, pl.program_id(1) == 0)

    @pl.when(first)
    def _():
        base_ref[...] = jnp.zeros_like(base_ref)

    x = h_ref[...]
    tm = x.shape[0]
    f = _norm_mod(x, g_ref[...], mod_ref[4:5, :], mod_ref[3:4, :])
    f_ref[...] = f.astype(f_ref.dtype)
    logits = _dot(f, rw_ref[...], HIGHEST) + rb_ref[...]
    lane = lax.broadcasted_iota(jnp.int32, (tm, LANES), 1)
    tri = (lax.broadcasted_iota(jnp.int32, (tm, tm), 1) < lax.broadcasted_iota(jnp.int32, (tm, tm), 0))
    vals, sels = [], []
    idx_out = jnp.zeros((tm, LANES), jnp.int32)
    l = logits
    for kk in range(TOP_K):
        m = jnp.max(l, axis=-1, keepdims=True)
        idx = jnp.min(jnp.where(l == m, lane, LANES), axis=-1, keepdims=True)
        sel = lane == idx
        l = jnp.where(sel, -jnp.inf, l)
        vals.append(m)
        sels.append(sel)
        idx_out = jnp.where(lane == kk, idx, idx_out)
    es = [jnp.exp(vv - vals[0]) for vv in vals]
    den = es[0] + es[1] + es[2] + es[3]
    wt_out = jnp.zeros((tm, LANES), F32)
    for kk in range(TOP_K):
        wt_out = jnp.where(lane == kk, es[kk] / den, wt_out)
    onehot = jnp.zeros((tm, LANES), F32)
    for sel in sels:
        onehot = onehot + sel.astype(F32)
    ahead = _dot(tri.astype(BF16), onehot.astype(BF16)) + base_ref[...]
    rank_out = jnp.zeros((tm, LANES), jnp.int32)
    for kk in range(TOP_K):
        rk = jnp.sum(jnp.where(sels[kk], ahead, 0.0), axis=-1, keepdims=True).astype(jnp.int32)
        rank_out = jnp.where(lane == kk, rk, rank_out)
    idx_ref[...] = idx_out
    wt_ref[...] = wt_out
    rank_ref[...] = rank_out
    base_ref[...] = base_ref[...] + jnp.sum(onehot, axis=0, keepdims=True)
    cnt_ref[...] = base_ref[...]


def _router(h, mods, g, rt_w, rt_b, n_ctx):
    b, tt, d = h.shape
    tm = ROW_TILE
    rw = jnp.zeros((d, LANES), F32).at[:, :N_EXPERTS].set(rt_w)
    rb = jnp.full((1, LANES), -jnp.inf, F32).at[0, :N_EXPERTS].set(rt_b)
    tok = lambda width: pl.BlockSpec((None, tm, width), lambda bi, i: (bi, i, 0))
    return pl.pallas_call(
        _router_kernel,
        grid=(b, tt // tm),
        in_specs=[tok(d),
                  pl.BlockSpec((None, None, 6, d), lambda bi, i: (bi, _kind(i, n_ctx), 0, 0)),
                  pl.BlockSpec((1, d), lambda bi, i: (0, 0)),
                  pl.BlockSpec((d, LANES), lambda bi, i: (0, 0)),
                  pl.BlockSpec((1, LANES), lambda bi, i: (0, 0))],
        out_specs=[tok(d), tok(LANES), tok(LANES), tok(LANES), pl.BlockSpec((1, LANES), lambda bi, i: (0, 0))],
        out_shape=[jax.ShapeDtypeStruct((b, tt, d), BF16),
                   jax.ShapeDtypeStruct((b, tt, LANES), jnp.int32),
                   jax.ShapeDtypeStruct((b, tt, LANES), F32),
                   jax.ShapeDtypeStruct((b, tt, LANES), jnp.int32),
                   jax.ShapeDtypeStruct((1, LANES), F32)],
        scratch_shapes=[pltpu.VMEM((1, LANES), F32)],
        compiler_params=_cparams(("arbitrary", "arbitrary")),
        name="moe_router",
    )(h, mods, g.reshape(1, d), rw, rb)


def _expert_kernel(te_ref, nv_ref, x_ref, gate_ref, w1_ref, b1_ref, w2_ref, b2_ref, o_ref, w1b_ref, w2b_ref):
    j = pl.program_id(0)
    active = j < nv_ref[0]
    new_expert = jnp.logical_or(j == 0, te_ref[j] != te_ref[jnp.maximum(j - 1, 0)])

    @pl.when(jnp.logical_and(active, new_expert))
    def _():
        w1b_ref[...] = w1_ref[...].astype(BF16)
        w2b_ref[...] = w2_ref[...].astype(BF16)

    @pl.when(active)
    def _():
        f = w2b_ref.shape[0]
        fh = f // 2
        x = x_ref[...]
        y = b2_ref[...]
        for c in range(2):
            lo, hi = c * fh, (c + 1) * fh
            glu = _dot(x, w1b_ref[:, lo:hi]) + b1_ref[:, lo:hi]
            lin = _dot(x, w1b_ref[:, f + lo:f + hi]) + b1_ref[:, f + lo:f + hi]
            glu = jnp.minimum(glu, SWIGLU_LIMIT)
            lin = jnp.clip(lin, -SWIGLU_LIMIT, SWIGLU_LIMIT)
            act = glu * jax.nn.sigmoid(SWIGLU_ALPHA * glu) * (lin + 1.0)
            y = y + _dot(act.astype(BF16), w2b_ref[lo:hi, :])
        o_ref[...] = (gate_ref[...] * y).astype(o_ref.dtype)

    @pl.when(jnp.logical_not(active))
    def _():
        o_ref[...] = jnp.zeros_like(o_ref)


def _experts(xs, row_gate, tile_expert, n_valid, w1, b1, w2, b2):
    s, d = xs.shape
    e, _, f2 = w1.shape
    f = f2 // 2
    tm = EXPERT_ROW_TILE
    grid_spec = pltpu.PrefetchScalarGridSpec(
        num_scalar_prefetch=2,
        grid=(s // tm,),
        in_specs=[pl.BlockSpec((tm, d), lambda j, te, nv: (j, 0)),
                  pl.BlockSpec((tm, 1), lambda j, te, nv: (j, 0)),
                  pl.BlockSpec((None, d, f2), lambda j, te, nv: (te[j], 0, 0)),
                  pl.BlockSpec((None, 1, f2), lambda j, te, nv: (te[j], 0, 0)),
                  pl.BlockSpec((None, f, d), lambda j, te, nv: (te[j], 0, 0)),
                  pl.BlockSpec((None, 1, d), lambda j, te, nv: (te[j], 0, 0))],
        out_specs=pl.BlockSpec((tm, d), lambda j, te, nv: (j, 0)),
        scratch_shapes=[pltpu.VMEM((d, f2), BF16), pltpu.VMEM((f, d), BF16)],
    )
    return pl.pallas_call(
        _expert_kernel,
        grid_spec=grid_spec,
        out_shape=jax.ShapeDtypeStruct((s, d), BF16),
        compiler_params=_cparams(("arbitrary",)),
        name="moe_experts",
    )(tile_expert, n_valid, xs, row_gate, w1, b1.reshape(e, 1, f2), w2, b2.reshape(e, 1, d))


def _moe(h, mods, g, rt_w, rt_b, w1, b1, w2, b2, n_ctx):
    b, tt, d = h.shape
    n = b * tt
    tm = EXPERT_ROW_TILE
    f, idx, wt, rank, cnt = _router(h, mods, g, rt_w, rt_b, n_ctx)
    top_i = idx.reshape(n, LANES)[:, :TOP_K]
    top_w = wt.reshape(n, LANES)[:, :TOP_K]
    rank = rank.reshape(n, LANES)[:, :TOP_K]
    counts = cnt[0, :N_EXPERTS].astype(jnp.int32)
    padded = ((counts + tm - 1) // tm) * tm
    p_end = jnp.cumsum(padded)
    p_off = p_end - padded
    u_off = jnp.cumsum(counts) - counts
    slot = (p_off[top_i] + rank).reshape(-1)
    n_tiles = (n * TOP_K) // tm + N_EXPERTS
    s_rows = n_tiles * tm
    _, sorted_pair = lax.sort_key_val(slot, jnp.arange(n * TOP_K, dtype=jnp.int32))
    srow = jnp.arange(s_rows, dtype=jnp.int32)
    row_e = jnp.minimum(jnp.searchsorted(p_end, srow, side="right"), N_EXPERTS - 1).astype(jnp.int32)
    within = srow - p_off[row_e]
    valid = within < counts[row_e]
    src = jnp.clip(u_off[row_e] + within, 0, n * TOP_K - 1)
    pair = sorted_pair[src]
    row_token = jnp.where(valid, pair // TOP_K, 0)
    row_gate = jnp.where(valid, top_w.reshape(-1)[pair], 0.0).reshape(s_rows, 1)
    n_valid = (p_end[-1] // tm).astype(jnp.int32).reshape(1)
    tile_start = jnp.arange(n_tiles, dtype=jnp.int32) * tm
    tile_expert = jnp.minimum(jnp.searchsorted(p_end, tile_start, side="right"), N_EXPERTS - 1).astype(jnp.int32)
    last_e = tile_expert[jnp.maximum(n_valid[0] - 1, 0)]
    tile_expert = jnp.where(tile_start < p_end[-1], tile_expert, last_e)
    xs = jnp.take(f.reshape(n, d), row_token, axis=0)
    ys = _experts(xs, row_gate, tile_expert, n_valid, w1, b1, w2, b2)
    m = jnp.take(ys, slot, axis=0).astype(F32).reshape(n, TOP_K, d).sum(axis=1).reshape(b, tt, d)
    gate = jnp.where((jnp.arange(tt) >= n_ctx)[None, :, None], mods[:, 1, 5][:, None, :], mods[:, 0, 5][:, None, :])
    return h + gate * m


def _rope_tables(n_tokens, n_ctx):
    rows = n_tokens // GRID_W
    row = jnp.repeat(jnp.arange(rows, dtype=F32), GRID_W)
    col = jnp.tile(jnp.arange(GRID_W, dtype=F32), rows)
    axis_dim = DF_HEAD_DIM // 2
    inv_freq = ROPE_THETA ** (-jnp.arange(0, axis_dim, 2, dtype=F32) / axis_dim)
    ar = row[:, None] * inv_freq
    ac = col[:, None] * inv_freq
    ang = jnp.concatenate([ar, ar, ac, ac], axis=-1)
    cos = jnp.concatenate([jnp.ones((n_ctx, DF_HEAD_DIM), F32), jnp.cos(ang)], axis=0)
    sin = jnp.concatenate([jnp.zeros((n_ctx, DF_HEAD_DIM), F32), jnp.sin(ang)], axis=0)
    return jnp.tile(cos, (1, 2)), jnp.tile(sin, (1, 2))


def _even_layer(h, mods, g1, p, li, n_ctx, rope):
    z = _nm_mm(h, mods, g1, p["w_in"].astype(BF16), n_ctx)
    r, v, kk, kd0, kd1, lw0, lw1, bb0, bb1, gate, bonus = _rw_features(z, p, n_ctx)
    yf = _rw_scan(r, v, kk, kd0, lw0, bb0, n_ctx, False)
    yb = _rw_scan(r, v, kk, kd1, lw1, bb1, n_ctx, True)
    lam_init = 0.8 - 0.6 * math.exp(-0.3 * li)
    lp = p["lam"].astype(F32)
    lam = jnp.exp(jnp.sum(lp[0] * lp[1])) - jnp.exp(jnp.sum(lp[2] * lp[3])) + lam_init
    ob = _diff_attention(z, p, lam, lam_init, rope[0], rope[1], n_ctx)
    return _mix_out(yf, yb, bonus, gate, ob, p, h, mods, n_ctx)


def kernel(x, c, ctx, c_ctx, norm1_g, norm2_g, ada_w, ada_b, ev_w_in, ev_w_out, rw_mu, rw_w0, rw_w2, rw_a0, rw_a2, rw_g2, rw_kk, rw_ka, rw_rk, rw_gn_w, rw_gn_b, df_qn, df_kn, df_lam, df_subln, sg_w_in, sg_b_in, sg_ln_g, sg_ln_b, sg_ws, sg_bs, sg_w_out, rt_w, rt_b, ex_w1, ex_b1, ex_w2, ex_b2):
    b, t, d = x.shape
    n_ctx = ctx.shape[1]
    depth = ada_w.shape[0]
    assert d == D_MODEL and t % ROW_TILE == 0 and n_ctx % ROW_TILE == 0 and t % GRID_W == 0
    rope = _rope_tables(t, n_ctx)
    r_pad = -(-(b + 1) // 8) * 8
    cs = jnp.zeros((r_pad, d), F32).at[:b].set(c).at[b].set(c_ctx)
    ada = _ada_table(cs, ada_w, ada_b).reshape(depth, r_pad, 6, d)
    h = jnp.concatenate([ctx, x], axis=1)
    for l in range(depth):
        ctx_out = any(m % 2 == 0 for m in range(l + 1, depth))
        j = l // 2
        mods = jnp.stack([jnp.broadcast_to(ada[l, b], (b, 6, d)), ada[l, :b]], axis=1)
        if l % 2 == 0:
            p = {"w_in": ev_w_in[j], "w_out": ev_w_out[j], "mu": rw_mu[j], "w0": rw_w0[j], "w2": rw_w2[j],
                 "a0": rw_a0[j], "a2": rw_a2[j], "g2": rw_g2[j], "k_k": rw_kk[j], "k_a": rw_ka[j],
                 "r_k": rw_rk[j], "gn_w": rw_gn_w[j], "gn_b": rw_gn_b[j], "qn": df_qn[j], "kn": df_kn[j],
                 "lam": df_lam[j], "subln": df_subln[j]}
            h = _even_layer(h, mods, norm1_g[l], p, l, n_ctx, rope)
        else:
            p = {"w_in": sg_w_in[j], "b_in": sg_b_in[j], "ln_g": sg_ln_g[j], "ln_b": sg_ln_b[j],
                 "ws": sg_ws[j], "bs": sg_bs[j], "w_out": sg_w_out[j]}
            h = _sgu_layer(h, mods, norm1_g[l], p, n_ctx)
        if n_ctx > 0 and not ctx_out:
            h = h[:, n_ctx:]
            n_ctx = 0
        h = _moe(h, mods, norm2_g[l], rt_w[l], rt_b[l], ex_w1[l], ex_b1[l], ex_w2[l], ex_b2[l], n_ctx)
    return h[:, n_ctx:]
```

```python
import functools
import math

import jax
import jax.numpy as jnp
from jax import lax
from jax.experimental import pallas as pl
from jax.experimental.pallas import tpu as pltpu

F32 = jnp.float32
BF16 = jnp.bfloat16
HIGHEST = lax.Precision.HIGHEST

D_MODEL = 1024
GRID_W = 64
RMS_EPS = 1e-6
LN_EPS = 1e-5

RW_HEAD_DIM = 64
RW_WIDTH = 512
RW_HEADS = RW_WIDTH // RW_HEAD_DIM
DECAY_LORA = 64
ICLR_LORA = 64
GATE_LORA = 128
RW_COLS = 3 * RW_WIDTH + 2 * DECAY_LORA + 2 * ICLR_LORA + GATE_LORA
RW_GN_EPS = 64e-5

DF_HEAD_DIM = 64
DF_V_DIM = 128
DF_WIDTH = 512
DF_HEADS = DF_WIDTH // DF_V_DIM
DF_QK = DF_HEADS * 2 * DF_HEAD_DIM
ROPE_THETA = 10000.0
EVEN_COLS = RW_COLS + 2 * DF_QK + DF_WIDTH

SGU_CHUNK = 128
SGU_WIDTH = 1024
SGU_GROUPS = 8
SGU_GROUP_DIM = SGU_WIDTH // SGU_GROUPS

N_EXPERTS = 32
TOP_K = 4
SWIGLU_LIMIT = 7.0
SWIGLU_ALPHA = 1.702

LANES = 128
ROW_TILE = 256
SCAN_CHUNK = 64
EXPERT_ROW_TILE = 512
VMEM_LIMIT = 56 * 1024 * 1024


def _cparams(sem):
    return pltpu.CompilerParams(dimension_semantics=sem, vmem_limit_bytes=VMEM_LIMIT)


def _dot(a, b, precision=None):
    return jnp.dot(a, b, preferred_element_type=F32, precision=precision)


def _dot_nt(a, b, precision=None):
    return lax.dot_general(a, b, (((1,), (1,)), ((), ())), preferred_element_type=F32, precision=precision)


def _dot_tn(a, b, precision=None):
    return lax.dot_general(a, b, (((0,), (0,)), ((), ())), preferred_element_type=F32, precision=precision)


def _norm_mod(x, g, scale, shift):
    ms = jnp.mean(x * x, axis=-1, keepdims=True)
    return (x * lax.rsqrt(ms + RMS_EPS) * g) * (1.0 + scale) + shift


def _ada_kernel(x_ref, w_ref, b_ref, o_ref):
    x = x_ref[...]
    s = x * jax.nn.sigmoid(x)
    o_ref[...] = _dot(s, w_ref[...], HIGHEST) + b_ref[...]


def _ada_table(cs, ada_w, ada_b):
    depth, d, n = ada_w.shape
    r = cs.shape[0]
    tn = 1024
    return pl.pallas_call(
        _ada_kernel,
        grid=(depth, n // tn),
        in_specs=[pl.BlockSpec((r, d), lambda l, j: (0, 0)),
                  pl.BlockSpec((None, d, tn), lambda l, j: (l, 0, j)),
                  pl.BlockSpec((None, 1, tn), lambda l, j: (l, 0, j))],
        out_specs=pl.BlockSpec((None, r, tn), lambda l, j: (l, 0, j)),
        out_shape=jax.ShapeDtypeStruct((depth, r, n), F32),
        compiler_params=_cparams(("parallel", "parallel")),
        name="ada_table",
    )(cs, ada_w, ada_b.reshape(depth, 1, n))


def _kind(i, n_ctx):
    return (i * ROW_TILE >= n_ctx).astype(jnp.int32) if n_ctx > 0 else 1


def _nm_mm_kernel(h_ref, mod_ref, g_ref, w_ref, o_ref):
    y = _norm_mod(h_ref[...], g_ref[...], mod_ref[1:2, :], mod_ref[0:1, :])
    o_ref[...] = _dot(y.astype(BF16), w_ref[...]).astype(o_ref.dtype)


def _nm_mm(h, mods, g, w, n_ctx, out_dtype=F32):
    b, tt, d = h.shape
    n = w.shape[1]
    tm = ROW_TILE
    return pl.pallas_call(
        _nm_mm_kernel,
        grid=(b, tt // tm),
        in_specs=[pl.BlockSpec((None, tm, d), lambda bi, i: (bi, i, 0)),
                  pl.BlockSpec((None, None, 6, d), lambda bi, i: (bi, _kind(i, n_ctx), 0, 0)),
                  pl.BlockSpec((1, d), lambda bi, i: (0, 0)),
                  pl.BlockSpec((d, n), lambda bi, i: (0, 0))],
        out_specs=pl.BlockSpec((None, tm, n), lambda bi, i: (bi, i, 0)),
        out_shape=jax.ShapeDtypeStruct((b, tt, n), out_dtype),
        compiler_params=_cparams(("parallel", "parallel")),
        name="norm_mod_proj",
    )(h, mods, g.reshape(1, d), w)


def _group_ones(width, group):
    idx = jnp.arange(width) // group
    return (idx[:, None] == idx[None, :]).astype(BF16)


def _group_sum(x, ones_bf16):
    hi = x.astype(BF16)
    lo = (x - hi.astype(F32)).astype(BF16)
    return _dot(hi, ones_bf16) + _dot(lo, ones_bf16)


def _softplus(x):
    return jnp.maximum(x, 0.0) + jnp.log1p(jnp.exp(-jnp.abs(x)))


def _rw_feat_kernel(z_ref, zprev_ref, znext_ref, mu_ref, w0_ref, w2_ref, a0_ref, a2_ref, g2_ref, kk_ref, ka_ref,
                    rk_ref, seg_ref,
                    r_ref, v_ref, kko_ref, kd0_ref, kd1_ref, lw0_ref, lw1_ref, bb0_ref, bb1_ref, g_ref, bonus_ref,
                    *, n_ctx, tt):
    c = RW_WIDTH
    tm = z_ref.shape[0]
    i = pl.program_id(1)
    z = z_ref[...]
    seq_first = jnp.logical_or(i * tm == 0, i * tm == n_ctx)
    seq_last = jnp.logical_or((i + 1) * tm == n_ctx, (i + 1) * tm == tt)
    prev_row = jnp.where(seq_first, 0.0, zprev_ref[7:8, :])
    next_row = jnp.where(seq_last, 0.0, znext_ref[0:1, :])
    rows = lax.broadcasted_iota(jnp.int32, z.shape, 0)
    zp = jnp.where(rows == 0, prev_row, pltpu.roll(z, 1, 0))
    zn = jnp.where(rows == tm - 1, next_row, pltpu.roll(z, tm - 1, 0))
    za = z + mu_ref[0:1, :] * (zp - z) + mu_ref[1:2, :] * (zn - z)
    r = za[:, :c]
    k = za[:, c:2 * c]
    v = za[:, 2 * c:3 * c]
    wl = za[:, 3 * c:3 * c + 2 * DECAY_LORA]
    al = za[:, 3 * c + 2 * DECAY_LORA:3 * c + 2 * DECAY_LORA + 2 * ICLR_LORA]
    gl = za[:, 3 * c + 2 * DECAY_LORA + 2 * ICLR_LORA:]
    r_ref[...] = r
    v_ref[...] = v
    twl = jnp.tanh(wl)
    seg = seg_ref[...]
    kkr = k * kk_ref[...]
    kk = kkr / jnp.maximum(jnp.sqrt(_group_sum(kkr * kkr, seg)), 1e-12)
    kko_ref[...] = kk
    lw_refs = (lw0_ref, lw1_ref)
    bb_refs = (bb0_ref, bb1_ref)
    kd_refs = (kd0_ref, kd1_ref)
    kd_sum = None
    for d in range(2):
        wd = w0_ref[d:d + 1, :] + _dot(twl[:, d * DECAY_LORA:(d + 1) * DECAY_LORA], w2_ref[d], HIGHEST)
        w = -_softplus(-wd) - 0.5
        lw_refs[d][...] = -jnp.exp(w)
        a = jax.nn.sigmoid(a0_ref[d:d + 1, :] + _dot(al[:, d * ICLR_LORA:(d + 1) * ICLR_LORA], a2_ref[d], HIGHEST))
        bb_refs[d][...] = kk * a
        kd = k * (1.0 + (a - 1.0) * ka_ref[...])
        kd_refs[d][...] = kd
        kd_sum = kd if kd_sum is None else kd_sum + kd
    g_ref[...] = _dot(jax.nn.sigmoid(gl), g2_ref[...], HIGHEST)
    bonus_ref[...] = _group_sum(r * kd_sum * rk_ref[...], seg) * v


def _rw_features(z, p, n_ctx):
    b, tt, _ = z.shape
    c = RW_WIDTH
    tm = ROW_TILE
    hb = tm // 8
    last_halo = tt // 8 - 1
    full = lambda shape: pl.BlockSpec(shape, lambda bi, i: (0,) * len(shape))
    out_spec = pl.BlockSpec((None, tm, c), lambda bi, i: (bi, i, 0))
    return pl.pallas_call(
        functools.partial(_rw_feat_kernel, n_ctx=n_ctx, tt=tt),
        grid=(b, tt // tm),
        in_specs=[pl.BlockSpec((None, tm, RW_COLS), lambda bi, i: (bi, i, 0)),
                  pl.BlockSpec((None, 8, RW_COLS), lambda bi, i: (bi, jnp.maximum(i * hb - 1, 0), 0)),
                  pl.BlockSpec((None, 8, RW_COLS), lambda bi, i: (bi, jnp.minimum((i + 1) * hb, last_halo), 0)),
                  full((2, RW_COLS)), full((2, c)), full((2, DECAY_LORA, c)), full((2, c)), full((2, ICLR_LORA, c)),
                  full((GATE_LORA, c)), full((1, c)), full((1, c)), full((1, c)), full((c, c))],
        out_specs=[out_spec] * 11,
        out_shape=[jax.ShapeDtypeStruct((b, tt, c), F32)] * 11,
        compiler_params=_cparams(("parallel", "parallel")),
        name="rwkv_features",
    )(z, z, z, p["mu"], p["w0"], p["w2"], p["a0"], p["a2"], p["g2"], p["k_k"].reshape(1, c), p["k_a"].reshape(1, c),
      p["r_k"].reshape(1, c), _group_ones(c, RW_HEAD_DIM))


def _cumsum_rows(tri_bf16, x):
    hi = x.astype(BF16)
    r1 = x - hi.astype(F32)
    mid = r1.astype(BF16)
    lo = (r1 - mid.astype(F32)).astype(BF16)
    return _dot(tri_bf16, hi) + _dot(tri_bf16, mid) + _dot(tri_bf16, lo)


def _scan_masks(reverse):
    L = SCAN_CHUNK
    row = lax.broadcasted_iota(jnp.int32, (L, L), 0)
    col = lax.broadcasted_iota(jnp.int32, (L, L), 1)
    incl = (col >= row) if reverse else (col <= row)
    strict = (col > row) if reverse else (col < row)
    levels = []
    for lg in range(int(math.log2(L))):
        same_pair = jnp.right_shift(row, lg + 1) == jnp.right_shift(col, lg + 1)
        other_half = jnp.right_shift(row, lg) != jnp.right_shift(col, lg)
        levels.append(jnp.logical_and(jnp.logical_and(same_pair, other_half), strict))
    return incl, strict, row == col, levels


def _scan_operands(r_ref, v_ref, kk_ref, kd_ref, lw_ref, bb_ref, incl, reverse):
    L = SCAN_CHUNK
    lw = lw_ref[...]
    cum = _cumsum_rows(incl.astype(BF16), lw)
    ctot = cum[0:1, :] if reverse else cum[L - 1:L, :]
    g_inv = jnp.exp(-cum)
    g_end = jnp.exp(ctot - cum)
    kd = kd_ref[...]
    bb = bb_ref[...]
    r_f = r_ref[...] * jnp.exp(cum)
    return dict(a=(-kk_ref[...] * jnp.exp(cum - lw)).astype(BF16), b=(bb * g_inv).astype(BF16),
                k=(kd * g_inv).astype(BF16), r_f=r_f, r=r_f.astype(BF16), b_e=(bb * g_end).astype(BF16),
                k_e=(kd * g_end).astype(BF16), v=v_ref[...].astype(BF16), g_tot=jnp.exp(ctot))


def _scan_kernel(rf_ref, vf_ref, kkf_ref, kdf_ref, lwf_ref, bbf_ref, rb_ref, vb_ref, kkb_ref, kdb_ref, lwb_ref, bbb_ref,
                 yf_ref, yb_ref, s_ref):
    L = SCAN_CHUNK
    n = RW_HEAD_DIM

    @pl.when(pl.program_id(1) == 0)
    def _():
        s_ref[...] = jnp.zeros_like(s_ref)

    masks = (_scan_masks(False), _scan_masks(True))
    ops = (_scan_operands(rf_ref, vf_ref, kkf_ref, kdf_ref, lwf_ref, bbf_ref, masks[0][0], False),
           _scan_operands(rb_ref, vb_ref, kkb_ref, kdb_ref, lwb_ref, bbb_ref, masks[1][0], True))
    y_refs = (yf_ref, yb_ref)
    chains = [(d, h) for h in range(RW_HEADS) for d in range(2)]
    cut = lambda d, h, name: ops[d][name][:, h * n:(h + 1) * n]

    gram = [_dot_nt(jnp.concatenate([cut(d, h, "a"), cut(d, h, "r")], axis=0),
                    jnp.concatenate([cut(d, h, "b"), cut(d, h, "k")], axis=0)) for d, h in chains]
    n_ab = [g[:L, :L] for g in gram]
    lower = [jnp.concatenate([jnp.where(masks[d][1], g[:L, L:], 0.0), jnp.where(masks[d][0], g[L:, L:], 0.0)],
                             axis=0).astype(BF16) for (d, h), g in zip(chains, gram)]
    m_rb = [jnp.where(masks[d][0], g[L:, :L], 0.0).astype(BF16) for (d, h), g in zip(chains, gram)]
    nv = [_dot(lo, cut(d, h, "v")) for (d, h), lo in zip(chains, lower)]
    t_inv = [jnp.where(masks[d][2], 1.0, jnp.where(masks[d][3][0], nab, 0.0)) for (d, h), nab in zip(chains, n_ab)]
    for lv in range(1, len(masks[0][3])):
        tb = [t.astype(BF16) for t in t_inv]
        half = [_dot(t, jnp.where(masks[d][3][lv], nab, 0.0).astype(BF16)).astype(BF16)
                for (d, h), t, nab in zip(chains, tb, n_ab)]
        t_inv = [t + _dot(hf, t16) for t, hf, t16 in zip(t_inv, half, tb)]
    au = [_dot(t.astype(BF16), jnp.concatenate([cut(d, h, "a"), x[:L].astype(BF16)], axis=1)).astype(BF16)
          for (d, h), t, x in zip(chains, t_inv, nv)]
    ry = [_dot(m, x) for m, x in zip(m_rb, au)]
    pq = [_dot_tn(x, cut(d, h, "b_e")) for (d, h), x in zip(chains, au)]
    vk = [_dot_tn(cut(d, h, "v"), cut(d, h, "k_e")) for d, h in chains]
    for i, (d, h) in enumerate(chains):
        sl = slice(h * n, (h + 1) * n)
        r_hat = (ops[d]["r_f"][:, sl] + ry[i][:, :n]).astype(BF16)
        y_hat = ry[i][:, n:] + nv[i][L:]
        p_mat = jnp.where(masks[d][2], ops[d]["g_tot"][:, sl], 0.0) + pq[i][:n]
        q_mat = pq[i][n:] + vk[i]
        s0b = s_ref[d, h].astype(BF16)
        y_refs[d][:, sl] = _dot_nt(r_hat, s0b) + y_hat
        s_ref[d, h] = _dot(s0b, p_mat.astype(BF16)) + q_mat


def _rw_scan(r, v, kk, kd0, lw0, bb0, kd1, lw1, bb1, n_ctx):
    b, tt, c = r.shape
    L = SCAN_CHUNK
    nc = tt // L
    ncc = n_ctx // L
    fwd = pl.BlockSpec((None, L, c), lambda bi, i: (bi, i, 0))
    bwd = pl.BlockSpec((None, L, c), lambda bi, i: (bi, jnp.where(i < ncc, ncc - 1 - i, nc - 1 - (i - ncc)), 0))
    return pl.pallas_call(
        _scan_kernel,
        grid=(b, nc),
        in_specs=[fwd] * 6 + [bwd] * 6,
        out_specs=[fwd, bwd],
        out_shape=[jax.ShapeDtypeStruct((b, tt, c), F32)] * 2,
        scratch_shapes=[pltpu.VMEM((2, RW_HEADS, RW_HEAD_DIM, RW_HEAD_DIM), F32)],
        compiler_params=_cparams(("parallel", "arbitrary")),
        name="rwkv_scan",
    )(r, v, kk, kd0, lw0, bb0, r, v, kk, kd1, lw1, bb1)


def _qk_prep(x, gain, cos, sin, ones):
    xn = x * lax.rsqrt(_group_sum(x * x, ones) * (1.0 / DF_HEAD_DIM) + RMS_EPS) * gain
    lane = lax.broadcasted_iota(jnp.int32, x.shape, 1)
    quarter = DF_HEAD_DIM // 4
    rot = jnp.where(jnp.bitwise_and(lane, 2 * quarter - 1) < quarter, -pltpu.roll(xn, x.shape[1] - quarter, 1), pltpu.roll(xn, quarter, 1))
    return xn * cos + rot * sin


def _attn_kernel(lam_ref, q_ref, k_ref, v_ref, cosq_ref, sinq_ref, cosk_ref, sink_ref, qn_ref, kn_ref, sub_ref, ones_ref,
                 o_ref, kb_ref, vb_ref, *, n_ctx, tq, out_scale):
    e = DF_HEAD_DIM
    tt = k_ref.shape[0]
    qi = pl.program_id(2)
    ones = ones_ref[...]

    @pl.when(qi == 0)
    def _():
        kb_ref[...] = _qk_prep(k_ref[...], kn_ref[...], cosk_ref[...], sink_ref[...], ones).astype(BF16)
        vb_ref[...] = v_ref[...].astype(BF16)

    q = (_qk_prep(q_ref[...], qn_ref[...], cosq_ref[...], sinq_ref[...], ones) * (DF_HEAD_DIM ** -0.5)).astype(BF16)
    k = kb_ref[...]
    v = vb_ref[...]
    key_pos = lax.broadcasted_iota(jnp.int32, (tq, tt), 1)
    kv_len = jnp.where(qi * tq < n_ctx, n_ctx, tt)
    visible = key_pos < kv_len
    outs = []
    for m in range(2):
        s = _dot_nt(q[:, m * e:(m + 1) * e], k[:, m * e:(m + 1) * e])
        s = jnp.where(visible, s, -jnp.inf)
        p = jnp.exp(s - jnp.max(s, axis=-1, keepdims=True))
        l = jnp.sum(p, axis=-1, keepdims=True)
        outs.append(_dot(p.astype(BF16), v) / l)
    o = outs[0] - lam_ref[0] * outs[1]
    o = o * lax.rsqrt(jnp.mean(o * o, axis=-1, keepdims=True) + RMS_EPS) * sub_ref[...] * out_scale
    o_ref[...] = o.astype(o_ref.dtype)


def _diff_attention(z, p, lam, lam_init, cos, sin, n_ctx):
    b, tt, _ = z.shape
    tq = ROW_TILE
    w = DF_V_DIM
    q0 = RW_COLS // w
    k0 = q0 + DF_HEADS
    v0 = k0 + DF_HEADS
    two = lambda g: jnp.tile(g, 2).reshape(1, w)
    full = lambda shape: pl.BlockSpec(shape, lambda bi, hi, i: (0,) * len(shape))
    return pl.pallas_call(
        functools.partial(_attn_kernel, n_ctx=n_ctx, tq=tq, out_scale=1.0 - lam_init),
        grid=(b, DF_HEADS, tt // tq),
        in_specs=[pl.BlockSpec(memory_space=pltpu.SMEM),
                  pl.BlockSpec((None, tq, w), lambda bi, hi, i: (bi, i, q0 + hi)),
                  pl.BlockSpec((None, tt, w), lambda bi, hi, i: (bi, 0, k0 + hi)),
                  pl.BlockSpec((None, tt, w), lambda bi, hi, i: (bi, 0, v0 + hi)),
                  pl.BlockSpec((tq, w), lambda bi, hi, i: (i, 0)),
                  pl.BlockSpec((tq, w), lambda bi, hi, i: (i, 0)),
                  full((tt, w)), full((tt, w)), full((1, w)), full((1, w)), full((1, w)), full((w, w))],
        out_specs=pl.BlockSpec((None, tq, w), lambda bi, hi, i: (bi, i, hi)),
        out_shape=jax.ShapeDtypeStruct((b, tt, DF_WIDTH), BF16),
        scratch_shapes=[pltpu.VMEM((tt, w), BF16), pltpu.VMEM((tt, w), BF16)],
        compiler_params=_cparams(("parallel", "parallel", "arbitrary")),
        name="diff_attention",
    )(lam.reshape(1).astype(F32), z, z, z, cos, sin, cos, sin, two(p["qn"]), two(p["kn"]),
      p["subln"].reshape(1, w), _group_ones(w, DF_HEAD_DIM))


def _mix_out_kernel(yf_ref, yb_ref, bonus_ref, gate_ref, ob_ref, gnw_ref, gnb_ref, seg_ref, w_ref, h_ref, mod_ref, o_ref):
    seg = seg_ref[...]
    y = yf_ref[...] + yb_ref[...]
    mu = _group_sum(y, seg) * (1.0 / RW_HEAD_DIM)
    d = y - mu
    var = _group_sum(d * d, seg) * (1.0 / RW_HEAD_DIM)
    yn = d * lax.rsqrt(var + RW_GN_EPS) * gnw_ref[...] + gnb_ref[...]
    ya = (yn + bonus_ref[...]) * gate_ref[...]
    mix = _dot(ya.astype(BF16), w_ref[:RW_WIDTH, :]) + _dot(ob_ref[...], w_ref[RW_WIDTH:, :])
    o_ref[...] = h_ref[...] + mod_ref[2:3, :] * mix


def _mix_out(yf, yb, bonus, gate, ob, p, h, mods, n_ctx):
    b, tt, d = h.shape
    c = RW_WIDTH
    tm = ROW_TILE
    tok = lambda width: pl.BlockSpec((None, tm, width), lambda bi, i: (bi, i, 0))
    full = lambda shape: pl.BlockSpec(shape, lambda bi, i: (0,) * len(shape))
    return pl.pallas_call(
        _mix_out_kernel,
        grid=(b, tt // tm),
        in_specs=[tok(c), tok(c), tok(c), tok(c), tok(DF_WIDTH), full((1, c)), full((1, c)), full((c, c)),
                  full((c + DF_WIDTH, d)), tok(d),
                  pl.BlockSpec((None, None, 6, d), lambda bi, i: (bi, _kind(i, n_ctx), 0, 0))],
        out_specs=tok(d),
        out_shape=jax.ShapeDtypeStruct((b, tt, d), F32),
        compiler_params=_cparams(("parallel", "parallel")),
        name="mix_out_residual",
    )(yf, yb, bonus, gate, ob, p["gn_w"].reshape(1, c), p["gn_b"].reshape(1, c), _group_ones(c, RW_HEAD_DIM),
      p["w_out"].astype(BF16), h, mods)


def _sgu_kernel(h_ref, mod_ref, g_ref, w_in_ref, b_in_ref, lng_ref, lnb_ref, ws_ref, bst_ref, w_out_ref, o_ref):
    x = h_ref[...]
    tm = x.shape[0]
    y = _norm_mod(x, g_ref[...], mod_ref[1:2, :], mod_ref[0:1, :])
    z = _dot(y.astype(BF16), w_in_ref[...]) + b_in_ref[...]
    z = 0.5 * z * (1.0 + lax.erf(z * (2.0 ** -0.5)))
    u = z[:, :SGU_WIDTH]
    v = z[:, SGU_WIDTH:]
    mu = jnp.mean(v, axis=-1, keepdims=True)
    var = jnp.mean(jnp.square(v - mu), axis=-1, keepdims=True)
    v = ((v - mu) * lax.rsqrt(var + LN_EPS) * lng_ref[...] + lnb_ref[...]).astype(BF16)
    rows = []
    for ci in range(tm // SGU_CHUNK):
        cols = []
        for gi in range(SGU_GROUPS):
            vb = v[ci * SGU_CHUNK:(ci + 1) * SGU_CHUNK, gi * SGU_GROUP_DIM:(gi + 1) * SGU_GROUP_DIM]
            cols.append(_dot(ws_ref[gi], vb) + bst_ref[:, gi:gi + 1])
        rows.append(jnp.concatenate(cols, axis=1))
    sv = jnp.concatenate(rows, axis=0)
    o_ref[...] = x + mod_ref[2:3, :] * _dot((u * sv).astype(BF16), w_out_ref[...])


def _sgu_layer(h, mods, g, p, n_ctx):
    b, tt, d = h.shape
    tm = ROW_TILE
    full = lambda shape: pl.BlockSpec(shape, lambda bi, i: (0,) * len(shape))
    return pl.pallas_call(
        _sgu_kernel,
        grid=(b, tt // tm),
        in_specs=[pl.BlockSpec((None, tm, d), lambda bi, i: (bi, i, 0)),
                  pl.BlockSpec((None, None, 6, d), lambda bi, i: (bi, _kind(i, n_ctx), 0, 0)),
                  full((1, d)), full((d, 2 * SGU_WIDTH)), full((1, 2 * SGU_WIDTH)),
                  full((1, SGU_WIDTH)), full((1, SGU_WIDTH)),
                  full((SGU_GROUPS, SGU_CHUNK, SGU_CHUNK)), full((SGU_CHUNK, SGU_GROUPS)),
                  full((SGU_WIDTH, d))],
        out_specs=pl.BlockSpec((None, tm, d), lambda bi, i: (bi, i, 0)),
        out_shape=jax.ShapeDtypeStruct((b, tt, d), F32),
        compiler_params=_cparams(("parallel", "parallel")),
        name="sgu_layer",
    )(h, mods, g.reshape(1, d), p["w_in"].astype(BF16), p["b_in"].reshape(1, -1),
      p["ln_g"].reshape(1, -1), p["ln_b"].reshape(1, -1), p["ws"].astype(BF16), p["bs"].T,
      p["w_out"].astype(BF16))


def _router_kernel(h_ref, mod_ref, g_ref, rw_ref, rb_ref, f_ref, idx_ref, wt_ref, rank_ref, cnt_ref, base_ref):
    first = jnp.logical_and(pl.program_id(0) == 0, pl.program_id(1) == 0)

    @pl.when(first)
    def _():
        base_ref[...] = jnp.zeros_like(base_ref)

    x = h_ref[...]
    tm = x.shape[0]
    f = _norm_mod(x, g_ref[...], mod_ref[4:5, :], mod_ref[3:4, :])
    f_ref[...] = f.astype(f_ref.dtype)
    logits = _dot(f, rw_ref[...], HIGHEST) + rb_ref[...]
    lane = lax.broadcasted_iota(jnp.int32, (tm, LANES), 1)
    tri = (lax.broadcasted_iota(jnp.int32, (tm, tm), 1) < lax.broadcasted_iota(jnp.int32, (tm, tm), 0))
    vals, sels = [], []
    idx_out = jnp.zeros((tm, LANES), jnp.int32)
    l = logits
    for kk in range(TOP_K):
        m = jnp.max(l, axis=-1, keepdims=True)
        idx = jnp.min(jnp.where(l == m, lane, LANES), axis=-1, keepdims=True)
        sel = lane == idx
        l = jnp.where(sel, -jnp.inf, l)
        vals.append(m)
        sels.append(sel)
        idx_out = jnp.where(lane == kk, idx, idx_out)
    es = [jnp.exp(vv - vals[0]) for vv in vals]
    den = es[0] + es[1] + es[2] + es[3]
    wt_out = jnp.zeros((tm, LANES), F32)
    for kk in range(TOP_K):
        wt_out = jnp.where(lane == kk, es[kk] / den, wt_out)
    onehot = jnp.zeros((tm, LANES), F32)
    for sel in sels:
        onehot = onehot + sel.astype(F32)
    ahead = _dot(tri.astype(BF16), onehot.astype(BF16)) + base_ref[...]
    rank_out = jnp.zeros((tm, LANES), jnp.int32)
    for kk in range(TOP_K):
        rk = jnp.sum(jnp.where(sels[kk], ahead, 0.0), axis=-1, keepdims=True).astype(jnp.int32)
        rank_out = jnp.where(lane == kk, rk, rank_out)
    idx_ref[...] = idx_out
    wt_ref[...] = wt_out
    rank_ref[...] = rank_out
    base_ref[...] = base_ref[...] + jnp.sum(onehot, axis=0, keepdims=True)
    cnt_ref[...] = base_ref[...]


def _router(h, mods, g, rt_w, rt_b, n_ctx):
    b, tt, d = h.shape
    tm = ROW_TILE
    rw = jnp.zeros((d, LANES), F32).at[:, :N_EXPERTS].set(rt_w)
    rb = jnp.full((1, LANES), -jnp.inf, F32).at[0, :N_EXPERTS].set(rt_b)
    tok = lambda width: pl.BlockSpec((None, tm, width), lambda bi, i: (bi, i, 0))
    return pl.pallas_call(
        _router_kernel,
        grid=(b, tt // tm),
        in_specs=[tok(d),
                  pl.BlockSpec((None, None, 6, d), lambda bi, i: (bi, _kind(i, n_ctx), 0, 0)),
                  pl.BlockSpec((1, d), lambda bi, i: (0, 0)),
                  pl.BlockSpec((d, LANES), lambda bi, i: (0, 0)),
                  pl.BlockSpec((1, LANES), lambda bi, i: (0, 0))],
        out_specs=[tok(d), tok(LANES), tok(LANES), tok(LANES), pl.BlockSpec((1, LANES), lambda bi, i: (0, 0))],
        out_shape=[jax.ShapeDtypeStruct((b, tt, d), BF16),
                   jax.ShapeDtypeStruct((b, tt, LANES), jnp.int32),
                   jax.ShapeDtypeStruct((b, tt, LANES), F32),
                   jax.ShapeDtypeStruct((b, tt, LANES), jnp.int32),
                   jax.ShapeDtypeStruct((1, LANES), F32)],
        scratch_shapes=[pltpu.VMEM((1, LANES), F32)],
        compiler_params=_cparams(("arbitrary", "arbitrary")),
        name="moe_router",
    )(h, mods, g.reshape(1, d), rw, rb)


def _expert_kernel(te_ref, nv_ref, x_ref, w1_ref, b1_ref, w2_ref, b2_ref, o_ref, w1b_ref, w2b_ref):
    j = pl.program_id(0)
    active = j < nv_ref[0]
    new_expert = jnp.logical_or(j == 0, te_ref[j] != te_ref[jnp.maximum(j - 1, 0)])

    @pl.when(jnp.logical_and(active, new_expert))
    def _():
        w1b_ref[...] = w1_ref[...].astype(BF16)
        w2b_ref[...] = w2_ref[...].astype(BF16)

    @pl.when(active)
    def _():
        f = w2b_ref.shape[0]
        fh = f // 2
        x = x_ref[...]
        y = b2_ref[...]
        for c in range(2):
            lo, hi = c * fh, (c + 1) * fh
            glu = _dot(x, w1b_ref[:, lo:hi]) + b1_ref[:, lo:hi]
            lin = _dot(x, w1b_ref[:, f + lo:f + hi]) + b1_ref[:, f + lo:f + hi]
            glu = jnp.minimum(glu, SWIGLU_LIMIT)
            lin = jnp.clip(lin, -SWIGLU_LIMIT, SWIGLU_LIMIT)
            act = glu * jax.nn.sigmoid(SWIGLU_ALPHA * glu) * (lin + 1.0)
            y = y + _dot(act.astype(BF16), w2b_ref[lo:hi, :])
        o_ref[...] = y.astype(o_ref.dtype)

    @pl.when(jnp.logical_not(active))
    def _():
        o_ref[...] = jnp.zeros_like(o_ref)


def _experts(xs, tile_expert, n_valid, w1, b1, w2, b2, layer):
    s, d = xs.shape
    depth, e, _, f2 = w1.shape
    f = f2 // 2
    tm = EXPERT_ROW_TILE
    grid_spec = pltpu.PrefetchScalarGridSpec(
        num_scalar_prefetch=2,
        grid=(s // tm,),
        in_specs=[pl.BlockSpec((tm, d), lambda j, te, nv: (j, 0)),
                  pl.BlockSpec((None, None, d, f2), lambda j, te, nv: (layer, te[j], 0, 0)),
                  pl.BlockSpec((None, None, 1, f2), lambda j, te, nv: (layer, te[j], 0, 0)),
                  pl.BlockSpec((None, None, f, d), lambda j, te, nv: (layer, te[j], 0, 0)),
                  pl.BlockSpec((None, None, 1, d), lambda j, te, nv: (layer, te[j], 0, 0))],
        out_specs=pl.BlockSpec((tm, d), lambda j, te, nv: (j, 0)),
        scratch_shapes=[pltpu.VMEM((d, f2), BF16), pltpu.VMEM((f, d), BF16)],
    )
    return pl.pallas_call(
        _expert_kernel,
        grid_spec=grid_spec,
        out_shape=jax.ShapeDtypeStruct((s, d), BF16),
        compiler_params=_cparams(("arbitrary",)),
        name="moe_experts",
    )(tile_expert, n_valid, xs, w1, b1.reshape(depth, e, 1, f2), w2, b2.reshape(depth, e, 1, d))


def _moe(h, mods, g, rt_w, rt_b, w1, b1, w2, b2, layer, n_ctx):
    b, tt, d = h.shape
    n = b * tt
    tm = EXPERT_ROW_TILE
    f, idx, wt, rank, cnt = _router(h, mods, g, rt_w, rt_b, n_ctx)
    top_i = idx.reshape(n, LANES)[:, :TOP_K]
    top_w = wt.reshape(n, LANES)[:, :TOP_K]
    rank = rank.reshape(n, LANES)[:, :TOP_K]
    counts = cnt[0, :N_EXPERTS].astype(jnp.int32)
    padded = ((counts + tm - 1) // tm) * tm
    p_end = jnp.cumsum(padded)
    p_off = p_end - padded
    u_off = jnp.cumsum(counts) - counts
    slot = (p_off[top_i] + rank).reshape(-1)
    n_tiles = (n * TOP_K) // tm + N_EXPERTS
    s_rows = n_tiles * tm
    _, sorted_pair = lax.sort_key_val(slot, jnp.arange(n * TOP_K, dtype=jnp.int32))
    srow = jnp.arange(s_rows, dtype=jnp.int32)
    row_e = jnp.minimum(jnp.searchsorted(p_end, srow, side="right"), N_EXPERTS - 1).astype(jnp.int32)
    within = srow - p_off[row_e]
    valid = within < counts[row_e]
    src = jnp.clip(u_off[row_e] + within, 0, n * TOP_K - 1)
    pair = sorted_pair[src]
    row_token = jnp.where(valid, pair // TOP_K, 0)
    n_valid = (p_end[-1] // tm).astype(jnp.int32).reshape(1)
    tile_start = jnp.arange(n_tiles, dtype=jnp.int32) * tm
    tile_expert = jnp.minimum(jnp.searchsorted(p_end, tile_start, side="right"), N_EXPERTS - 1).astype(jnp.int32)
    last_e = tile_expert[jnp.maximum(n_valid[0] - 1, 0)]
    tile_expert = jnp.where(tile_start < p_end[-1], tile_expert, last_e)
    xs = jnp.take(f.reshape(n, d), row_token, axis=0)
    ys = _experts(xs, tile_expert, n_valid, w1, b1, w2, b2, layer)
    picked = jnp.take(ys, slot.reshape(n, TOP_K).T.reshape(-1), axis=0).reshape(TOP_K, n, d)
    m = jnp.sum(picked.astype(F32) * top_w.T[:, :, None], axis=0).reshape(b, tt, d)
    gate = jnp.where((jnp.arange(tt) >= n_ctx)[None, :, None], mods[:, 1, 5][:, None, :], mods[:, 0, 5][:, None, :])
    return h + gate * m


def _rope_tables(n_tokens, n_ctx):
    rows = n_tokens // GRID_W
    row = jnp.repeat(jnp.arange(rows, dtype=F32), GRID_W)
    col = jnp.tile(jnp.arange(GRID_W, dtype=F32), rows)
    axis_dim = DF_HEAD_DIM // 2
    inv_freq = ROPE_THETA ** (-jnp.arange(0, axis_dim, 2, dtype=F32) / axis_dim)
    ar = row[:, None] * inv_freq
    ac = col[:, None] * inv_freq
    ang = jnp.concatenate([ar, ar, ac, ac], axis=-1)
    cos = jnp.concatenate([jnp.ones((n_ctx, DF_HEAD_DIM), F32), jnp.cos(ang)], axis=0)
    sin = jnp.concatenate([jnp.zeros((n_ctx, DF_HEAD_DIM), F32), jnp.sin(ang)], axis=0)
    return jnp.tile(cos, (1, 2)), jnp.tile(sin, (1, 2))


def _even_layer(h, mods, g1, p, li, n_ctx, rope):
    z = _nm_mm(h, mods, g1, p["w_in"].astype(BF16), n_ctx)
    r, v, kk, kd0, kd1, lw0, lw1, bb0, bb1, gate, bonus = _rw_features(z, p, n_ctx)
    yf, yb = _rw_scan(r, v, kk, kd0, lw0, bb0, kd1, lw1, bb1, n_ctx)
    lam_init = 0.8 - 0.6 * math.exp(-0.3 * li)
    lp = p["lam"].astype(F32)
    lam = jnp.exp(jnp.sum(lp[0] * lp[1])) - jnp.exp(jnp.sum(lp[2] * lp[3])) + lam_init
    ob = _diff_attention(z, p, lam, lam_init, rope[0], rope[1], n_ctx)
    return _mix_out(yf, yb, bonus, gate, ob, p, h, mods, n_ctx)


def kernel(x, c, ctx, c_ctx, norm1_g, norm2_g, ada_w, ada_b, ev_w_in, ev_w_out, rw_mu, rw_w0, rw_w2, rw_a0, rw_a2, rw_g2, rw_kk, rw_ka, rw_rk, rw_gn_w, rw_gn_b, df_qn, df_kn, df_lam, df_subln, sg_w_in, sg_b_in, sg_ln_g, sg_ln_b, sg_ws, sg_bs, sg_w_out, rt_w, rt_b, ex_w1, ex_b1, ex_w2, ex_b2):
    b, t, d = x.shape
    n_ctx = ctx.shape[1]
    depth = ada_w.shape[0]
    assert d == D_MODEL and t % ROW_TILE == 0 and n_ctx % ROW_TILE == 0 and t % GRID_W == 0
    rope = _rope_tables(t, n_ctx)
    r_pad = -(-(b + 1) // 8) * 8
    cs = jnp.zeros((r_pad, d), F32).at[:b].set(c).at[b].set(c_ctx)
    ada = _ada_table(cs, ada_w, ada_b).reshape(depth, r_pad, 6, d)
    h = jnp.concatenate([ctx, x], axis=1)
    for l in range(depth):
        ctx_out = any(m % 2 == 0 for m in range(l + 1, depth))
        j = l // 2
        mods = jnp.stack([jnp.broadcast_to(ada[l, b], (b, 6, d)), ada[l, :b]], axis=1)
        if l % 2 == 0:
            p = {"w_in": ev_w_in[j], "w_out": ev_w_out[j], "mu": rw_mu[j], "w0": rw_w0[j], "w2": rw_w2[j],
                 "a0": rw_a0[j], "a2": rw_a2[j], "g2": rw_g2[j], "k_k": rw_kk[j], "k_a": rw_ka[j],
                 "r_k": rw_rk[j], "gn_w": rw_gn_w[j], "gn_b": rw_gn_b[j], "qn": df_qn[j], "kn": df_kn[j],
                 "lam": df_lam[j], "subln": df_subln[j]}
            h = _even_layer(h, mods, norm1_g[l], p, l, n_ctx, rope)
        else:
            p = {"w_in": sg_w_in[j], "b_in": sg_b_in[j], "ln_g": sg_ln_g[j], "ln_b": sg_ln_b[j],
                 "ws": sg_ws[j], "bs": sg_bs[j], "w_out": sg_w_out[j]}
            h = _sgu_layer(h, mods, norm1_g[l], p, n_ctx)
        if n_ctx > 0 and not ctx_out:
            h = h[:, n_ctx:]
            n_ctx = 0
        h = _moe(h, mods, norm2_g[l], rt_w[l], rt_b[l], ex_w1, ex_b1, ex_w2, ex_b2, l, n_ctx)
    return h[:, n_ctx:]
```

```python
import functools
import math

import jax
import jax.numpy as jnp
from jax import lax
from jax.experimental import pallas as pl
from jax.experimental.pallas import tpu as pltpu

F32 = jnp.float32
BF16 = jnp.bfloat16
HIGHEST = lax.Precision.HIGHEST

D_MODEL = 1024
GRID_W = 64
RMS_EPS = 1e-6
LN_EPS = 1e-5

RW_HEAD_DIM = 64
RW_WIDTH = 512
RW_HEADS = RW_WIDTH // RW_HEAD_DIM
DECAY_LORA = 64
ICLR_LORA = 64
GATE_LORA = 128
RW_COLS = 3 * RW_WIDTH + 2 * DECAY_LORA + 2 * ICLR_LORA + GATE_LORA
RW_GN_EPS = 64e-5

DF_HEAD_DIM = 64
DF_V_DIM = 128
DF_WIDTH = 512
DF_HEADS = DF_WIDTH // DF_V_DIM
DF_QK = DF_HEADS * 2 * DF_HEAD_DIM
ROPE_THETA = 10000.0
EVEN_COLS = RW_COLS + 2 * DF_QK + DF_WIDTH

SGU_CHUNK = 128
SGU_WIDTH = 1024
SGU_GROUPS = 8
SGU_GROUP_DIM = SGU_WIDTH // SGU_GROUPS

N_EXPERTS = 32
TOP_K = 4
SWIGLU_LIMIT = 7.0
SWIGLU_ALPHA = 1.702

LANES = 128
ROW_TILE = 256
SCAN_CHUNK = 64
EXPERT_ROW_TILE = 512
VMEM_LIMIT = 56 * 1024 * 1024


def _cparams(sem):
    return pltpu.CompilerParams(dimension_semantics=sem, vmem_limit_bytes=VMEM_LIMIT)


def _dot(a, b, precision=None):
    return jnp.dot(a, b, preferred_element_type=F32, precision=precision)


def _dot_nt(a, b, precision=None):
    return lax.dot_general(a, b, (((1,), (1,)), ((), ())), preferred_element_type=F32, precision=precision)


def _dot_tn(a, b, precision=None):
    return lax.dot_general(a, b, (((0,), (0,)), ((), ())), preferred_element_type=F32, precision=precision)


def _norm_mod(x, g, scale, shift):
    ms = jnp.mean(x * x, axis=-1, keepdims=True)
    return (x * lax.rsqrt(ms + RMS_EPS) * g) * (1.0 + scale) + shift


def _ada_kernel(x_ref, w_ref, b_ref, o_ref):
    x = x_ref[...]
    s = x * jax.nn.sigmoid(x)
    o_ref[...] = _dot(s, w_ref[...], HIGHEST) + b_ref[...]


def _ada_table(cs, ada_w, ada_b):
    depth, d, n = ada_w.shape
    r = cs.shape[0]
    tn = 1024
    return pl.pallas_call(
        _ada_kernel,
        grid=(depth, n // tn),
        in_specs=[pl.BlockSpec((r, d), lambda l, j: (0, 0)),
                  pl.BlockSpec((None, d, tn), lambda l, j: (l, 0, j)),
                  pl.BlockSpec((None, 1, tn), lambda l, j: (l, 0, j))],
        out_specs=pl.BlockSpec((None, r, tn), lambda l, j: (l, 0, j)),
        out_shape=jax.ShapeDtypeStruct((depth, r, n), F32),
        compiler_params=_cparams(("parallel", "parallel")),
        name="ada_table",
    )(cs, ada_w, ada_b.reshape(depth, 1, n))


def _kind(i, n_ctx):
    return (i * ROW_TILE >= n_ctx).astype(jnp.int32) if n_ctx > 0 else 1


def _nm_mm_kernel(h_ref, mod_ref, g_ref, w_ref, o_ref):
    y = _norm_mod(h_ref[...], g_ref[...], mod_ref[1:2, :], mod_ref[0:1, :])
    o_ref[...] = _dot(y.astype(BF16), w_ref[...]).astype(o_ref.dtype)


def _nm_mm(h, mods, g, w, n_ctx, out_dtype=F32):
    b, tt, d = h.shape
    n = w.shape[1]
    tm = ROW_TILE
    return pl.pallas_call(
        _nm_mm_kernel,
        grid=(b, tt // tm),
        in_specs=[pl.BlockSpec((None, tm, d), lambda bi, i: (bi, i, 0)),
                  pl.BlockSpec((None, None, 6, d), lambda bi, i: (bi, _kind(i, n_ctx), 0, 0)),
                  pl.BlockSpec((1, d), lambda bi, i: (0, 0)),
                  pl.BlockSpec((d, n), lambda bi, i: (0, 0))],
        out_specs=pl.BlockSpec((None, tm, n), lambda bi, i: (bi, i, 0)),
        out_shape=jax.ShapeDtypeStruct((b, tt, n), out_dtype),
        compiler_params=_cparams(("parallel", "parallel")),
        name="norm_mod_proj",
    )(h, mods, g.reshape(1, d), w)


def _group_ones(width, group):
    idx = jnp.arange(width) // group
    return (idx[:, None] == idx[None, :]).astype(BF16)


def _group_sum(x, ones_bf16):
    hi = x.astype(BF16)
    lo = (x - hi.astype(F32)).astype(BF16)
    return _dot(hi, ones_bf16) + _dot(lo, ones_bf16)


def _softplus(x):
    return jnp.maximum(x, 0.0) + jnp.log1p(jnp.exp(-jnp.abs(x)))


def _rw_feat_kernel(z_ref, zprev_ref, znext_ref, mu_ref, w0_ref, w2_ref, a0_ref, a2_ref, g2_ref, kk_ref, ka_ref,
                    rk_ref, seg_ref,
                    r_ref, v_ref, kko_ref, kd0_ref, kd1_ref, lw0_ref, lw1_ref, bb0_ref, bb1_ref, g_ref, bonus_ref,
                    *, n_ctx, tt):
    c = RW_WIDTH
    tm = z_ref.shape[0]
    i = pl.program_id(1)
    z = z_ref[...]
    seq_first = jnp.logical_or(i * tm == 0, i * tm == n_ctx)
    seq_last = jnp.logical_or((i + 1) * tm == n_ctx, (i + 1) * tm == tt)
    prev_row = jnp.where(seq_first, 0.0, zprev_ref[7:8, :])
    next_row = jnp.where(seq_last, 0.0, znext_ref[0:1, :])
    rows = lax.broadcasted_iota(jnp.int32, z.shape, 0)
    zp = jnp.where(rows == 0, prev_row, pltpu.roll(z, 1, 0))
    zn = jnp.where(rows == tm - 1, next_row, pltpu.roll(z, tm - 1, 0))
    za = z + mu_ref[0:1, :] * (zp - z) + mu_ref[1:2, :] * (zn - z)
    r = za[:, :c]
    k = za[:, c:2 * c]
    v = za[:, 2 * c:3 * c]
    wl = za[:, 3 * c:3 * c + 2 * DECAY_LORA]
    al = za[:, 3 * c + 2 * DECAY_LORA:3 * c + 2 * DECAY_LORA + 2 * ICLR_LORA]
    gl = za[:, 3 * c + 2 * DECAY_LORA + 2 * ICLR_LORA:]
    r_ref[...] = r
    v_ref[...] = v
    twl = jnp.tanh(wl)
    seg = seg_ref[...]
    kkr = k * kk_ref[...]
    kk = kkr / jnp.maximum(jnp.sqrt(_group_sum(kkr * kkr, seg)), 1e-12)
    kko_ref[...] = kk
    lw_refs = (lw0_ref, lw1_ref)
    bb_refs = (bb0_ref, bb1_ref)
    kd_refs = (kd0_ref, kd1_ref)
    kd_sum = None
    for d in range(2):
        wd = w0_ref[d:d + 1, :] + _dot(twl[:, d * DECAY_LORA:(d + 1) * DECAY_LORA], w2_ref[d], HIGHEST)
        w = -_softplus(-wd) - 0.5
        lw_refs[d][...] = -jnp.exp(w)
        a = jax.nn.sigmoid(a0_ref[d:d + 1, :] + _dot(al[:, d * ICLR_LORA:(d + 1) * ICLR_LORA], a2_ref[d], HIGHEST))
        bb_refs[d][...] = kk * a
        kd = k * (1.0 + (a - 1.0) * ka_ref[...])
        kd_refs[d][...] = kd
        kd_sum = kd if kd_sum is None else kd_sum + kd
    g_ref[...] = _dot(jax.nn.sigmoid(gl), g2_ref[...], HIGHEST)
    bonus_ref[...] = _group_sum(r * kd_sum * rk_ref[...], seg) * v


def _rw_features(z, p, n_ctx):
    b, tt, _ = z.shape
    c = RW_WIDTH
    tm = ROW_TILE
    hb = tm // 8
    last_halo = tt // 8 - 1
    full = lambda shape: pl.BlockSpec(shape, lambda bi, i: (0,) * len(shape))
    out_spec = pl.BlockSpec((None, tm, c), lambda bi, i: (bi, i, 0))
    return pl.pallas_call(
        functools.partial(_rw_feat_kernel, n_ctx=n_ctx, tt=tt),
        grid=(b, tt // tm),
        in_specs=[pl.BlockSpec((None, tm, RW_COLS), lambda bi, i: (bi, i, 0)),
                  pl.BlockSpec((None, 8, RW_COLS), lambda bi, i: (bi, jnp.maximum(i * hb - 1, 0), 0)),
                  pl.BlockSpec((None, 8, RW_COLS), lambda bi, i: (bi, jnp.minimum((i + 1) * hb, last_halo), 0)),
                  full((2, RW_COLS)), full((2, c)), full((2, DECAY_LORA, c)), full((2, c)), full((2, ICLR_LORA, c)),
                  full((GATE_LORA, c)), full((1, c)), full((1, c)), full((1, c)), full((c, c))],
        out_specs=[out_spec] * 11,
        out_shape=[jax.ShapeDtypeStruct((b, tt, c), F32)] * 11,
        compiler_params=_cparams(("parallel", "parallel")),
        name="rwkv_features",
    )(z, z, z, p["mu"], p["w0"], p["w2"], p["a0"], p["a2"], p["g2"], p["k_k"].reshape(1, c), p["k_a"].reshape(1, c),
      p["r_k"].reshape(1, c), _group_ones(c, RW_HEAD_DIM))


def _cumsum_rows(tri_bf16, x):
    hi = x.astype(BF16)
    r1 = x - hi.astype(F32)
    mid = r1.astype(BF16)
    lo = (r1 - mid.astype(F32)).astype(BF16)
    return _dot(tri_bf16, hi) + _dot(tri_bf16, mid) + _dot(tri_bf16, lo)


def _scan_masks(reverse):
    L = SCAN_CHUNK
    row = lax.broadcasted_iota(jnp.int32, (L, L), 0)
    col = lax.broadcasted_iota(jnp.int32, (L, L), 1)
    incl = (col >= row) if reverse else (col <= row)
    strict = (col > row) if reverse else (col < row)
    levels = []
    for lg in range(int(math.log2(L))):
        same_pair = jnp.right_shift(row, lg + 1) == jnp.right_shift(col, lg + 1)
        other_half = jnp.right_shift(row, lg) != jnp.right_shift(col, lg)
        levels.append(jnp.logical_and(jnp.logical_and(same_pair, other_half), strict))
    return incl, strict, row == col, levels


def _scan_operands(r_ref, v_ref, kk_ref, kd_ref, lw_ref, bb_ref, incl, reverse):
    L = SCAN_CHUNK
    lw = lw_ref[...]
    cum = _cumsum_rows(incl.astype(BF16), lw)
    ctot = cum[0:1, :] if reverse else cum[L - 1:L, :]
    g_inv = jnp.exp(-cum)
    g_end = jnp.exp(ctot - cum)
    kd = kd_ref[...]
    bb = bb_ref[...]
    r_f = r_ref[...] * jnp.exp(cum)
    return dict(a=(-kk_ref[...] * jnp.exp(cum - lw)).astype(BF16), b=(bb * g_inv).astype(BF16),
                k=(kd * g_inv).astype(BF16), r_f=r_f, r=r_f.astype(BF16), b_e=(bb * g_end).astype(BF16),
                k_e=(kd * g_end).astype(BF16), v=v_ref[...].astype(BF16), g_tot=jnp.exp(ctot))


def _scan_kernel(rf_ref, vf_ref, kkf_ref, kdf_ref, lwf_ref, bbf_ref, rb_ref, vb_ref, kkb_ref, kdb_ref, lwb_ref, bbb_ref,
                 yf_ref, yb_ref, s_ref):
    L = SCAN_CHUNK
    n = RW_HEAD_DIM

    @pl.when(pl.program_id(1) == 0)
    def _():
        s_ref[...] = jnp.zeros_like(s_ref)

    masks = (_scan_masks(False), _scan_masks(True))
    ops = (_scan_operands(rf_ref, vf_ref, kkf_ref, kdf_ref, lwf_ref, bbf_ref, masks[0][0], False),
           _scan_operands(rb_ref, vb_ref, kkb_ref, kdb_ref, lwb_ref, bbb_ref, masks[1][0], True))
    y_refs = (yf_ref, yb_ref)
    chains = [(d, h) for h in range(RW_HEADS) for d in range(2)]
    cut = lambda d, h, name: ops[d][name][:, h * n:(h + 1) * n]

    gram = [_dot_nt(jnp.concatenate([cut(d, h, "a"), cut(d, h, "r")], axis=0),
                    jnp.concatenate([cut(d, h, "b"), cut(d, h, "k")], axis=0)) for d, h in chains]
    n_ab = [g[:L, :L] for g in gram]
    lower = [jnp.concatenate([jnp.where(masks[d][1], g[:L, L:], 0.0), jnp.where(masks[d][0], g[L:, L:], 0.0)],
                             axis=0).astype(BF16) for (d, h), g in zip(chains, gram)]
    m_rb = [jnp.where(masks[d][0], g[L:, :L], 0.0).astype(BF16) for (d, h), g in zip(chains, gram)]
    nv = [_dot(lo, cut(d, h, "v")) for (d, h), lo in zip(chains, lower)]
    t_inv = [jnp.where(masks[d][2], 1.0, jnp.where(masks[d][3][0], nab, 0.0)) for (d, h), nab in zip(chains, n_ab)]
    for lv in range(1, len(masks[0][3])):
        tb = [t.astype(BF16) for t in t_inv]
        half = [_dot(t, jnp.where(masks[d][3][lv], nab, 0.0).astype(BF16)).astype(BF16)
                for (d, h), t, nab in zip(chains, tb, n_ab)]
        t_inv = [t + _dot(hf, t16) for t, hf, t16 in zip(t_inv, half, tb)]
    au = [_dot(t.astype(BF16), jnp.concatenate([cut(d, h, "a"), x[:L].astype(BF16)], axis=1)).astype(BF16)
          for (d, h), t, x in zip(chains, t_inv, nv)]
    ry = [_dot(m, x) for m, x in zip(m_rb, au)]
    pq = [_dot_tn(x, cut(d, h, "b_e")) for (d, h), x in zip(chains, au)]
    vk = [_dot_tn(cut(d, h, "v"), cut(d, h, "k_e")) for d, h in chains]
    for i, (d, h) in enumerate(chains):
        sl = slice(h * n, (h + 1) * n)
        r_hat = (ops[d]["r_f"][:, sl] + ry[i][:, :n]).astype(BF16)
        y_hat = ry[i][:, n:] + nv[i][L:]
        p_mat = jnp.where(masks[d][2], ops[d]["g_tot"][:, sl], 0.0) + pq[i][:n]
        q_mat = pq[i][n:] + vk[i]
        s0b = s_ref[d, h].astype(BF16)
        y_refs[d][:, sl] = _dot_nt(r_hat, s0b) + y_hat
        s_ref[d, h] = _dot(s0b, p_mat.astype(BF16)) + q_mat


def _rw_scan(r, v, kk, kd0, lw0, bb0, kd1, lw1, bb1, n_ctx):
    b, tt, c = r.shape
    L = SCAN_CHUNK
    nc = tt // L
    ncc = n_ctx // L
    fwd = pl.BlockSpec((None, L, c), lambda bi, i: (bi, i, 0))
    bwd = pl.BlockSpec((None, L, c), lambda bi, i: (bi, jnp.where(i < ncc, ncc - 1 - i, nc - 1 - (i - ncc)), 0))
    return pl.pallas_call(
        _scan_kernel,
        grid=(b, nc),
        in_specs=[fwd] * 6 + [bwd] * 6,
        out_specs=[fwd, bwd],
        out_shape=[jax.ShapeDtypeStruct((b, tt, c), F32)] * 2,
        scratch_shapes=[pltpu.VMEM((2, RW_HEADS, RW_HEAD_DIM, RW_HEAD_DIM), F32)],
        compiler_params=_cparams(("parallel", "arbitrary")),
        name="rwkv_scan",
    )(r, v, kk, kd0, lw0, bb0, r, v, kk, kd1, lw1, bb1)


def _qk_prep(x, gain, cos, sin, ones):
    xn = x * lax.rsqrt(_group_sum(x * x, ones) * (1.0 / DF_HEAD_DIM) + RMS_EPS) * gain
    lane = lax.broadcasted_iota(jnp.int32, x.shape, 1)
    quarter = DF_HEAD_DIM // 4
    rot = jnp.where(jnp.bitwise_and(lane, 2 * quarter - 1) < quarter, -pltpu.roll(xn, x.shape[1] - quarter, 1), pltpu.roll(xn, quarter, 1))
    return xn * cos + rot * sin


def _attn_kernel(lam_ref, q_ref, k_ref, v_ref, cosq_ref, sinq_ref, cosk_ref, sink_ref, qn_ref, kn_ref, sub_ref, ones_ref,
                 o_ref, kb_ref, vb_ref, *, n_ctx, tq, out_scale):
    e = DF_HEAD_DIM
    tt = k_ref.shape[0]
    qi = pl.program_id(2)
    ones = ones_ref[...]

    @pl.when(qi == 0)
    def _():
        kb_ref[...] = _qk_prep(k_ref[...], kn_ref[...], cosk_ref[...], sink_ref[...], ones).astype(BF16)
        vb_ref[...] = v_ref[...].astype(BF16)

    q = (_qk_prep(q_ref[...], qn_ref[...], cosq_ref[...], sinq_ref[...], ones) * (DF_HEAD_DIM ** -0.5)).astype(BF16)
    k = kb_ref[...]
    v = vb_ref[...]
    key_pos = lax.broadcasted_iota(jnp.int32, (tq, tt), 1)
    kv_len = jnp.where(qi * tq < n_ctx, n_ctx, tt)
    visible = key_pos < kv_len
    outs = []
    for m in range(2):
        s = _dot_nt(q[:, m * e:(m + 1) * e], k[:, m * e:(m + 1) * e])
        s = jnp.where(visible, s, -jnp.inf)
        p = jnp.exp(s - jnp.max(s, axis=-1, keepdims=True))
        l = jnp.sum(p, axis=-1, keepdims=True)
        outs.append(_dot(p.astype(BF16), v) / l)
    o = outs[0] - lam_ref[0] * outs[1]
    o = o * lax.rsqrt(jnp.mean(o * o, axis=-1, keepdims=True) + RMS_EPS) * sub_ref[...] * out_scale
    o_ref[...] = o.astype(o_ref.dtype)


def _diff_attention(z, p, lam, lam_init, cos, sin, n_ctx):
    b, tt, _ = z.shape
    tq = ROW_TILE
    w = DF_V_DIM
    q0 = RW_COLS // w
    k0 = q0 + DF_HEADS
    v0 = k0 + DF_HEADS
    two = lambda g: jnp.tile(g, 2).reshape(1, w)
    full = lambda shape: pl.BlockSpec(shape, lambda bi, hi, i: (0,) * len(shape))
    return pl.pallas_call(
        functools.partial(_attn_kernel, n_ctx=n_ctx, tq=tq, out_scale=1.0 - lam_init),
        grid=(b, DF_HEADS, tt // tq),
        in_specs=[pl.BlockSpec(memory_space=pltpu.SMEM),
                  pl.BlockSpec((None, tq, w), lambda bi, hi, i: (bi, i, q0 + hi)),
                  pl.BlockSpec((None, tt, w), lambda bi, hi, i: (bi, 0, k0 + hi)),
                  pl.BlockSpec((None, tt, w), lambda bi, hi, i: (bi, 0, v0 + hi)),
                  pl.BlockSpec((tq, w), lambda bi, hi, i: (i, 0)),
                  pl.BlockSpec((tq, w), lambda bi, hi, i: (i, 0)),
                  full((tt, w)), full((tt, w)), full((1, w)), full((1, w)), full((1, w)), full((w, w))],
        out_specs=pl.BlockSpec((None, tq, w), lambda bi, hi, i: (bi, i, hi)),
        out_shape=jax.ShapeDtypeStruct((b, tt, DF_WIDTH), BF16),
        scratch_shapes=[pltpu.VMEM((tt, w), BF16), pltpu.VMEM((tt, w), BF16)],
        compiler_params=_cparams(("parallel", "parallel", "arbitrary")),
        name="diff_attention",
    )(lam.reshape(1).astype(F32), z, z, z, cos, sin, cos, sin, two(p["qn"]), two(p["kn"]),
      p["subln"].reshape(1, w), _group_ones(w, DF_HEAD_DIM))


def _mix_out_kernel(yf_ref, yb_ref, bonus_ref, gate_ref, ob_ref, gnw_ref, gnb_ref, seg_ref, w_ref, h_ref, mod_ref, o_ref):
    seg = seg_ref[...]
    y = yf_ref[...] + yb_ref[...]
    mu = _group_sum(y, seg) * (1.0 / RW_HEAD_DIM)
    d = y - mu
    var = _group_sum(d * d, seg) * (1.0 / RW_HEAD_DIM)
    yn = d * lax.rsqrt(var + RW_GN_EPS) * gnw_ref[...] + gnb_ref[...]
    ya = (yn + bonus_ref[...]) * gate_ref[...]
    mix = _dot(ya.astype(BF16), w_ref[:RW_WIDTH, :]) + _dot(ob_ref[...], w_ref[RW_WIDTH:, :])
    o_ref[...] = h_ref[...] + mod_ref[2:3, :] * mix


def _mix_out(yf, yb, bonus, gate, ob, p, h, mods, n_ctx):
    b, tt, d = h.shape
    c = RW_WIDTH
    tm = ROW_TILE
    tok = lambda width: pl.BlockSpec((None, tm, width), lambda bi, i: (bi, i, 0))
    full = lambda shape: pl.BlockSpec(shape, lambda bi, i: (0,) * len(shape))
    return pl.pallas_call(
        _mix_out_kernel,
        grid=(b, tt // tm),
        in_specs=[tok(c), tok(c), tok(c), tok(c), tok(DF_WIDTH), full((1, c)), full((1, c)), full((c, c)),
                  full((c + DF_WIDTH, d)), tok(d),
                  pl.BlockSpec((None, None, 6, d), lambda bi, i: (bi, _kind(i, n_ctx), 0, 0))],
        out_specs=tok(d),
        out_shape=jax.ShapeDtypeStruct((b, tt, d), F32),
        compiler_params=_cparams(("parallel", "parallel")),
        name="mix_out_residual",
    )(yf, yb, bonus, gate, ob, p["gn_w"].reshape(1, c), p["gn_b"].reshape(1, c), _group_ones(c, RW_HEAD_DIM),
      p["w_out"].astype(BF16), h, mods)


def _sgu_kernel(h_ref, mod_ref, g_ref, w_in_ref, b_in_ref, lng_ref, lnb_ref, ws_ref, bst_ref, w_out_ref, o_ref):
    x = h_ref[...]
    tm = x.shape[0]
    y = _norm_mod(x, g_ref[...], mod_ref[1:2, :], mod_ref[0:1, :])
    z = _dot(y.astype(BF16), w_in_ref[...]) + b_in_ref[...]
    z = 0.5 * z * (1.0 + lax.erf(z * (2.0 ** -0.5)))
    u = z[:, :SGU_WIDTH]
    v = z[:, SGU_WIDTH:]
    mu = jnp.mean(v, axis=-1, keepdims=True)
    var = jnp.mean(jnp.square(v - mu), axis=-1, keepdims=True)
    v = ((v - mu) * lax.rsqrt(var + LN_EPS) * lng_ref[...] + lnb_ref[...]).astype(BF16)
    rows = []
    for ci in range(tm // SGU_CHUNK):
        cols = []
        for gi in range(SGU_GROUPS):
            vb = v[ci * SGU_CHUNK:(ci + 1) * SGU_CHUNK, gi * SGU_GROUP_DIM:(gi + 1) * SGU_GROUP_DIM]
            cols.append(_dot(ws_ref[gi], vb) + bst_ref[:, gi:gi + 1])
        rows.append(jnp.concatenate(cols, axis=1))
    sv = jnp.concatenate(rows, axis=0)
    o_ref[...] = x + mod_ref[2:3, :] * _dot((u * sv).astype(BF16), w_out_ref[...])


def _sgu_layer(h, mods, g, p, n_ctx):
    b, tt, d = h.shape
    tm = ROW_TILE
    full = lambda shape: pl.BlockSpec(shape, lambda bi, i: (0,) * len(shape))
    return pl.pallas_call(
        _sgu_kernel,
        grid=(b, tt // tm),
        in_specs=[pl.BlockSpec((None, tm, d), lambda bi, i: (bi, i, 0)),
                  pl.BlockSpec((None, None, 6, d), lambda bi, i: (bi, _kind(i, n_ctx), 0, 0)),
                  full((1, d)), full((d, 2 * SGU_WIDTH)), full((1, 2 * SGU_WIDTH)),
                  full((1, SGU_WIDTH)), full((1, SGU_WIDTH)),
                  full((SGU_GROUPS, SGU_CHUNK, SGU_CHUNK)), full((SGU_CHUNK, SGU_GROUPS)),
                  full((SGU_WIDTH, d))],
        out_specs=pl.BlockSpec((None, tm, d), lambda bi, i: (bi, i, 0)),
        out_shape=jax.ShapeDtypeStruct((b, tt, d), F32),
        compiler_params=_cparams(("parallel", "parallel")),
        name="sgu_layer",
    )(h, mods, g.reshape(1, d), p["w_in"].astype(BF16), p["b_in"].reshape(1, -1),
      p["ln_g"].reshape(1, -1), p["ln_b"].reshape(1, -1), p["ws"].astype(BF16), p["bs"].T,
      p["w_out"].astype(BF16))


def _router_kernel(h_ref, mod_ref, g_ref, rw_ref, rb_ref, f_ref, idx_ref, wt_ref, rank_ref, cnt_ref, base_ref):
    first = jnp.logical_and(pl.program_id(0) == 0, pl.program_id(1) == 0)

    @pl.when(first)
    def _():
        base_ref[...] = jnp.zeros_like(base_ref)

    x = h_ref[...]
    tm = x.shape[0]
    f = _norm_mod(x, g_ref[...], mod_ref[4:5, :], mod_ref[3:4, :])
    f_ref[...] = f.astype(f_ref.dtype)
    logits = _dot(f, rw_ref[...], HIGHEST) + rb_ref[...]
    lane = lax.broadcasted_iota(jnp.int32, (tm, LANES), 1)
    tri = (lax.broadcasted_iota(jnp.int32, (tm, tm), 1) < lax.broadcasted_iota(jnp.int32, (tm, tm), 0))
    vals, sels = [], []
    idx_out = jnp.zeros((tm, LANES), jnp.int32)
    l = logits
    for kk in range(TOP_K):
        m = jnp.max(l, axis=-1, keepdims=True)
        idx = jnp.min(jnp.where(l == m, lane, LANES), axis=-1, keepdims=True)
        sel = lane == idx
        l = jnp.where(sel, -jnp.inf, l)
        vals.append(m)
        sels.append(sel)
        idx_out = jnp.where(lane == kk, idx, idx_out)
    es = [jnp.exp(vv - vals[0]) for vv in vals]
    den = es[0] + es[1] + es[2] + es[3]
    wt_out = jnp.zeros((tm, LANES), F32)
    for kk in range(TOP_K):
        wt_out = jnp.where(lane == kk, es[kk] / den, wt_out)
    onehot = jnp.zeros((tm, LANES), F32)
    for sel in sels:
        onehot = onehot + sel.astype(F32)
    ahead = _dot(tri.astype(BF16), onehot.astype(BF16)) + base_ref[...]
    rank_out = jnp.zeros((tm, LANES), jnp.int32)
    for kk in range(TOP_K):
        rk = jnp.sum(jnp.where(sels[kk], ahead, 0.0), axis=-1, keepdims=True).astype(jnp.int32)
        rank_out = jnp.where(lane == kk, rk, rank_out)
    idx_ref[...] = idx_out
    wt_ref[...] = wt_out
    rank_ref[...] = rank_out
    base_ref[...] = base_ref[...] + jnp.sum(onehot, axis=0, keepdims=True)
    cnt_ref[...] = base_ref[...]


def _router(h, mods, g, rt_w, rt_b, n_ctx, b0, b):
    _, tt, d = h.shape
    tm = ROW_TILE
    rw = jnp.zeros((d, LANES), F32).at[:, :N_EXPERTS].set(rt_w)
    rb = jnp.full((1, LANES), -jnp.inf, F32).at[0, :N_EXPERTS].set(rt_b)
    tok = lambda width: pl.BlockSpec((None, tm, width), lambda bi, i: (bi, i, 0))
    return pl.pallas_call(
        _router_kernel,
        grid=(b, tt // tm),
        in_specs=[pl.BlockSpec((None, tm, d), lambda bi, i: (b0 + bi, i, 0)),
                  pl.BlockSpec((None, None, 6, d), lambda bi, i: (b0 + bi, _kind(i, n_ctx), 0, 0)),
                  pl.BlockSpec((1, d), lambda bi, i: (0, 0)),
                  pl.BlockSpec((d, LANES), lambda bi, i: (0, 0)),
                  pl.BlockSpec((1, LANES), lambda bi, i: (0, 0))],
        out_specs=[tok(d), tok(LANES), tok(LANES), tok(LANES), pl.BlockSpec((1, LANES), lambda bi, i: (0, 0))],
        out_shape=[jax.ShapeDtypeStruct((b, tt, d), BF16),
                   jax.ShapeDtypeStruct((b, tt, LANES), jnp.int32),
                   jax.ShapeDtypeStruct((b, tt, LANES), F32),
                   jax.ShapeDtypeStruct((b, tt, LANES), jnp.int32),
                   jax.ShapeDtypeStruct((1, LANES), F32)],
        scratch_shapes=[pltpu.VMEM((1, LANES), F32)],
        compiler_params=_cparams(("arbitrary", "arbitrary")),
        name="moe_router",
    )(h, mods, g.reshape(1, d), rw, rb)


def _expert_kernel(te_ref, nv_ref, x_ref, w1_ref, b1_ref, w2_ref, b2_ref, o_ref, w1b_ref, w2b_ref):
    j = pl.program_id(0)
    active = j < nv_ref[0]
    new_expert = jnp.logical_or(j == 0, te_ref[j] != te_ref[jnp.maximum(j - 1, 0)])

    @pl.when(jnp.logical_and(active, new_expert))
    def _():
        w1b_ref[...] = w1_ref[...].astype(BF16)
        w2b_ref[...] = w2_ref[...].astype(BF16)

    @pl.when(active)
    def _():
        f = w2b_ref.shape[0]
        fh = f // 2
        x = x_ref[...]
        y = b2_ref[...]
        for c in range(2):
            lo, hi = c * fh, (c + 1) * fh
            glu = _dot(x, w1b_ref[:, lo:hi]) + b1_ref[:, lo:hi]
            lin = _dot(x, w1b_ref[:, f + lo:f + hi]) + b1_ref[:, f + lo:f + hi]
            glu = jnp.minimum(glu, SWIGLU_LIMIT)
            lin = jnp.clip(lin, -SWIGLU_LIMIT, SWIGLU_LIMIT)
            act = glu * jax.nn.sigmoid(SWIGLU_ALPHA * glu) * (lin + 1.0)
            y = y + _dot(act.astype(BF16), w2b_ref[lo:hi, :])
        o_ref[...] = y.astype(o_ref.dtype)

    @pl.when(jnp.logical_not(active))
    def _():
        o_ref[...] = jnp.zeros_like(o_ref)


def _experts(xs, tile_expert, n_valid, w1, b1, w2, b2, layer):
    s, d = xs.shape
    depth, e, _, f2 = w1.shape
    f = f2 // 2
    tm = EXPERT_ROW_TILE
    grid_spec = pltpu.PrefetchScalarGridSpec(
        num_scalar_prefetch=2,
        grid=(s // tm,),
        in_specs=[pl.BlockSpec((tm, d), lambda j, te, nv: (j, 0)),
                  pl.BlockSpec((None, None, d, f2), lambda j, te, nv: (layer, te[j], 0, 0)),
                  pl.BlockSpec((None, None, 1, f2), lambda j, te, nv: (layer, te[j], 0, 0)),
                  pl.BlockSpec((None, None, f, d), lambda j, te, nv: (layer, te[j], 0, 0)),
                  pl.BlockSpec((None, None, 1, d), lambda j, te, nv: (layer, te[j], 0, 0))],
        out_specs=pl.BlockSpec((tm, d), lambda j, te, nv: (j, 0)),
        scratch_shapes=[pltpu.VMEM((d, f2), BF16), pltpu.VMEM((f, d), BF16)],
    )
    return pl.pallas_call(
        _expert_kernel,
        grid_spec=grid_spec,
        out_shape=jax.ShapeDtypeStruct((s, d), BF16),
        compiler_params=_cparams(("arbitrary",)),
        name="moe_experts",
    )(tile_expert, n_valid, xs, w1, b1.reshape(depth, e, 1, f2), w2, b2.reshape(depth, e, 1, d))


def _moe(h, mods, g, rt_w, rt_b, w1, b1, w2, b2, layer, n_ctx, b0, b):
    _, tt, d = h.shape
    n = b * tt
    tm = EXPERT_ROW_TILE
    f, idx, wt, rank, cnt = _router(h, mods, g, rt_w, rt_b, n_ctx, b0, b)
    top_i = idx.reshape(n, LANES)[:, :TOP_K]
    top_w = wt.reshape(n, LANES)[:, :TOP_K]
    rank = rank.reshape(n, LANES)[:, :TOP_K]
    counts = cnt[0, :N_EXPERTS].astype(jnp.int32)
    padded = ((counts + tm - 1) // tm) * tm
    p_end = jnp.cumsum(padded)
    p_off = p_end - padded
    u_off = jnp.cumsum(counts) - counts
    slot = (p_off[top_i] + rank).reshape(-1)
    n_tiles = (n * TOP_K) // tm + N_EXPERTS
    s_rows = n_tiles * tm
    _, sorted_pair = lax.sort_key_val(slot, jnp.arange(n * TOP_K, dtype=jnp.int32))
    srow = jnp.arange(s_rows, dtype=jnp.int32)
    expert_of = lambda rows: jnp.minimum(jnp.sum((rows[:, None] >= p_end[None, :]).astype(jnp.int32), axis=1),
                                         N_EXPERTS - 1)
    row_e = expert_of(srow)
    within = srow - p_off[row_e]
    valid = within < counts[row_e]
    src = jnp.clip(u_off[row_e] + within, 0, n * TOP_K - 1)
    pair = sorted_pair[src]
    row_token = jnp.where(valid, pair // TOP_K, 0)
    n_valid = (p_end[-1] // tm).astype(jnp.int32).reshape(1)
    tile_start = jnp.arange(n_tiles, dtype=jnp.int32) * tm
    tile_expert = expert_of(tile_start)
    last_e = tile_expert[jnp.maximum(n_valid[0] - 1, 0)]
    tile_expert = jnp.where(tile_start < p_end[-1], tile_expert, last_e)
    xs = f.reshape(n, d).at[row_token].get(mode="promise_in_bounds")
    ys = _experts(xs, tile_expert, n_valid, w1, b1, w2, b2, layer)
    picked = ys.at[slot.reshape(n, TOP_K).T.reshape(-1)].get(mode="promise_in_bounds").reshape(TOP_K, n, d)
    m = jnp.sum(picked.astype(F32) * top_w.T[:, :, None], axis=0).reshape(b, tt, d)
    mods = mods[b0:b0 + b]
    gate = jnp.where((jnp.arange(tt) >= n_ctx)[None, :, None], mods[:, 1, 5][:, None, :], mods[:, 0, 5][:, None, :])
    return h[b0:b0 + b] + gate * m


def _rope_tables(n_tokens, n_ctx):
    rows = n_tokens // GRID_W
    row = jnp.repeat(jnp.arange(rows, dtype=F32), GRID_W)
    col = jnp.tile(jnp.arange(GRID_W, dtype=F32), rows)
    axis_dim = DF_HEAD_DIM // 2
    inv_freq = ROPE_THETA ** (-jnp.arange(0, axis_dim, 2, dtype=F32) / axis_dim)
    ar = row[:, None] * inv_freq
    ac = col[:, None] * inv_freq
    ang = jnp.concatenate([ar, ar, ac, ac], axis=-1)
    cos = jnp.concatenate([jnp.ones((n_ctx, DF_HEAD_DIM), F32), jnp.cos(ang)], axis=0)
    sin = jnp.concatenate([jnp.zeros((n_ctx, DF_HEAD_DIM), F32), jnp.sin(ang)], axis=0)
    return jnp.tile(cos, (1, 2)), jnp.tile(sin, (1, 2))


def _even_layer(h, mods, g1, p, li, n_ctx, rope):
    z = _nm_mm(h, mods, g1, p["w_in"].astype(BF16), n_ctx)
    r, v, kk, kd0, kd1, lw0, lw1, bb0, bb1, gate, bonus = _rw_features(z, p, n_ctx)
    yf, yb = _rw_scan(r, v, kk, kd0, lw0, bb0, kd1, lw1, bb1, n_ctx)
    lam_init = 0.8 - 0.6 * math.exp(-0.3 * li)
    lp = p["lam"].astype(F32)
    lam = jnp.exp(jnp.sum(lp[0] * lp[1])) - jnp.exp(jnp.sum(lp[2] * lp[3])) + lam_init
    ob = _diff_attention(z, p, lam, lam_init, rope[0], rope[1], n_ctx)
    return _mix_out(yf, yb, bonus, gate, ob, p, h, mods, n_ctx)


def kernel(x, c, ctx, c_ctx, norm1_g, norm2_g, ada_w, ada_b, ev_w_in, ev_w_out, rw_mu, rw_w0, rw_w2, rw_a0, rw_a2, rw_g2, rw_kk, rw_ka, rw_rk, rw_gn_w, rw_gn_b, df_qn, df_kn, df_lam, df_subln, sg_w_in, sg_b_in, sg_ln_g, sg_ln_b, sg_ws, sg_bs, sg_w_out, rt_w, rt_b, ex_w1, ex_b1, ex_w2, ex_b2):
    b, t, d = x.shape
    n_ctx = ctx.shape[1]
    depth = ada_w.shape[0]
    assert d == D_MODEL and t % ROW_TILE == 0 and n_ctx % ROW_TILE == 0 and t % GRID_W == 0
    rope = _rope_tables(t, n_ctx)
    r_pad = -(-(b + 1) // 8) * 8
    cs = jnp.zeros((r_pad, d), F32).at[:b].set(c).at[b].set(c_ctx)
    ada = _ada_table(cs, ada_w, ada_b).reshape(depth, r_pad, 6, d)
    h = jnp.concatenate([ctx, x], axis=1)
    for l in range(depth):
        ctx_out = any(m % 2 == 0 for m in range(l + 1, depth))
        j = l // 2
        mods = jnp.stack([jnp.broadcast_to(ada[l, b], (b, 6, d)), ada[l, :b]], axis=1)
        if l % 2 == 0:
            p = {"w_in": ev_w_in[j], "w_out": ev_w_out[j], "mu": rw_mu[j], "w0": rw_w0[j], "w2": rw_w2[j],
                 "a0": rw_a0[j], "a2": rw_a2[j], "g2": rw_g2[j], "k_k": rw_kk[j], "k_a": rw_ka[j],
                 "r_k": rw_rk[j], "gn_w": rw_gn_w[j], "gn_b": rw_gn_b[j], "qn": df_qn[j], "kn": df_kn[j],
                 "lam": df_lam[j], "subln": df_subln[j]}
            h = _even_layer(h, mods, norm1_g[l], p, l, n_ctx, rope)
        else:
            p = {"w_in": sg_w_in[j], "b_in": sg_b_in[j], "ln_g": sg_ln_g[j], "ln_b": sg_ln_b[j],
                 "ws": sg_ws[j], "bs": sg_bs[j], "w_out": sg_w_out[j]}
            h = _sgu_layer(h, mods, norm1_g[l], p, n_ctx)
        if n_ctx > 0 and not ctx_out:
            h = h[:, n_ctx:]
            n_ctx = 0
        splits = ((0, b // 2), (b // 2, b - b // 2)) if b > 1 else ((0, b),)
        h = jnp.concatenate([_moe(h, mods, norm2_g[l], rt_w[l], rt_b[l], ex_w1, ex_b1, ex_w2, ex_b2, l, n_ctx, b0, nb)
                             for b0, nb in splits], axis=0)
    return h[:, n_ctx:]
```

```python
import functools
import math

import jax
import jax.numpy as jnp
from jax import lax
from jax.experimental import pallas as pl
from jax.experimental.pallas import tpu as pltpu

F32 = jnp.float32
BF16 = jnp.bfloat16
HIGHEST = lax.Precision.HIGHEST

D_MODEL = 1024
GRID_W = 64
RMS_EPS = 1e-6
LN_EPS = 1e-5

RW_HEAD_DIM = 64
RW_WIDTH = 512
RW_HEADS = RW_WIDTH // RW_HEAD_DIM
DECAY_LORA = 64
ICLR_LORA = 64
GATE_LORA = 128
RW_COLS = 3 * RW_WIDTH + 2 * DECAY_LORA + 2 * ICLR_LORA + GATE_LORA
RW_GN_EPS = 64e-5

DF_HEAD_DIM = 64
DF_V_DIM = 128
DF_WIDTH = 512
DF_HEADS = DF_WIDTH // DF_V_DIM
DF_QK = DF_HEADS * 2 * DF_HEAD_DIM
ROPE_THETA = 10000.0
EVEN_COLS = RW_COLS + 2 * DF_QK + DF_WIDTH

SGU_CHUNK = 128
SGU_WIDTH = 1024
SGU_GROUPS = 8
SGU_GROUP_DIM = SGU_WIDTH // SGU_GROUPS

N_EXPERTS = 32
TOP_K = 4
SWIGLU_LIMIT = 7.0
SWIGLU_ALPHA = 1.702

LANES = 128
ROW_TILE = 256
SCAN_CHUNK = 64
EXPERT_ROW_TILE = 512
VMEM_LIMIT = 56 * 1024 * 1024


def _cparams(sem):
    return pltpu.CompilerParams(dimension_semantics=sem, vmem_limit_bytes=VMEM_LIMIT)


def _dot(a, b, precision=None):
    return jnp.dot(a, b, preferred_element_type=F32, precision=precision)


def _dot_nt(a, b, precision=None):
    return lax.dot_general(a, b, (((1,), (1,)), ((), ())), preferred_element_type=F32, precision=precision)


def _dot_tn(a, b, precision=None):
    return lax.dot_general(a, b, (((0,), (0,)), ((), ())), preferred_element_type=F32, precision=precision)


def _norm_mod(x, g, scale, shift):
    ms = jnp.mean(x * x, axis=-1, keepdims=True)
    return (x * lax.rsqrt(ms + RMS_EPS) * g) * (1.0 + scale) + shift


def _ada_kernel(x_ref, w_ref, b_ref, o_ref):
    x = x_ref[...]
    s = x * jax.nn.sigmoid(x)
    o_ref[...] = _dot(s, w_ref[...], HIGHEST) + b_ref[...]


def _ada_table(cs, ada_w, ada_b):
    depth, d, n = ada_w.shape
    r = cs.shape[0]
    tn = 1024
    return pl.pallas_call(
        _ada_kernel,
        grid=(depth, n // tn),
        in_specs=[pl.BlockSpec((r, d), lambda l, j: (0, 0)),
                  pl.BlockSpec((None, d, tn), lambda l, j: (l, 0, j)),
                  pl.BlockSpec((None, 1, tn), lambda l, j: (l, 0, j))],
        out_specs=pl.BlockSpec((None, r, tn), lambda l, j: (l, 0, j)),
        out_shape=jax.ShapeDtypeStruct((depth, r, n), F32),
        compiler_params=_cparams(("parallel", "parallel")),
        name="ada_table",
    )(cs, ada_w, ada_b.reshape(depth, 1, n))


def _kind(i, n_ctx):
    return (i * ROW_TILE >= n_ctx).astype(jnp.int32) if n_ctx > 0 else 1


def _nm_mm_kernel(h_ref, mod_ref, g_ref, w_ref, o_ref):
    y = _norm_mod(h_ref[...], g_ref[...], mod_ref[1:2, :], mod_ref[0:1, :])
    o_ref[...] = _dot(y.astype(BF16), w_ref[...]).astype(o_ref.dtype)


def _nm_mm(h, mods, g, w, n_ctx, out_dtype=F32):
    b, tt, d = h.shape
    n = w.shape[1]
    tm = ROW_TILE
    return pl.pallas_call(
        _nm_mm_kernel,
        grid=(b, tt // tm),
        in_specs=[pl.BlockSpec((None, tm, d), lambda bi, i: (bi, i, 0)),
                  pl.BlockSpec((None, None, 6, d), lambda bi, i: (bi, _kind(i, n_ctx), 0, 0)),
                  pl.BlockSpec((1, d), lambda bi, i: (0, 0)),
                  pl.BlockSpec((d, n), lambda bi, i: (0, 0))],
        out_specs=pl.BlockSpec((None, tm, n), lambda bi, i: (bi, i, 0)),
        out_shape=jax.ShapeDtypeStruct((b, tt, n), out_dtype),
        compiler_params=_cparams(("parallel", "parallel")),
        name="norm_mod_proj",
    )(h, mods, g.reshape(1, d), w)


def _group_ones(width, group):
    idx = jnp.arange(width) // group
    return (idx[:, None] == idx[None, :]).astype(BF16)


def _group_sum(x, ones_bf16):
    hi = x.astype(BF16)
    lo = (x - hi.astype(F32)).astype(BF16)
    return _dot(hi, ones_bf16) + _dot(lo, ones_bf16)


def _softplus(x):
    return jnp.maximum(x, 0.0) + jnp.log1p(jnp.exp(-jnp.abs(x)))


def _rw_feat_kernel(z_ref, zprev_ref, znext_ref, mu_ref, w0_ref, w2_ref, a0_ref, a2_ref, g2_ref, kk_ref, ka_ref,
                    rk_ref, seg_ref,
                    r_ref, v_ref, kko_ref, kd0_ref, kd1_ref, lw0_ref, lw1_ref, bb0_ref, bb1_ref, g_ref, bonus_ref,
                    *, n_ctx, tt):
    c = RW_WIDTH
    tm = z_ref.shape[0]
    i = pl.program_id(1)
    z = z_ref[...]
    seq_first = jnp.logical_or(i * tm == 0, i * tm == n_ctx)
    seq_last = jnp.logical_or((i + 1) * tm == n_ctx, (i + 1) * tm == tt)
    prev_row = jnp.where(seq_first, 0.0, zprev_ref[7:8, :])
    next_row = jnp.where(seq_last, 0.0, znext_ref[0:1, :])
    rows = lax.broadcasted_iota(jnp.int32, z.shape, 0)
    zp = jnp.where(rows == 0, prev_row, pltpu.roll(z, 1, 0))
    zn = jnp.where(rows == tm - 1, next_row, pltpu.roll(z, tm - 1, 0))
    za = z + mu_ref[0:1, :] * (zp - z) + mu_ref[1:2, :] * (zn - z)
    r = za[:, :c]
    k = za[:, c:2 * c]
    v = za[:, 2 * c:3 * c]
    wl = za[:, 3 * c:3 * c + 2 * DECAY_LORA]
    al = za[:, 3 * c + 2 * DECAY_LORA:3 * c + 2 * DECAY_LORA + 2 * ICLR_LORA]
    gl = za[:, 3 * c + 2 * DECAY_LORA + 2 * ICLR_LORA:]
    r_ref[...] = r
    v_ref[...] = v
    twl = jnp.tanh(wl)
    seg = seg_ref[...]
    kkr = k * kk_ref[...]
    kk = kkr / jnp.maximum(jnp.sqrt(_group_sum(kkr * kkr, seg)), 1e-12)
    kko_ref[...] = kk
    lw_refs = (lw0_ref, lw1_ref)
    bb_refs = (bb0_ref, bb1_ref)
    kd_refs = (kd0_ref, kd1_ref)
    kd_sum = None
    for d in range(2):
        wd = w0_ref[d:d + 1, :] + _dot(twl[:, d * DECAY_LORA:(d + 1) * DECAY_LORA], w2_ref[d], HIGHEST)
        w = -_softplus(-wd) - 0.5
        lw_refs[d][...] = -jnp.exp(w)
        a = jax.nn.sigmoid(a0_ref[d:d + 1, :] + _dot(al[:, d * ICLR_LORA:(d + 1) * ICLR_LORA], a2_ref[d], HIGHEST))
        bb_refs[d][...] = kk * a
        kd = k * (1.0 + (a - 1.0) * ka_ref[...])
        kd_refs[d][...] = kd
        kd_sum = kd if kd_sum is None else kd_sum + kd
    g_ref[...] = _dot(jax.nn.sigmoid(gl), g2_ref[...], HIGHEST)
    bonus_ref[...] = _group_sum(r * kd_sum * rk_ref[...], seg) * v


def _rw_features(z, p, n_ctx):
    b, tt, _ = z.shape
    c = RW_WIDTH
    tm = ROW_TILE
    hb = tm // 8
    last_halo = tt // 8 - 1
    full = lambda shape: pl.BlockSpec(shape, lambda bi, i: (0,) * len(shape))
    out_spec = pl.BlockSpec((None, tm, c), lambda bi, i: (bi, i, 0))
    return pl.pallas_call(
        functools.partial(_rw_feat_kernel, n_ctx=n_ctx, tt=tt),
        grid=(b, tt // tm),
        in_specs=[pl.BlockSpec((None, tm, RW_COLS), lambda bi, i: (bi, i, 0)),
                  pl.BlockSpec((None, 8, RW_COLS), lambda bi, i: (bi, jnp.maximum(i * hb - 1, 0), 0)),
                  pl.BlockSpec((None, 8, RW_COLS), lambda bi, i: (bi, jnp.minimum((i + 1) * hb, last_halo), 0)),
                  full((2, RW_COLS)), full((2, c)), full((2, DECAY_LORA, c)), full((2, c)), full((2, ICLR_LORA, c)),
                  full((GATE_LORA, c)), full((1, c)), full((1, c)), full((1, c)), full((c, c))],
        out_specs=[out_spec] * 11,
        out_shape=[jax.ShapeDtypeStruct((b, tt, c), F32)] * 11,
        compiler_params=_cparams(("parallel", "parallel")),
        name="rwkv_features",
    )(z, z, z, p["mu"], p["w0"], p["w2"], p["a0"], p["a2"], p["g2"], p["k_k"].reshape(1, c), p["k_a"].reshape(1, c),
      p["r_k"].reshape(1, c), _group_ones(c, RW_HEAD_DIM))


def _cumsum_rows(tri_bf16, x):
    hi = x.astype(BF16)
    r1 = x - hi.astype(F32)
    mid = r1.astype(BF16)
    lo = (r1 - mid.astype(F32)).astype(BF16)
    return _dot(tri_bf16, hi) + _dot(tri_bf16, mid) + _dot(tri_bf16, lo)


def _scan_masks(reverse):
    L = SCAN_CHUNK
    row = lax.broadcasted_iota(jnp.int32, (L, L), 0)
    col = lax.broadcasted_iota(jnp.int32, (L, L), 1)
    incl = (col >= row) if reverse else (col <= row)
    strict = (col > row) if reverse else (col < row)
    levels = []
    for lg in range(int(math.log2(L))):
        same_pair = jnp.right_shift(row, lg + 1) == jnp.right_shift(col, lg + 1)
        other_half = jnp.right_shift(row, lg) != jnp.right_shift(col, lg)
        levels.append(jnp.logical_and(jnp.logical_and(same_pair, other_half), strict))
    return incl, strict, row == col, levels


def _scan_operands(r_ref, v_ref, kk_ref, kd_ref, lw_ref, bb_ref, incl, reverse):
    L = SCAN_CHUNK
    lw = lw_ref[...]
    cum = _cumsum_rows(incl.astype(BF16), lw)
    ctot = cum[0:1, :] if reverse else cum[L - 1:L, :]
    g_inv = jnp.exp(-cum)
    g_end = jnp.exp(ctot - cum)
    kd = kd_ref[...]
    bb = bb_ref[...]
    r_f = r_ref[...] * jnp.exp(cum)
    return dict(a=(-kk_ref[...] * jnp.exp(cum - lw)).astype(BF16), b=(bb * g_inv).astype(BF16),
                k=(kd * g_inv).astype(BF16), r_f=r_f, r=r_f.astype(BF16), b_e=(bb * g_end).astype(BF16),
                k_e=(kd * g_end).astype(BF16), v=v_ref[...].astype(BF16), g_tot=jnp.exp(ctot))


def _scan_kernel(rf_ref, vf_ref, kkf_ref, kdf_ref, lwf_ref, bbf_ref, rb_ref, vb_ref, kkb_ref, kdb_ref, lwb_ref, bbb_ref,
                 yf_ref, yb_ref, s_ref):
    L = SCAN_CHUNK
    n = RW_HEAD_DIM

    @pl.when(pl.program_id(1) == 0)
    def _():
        s_ref[...] = jnp.zeros_like(s_ref)

    masks = (_scan_masks(False), _scan_masks(True))
    ops = (_scan_operands(rf_ref, vf_ref, kkf_ref, kdf_ref, lwf_ref, bbf_ref, masks[0][0], False),
           _scan_operands(rb_ref, vb_ref, kkb_ref, kdb_ref, lwb_ref, bbb_ref, masks[1][0], True))
    y_refs = (yf_ref, yb_ref)
    chains = [(d, h) for h in range(RW_HEADS) for d in range(2)]
    cut = lambda d, h, name: ops[d][name][:, h * n:(h + 1) * n]

    gram = [_dot_nt(jnp.concatenate([cut(d, h, "a"), cut(d, h, "r")], axis=0),
                    jnp.concatenate([cut(d, h, "b"), cut(d, h, "k")], axis=0)) for d, h in chains]
    n_ab = [g[:L, :L] for g in gram]
    lower = [jnp.concatenate([jnp.where(masks[d][1], g[:L, L:], 0.0), jnp.where(masks[d][0], g[L:, L:], 0.0)],
                             axis=0).astype(BF16) for (d, h), g in zip(chains, gram)]
    m_rb = [jnp.where(masks[d][0], g[L:, :L], 0.0).astype(BF16) for (d, h), g in zip(chains, gram)]
    nv = [_dot(lo, cut(d, h, "v")) for (d, h), lo in zip(chains, lower)]
    t_inv = [jnp.where(masks[d][2], 1.0, jnp.where(masks[d][3][0], nab, 0.0)) for (d, h), nab in zip(chains, n_ab)]
    for lv in range(1, len(masks[0][3])):
        tb = [t.astype(BF16) for t in t_inv]
        half = [_dot(t, jnp.where(masks[d][3][lv], nab, 0.0).astype(BF16)).astype(BF16)
                for (d, h), t, nab in zip(chains, tb, n_ab)]
        t_inv = [t + _dot(hf, t16) for t, hf, t16 in zip(t_inv, half, tb)]
    au = [_dot(t.astype(BF16), jnp.concatenate([cut(d, h, "a"), x[:L].astype(BF16)], axis=1)).astype(BF16)
          for (d, h), t, x in zip(chains, t_inv, nv)]
    ry = [_dot(m, x) for m, x in zip(m_rb, au)]
    pq = [_dot_tn(x, cut(d, h, "b_e")) for (d, h), x in zip(chains, au)]
    vk = [_dot_tn(cut(d, h, "v"), cut(d, h, "k_e")) for d, h in chains]
    for i, (d, h) in enumerate(chains):
        sl = slice(h * n, (h + 1) * n)
        r_hat = (ops[d]["r_f"][:, sl] + ry[i][:, :n]).astype(BF16)
        y_hat = ry[i][:, n:] + nv[i][L:]
        p_mat = jnp.where(masks[d][2], ops[d]["g_tot"][:, sl], 0.0) + pq[i][:n]
        q_mat = pq[i][n:] + vk[i]
        s0b = s_ref[d, h].astype(BF16)
        y_refs[d][:, sl] = _dot_nt(r_hat, s0b) + y_hat
        s_ref[d, h] = _dot(s0b, p_mat.astype(BF16)) + q_mat


def _rw_scan(r, v, kk, kd0, lw0, bb0, kd1, lw1, bb1, n_ctx):
    b, tt, c = r.shape
    L = SCAN_CHUNK
    nc = tt // L
    ncc = n_ctx // L
    fwd = pl.BlockSpec((None, L, c), lambda bi, i: (bi, i, 0))
    bwd = pl.BlockSpec((None, L, c), lambda bi, i: (bi, jnp.where(i < ncc, ncc - 1 - i, nc - 1 - (i - ncc)), 0))
    return pl.pallas_call(
        _scan_kernel,
        grid=(b, nc),
        in_specs=[fwd] * 6 + [bwd] * 6,
        out_specs=[fwd, bwd],
        out_shape=[jax.ShapeDtypeStruct((b, tt, c), F32)] * 2,
        scratch_shapes=[pltpu.VMEM((2, RW_HEADS, RW_HEAD_DIM, RW_HEAD_DIM), F32)],
        compiler_params=_cparams(("parallel", "arbitrary")),
        name="rwkv_scan",
    )(r, v, kk, kd0, lw0, bb0, r, v, kk, kd1, lw1, bb1)


def _qk_prep(x, gain, cos, sin, ones):
    xn = x * lax.rsqrt(_group_sum(x * x, ones) * (1.0 / DF_HEAD_DIM) + RMS_EPS) * gain
    lane = lax.broadcasted_iota(jnp.int32, x.shape, 1)
    quarter = DF_HEAD_DIM // 4
    rot = jnp.where(jnp.bitwise_and(lane, 2 * quarter - 1) < quarter, -pltpu.roll(xn, x.shape[1] - quarter, 1), pltpu.roll(xn, quarter, 1))
    return xn * cos + rot * sin


def _attn_kernel(lam_ref, q_ref, k_ref, v_ref, cosq_ref, sinq_ref, cosk_ref, sink_ref, qn_ref, kn_ref, sub_ref, ones_ref,
                 o_ref, kb_ref, vb_ref, *, n_ctx, tq, out_scale):
    e = DF_HEAD_DIM
    tt = k_ref.shape[0]
    qi = pl.program_id(2)
    ones = ones_ref[...]

    @pl.when(qi == 0)
    def _():
        kb_ref[...] = _qk_prep(k_ref[...], kn_ref[...], cosk_ref[...], sink_ref[...], ones).astype(BF16)
        vb_ref[:, :DF_V_DIM] = v_ref[...].astype(BF16)
        vb_ref[:, DF_V_DIM:] = jnp.ones((tt, DF_V_DIM), BF16)

    q = (_qk_prep(q_ref[...], qn_ref[...], cosq_ref[...], sinq_ref[...], ones) * (DF_HEAD_DIM ** -0.5)).astype(BF16)
    k = kb_ref[...]
    v = vb_ref[...]
    key_pos = lax.broadcasted_iota(jnp.int32, (tq, tt), 1)
    kv_len = jnp.where(qi * tq < n_ctx, n_ctx, tt)
    visible = key_pos < kv_len
    outs = []
    for m in range(2):
        s = _dot_nt(q[:, m * e:(m + 1) * e], k[:, m * e:(m + 1) * e])
        s = jnp.where(visible, s, -jnp.inf)
        p = jnp.exp((s - jnp.max(s, axis=-1, keepdims=True)).astype(BF16))
        ol = _dot(p, v)
        outs.append(ol[:, :DF_V_DIM] / ol[:, DF_V_DIM:DF_V_DIM + 1])
    o = outs[0] - lam_ref[0] * outs[1]
    o = o * lax.rsqrt(jnp.mean(o * o, axis=-1, keepdims=True) + RMS_EPS) * sub_ref[...] * out_scale
    o_ref[...] = o.astype(o_ref.dtype)


def _diff_attention(z, p, lam, lam_init, cos, sin, n_ctx):
    b, tt, _ = z.shape
    tq = ROW_TILE
    w = DF_V_DIM
    q0 = RW_COLS // w
    k0 = q0 + DF_HEADS
    v0 = k0 + DF_HEADS
    two = lambda g: jnp.tile(g, 2).reshape(1, w)
    full = lambda shape: pl.BlockSpec(shape, lambda bi, hi, i: (0,) * len(shape))
    return pl.pallas_call(
        functools.partial(_attn_kernel, n_ctx=n_ctx, tq=tq, out_scale=1.0 - lam_init),
        grid=(b, DF_HEADS, tt // tq),
        in_specs=[pl.BlockSpec(memory_space=pltpu.SMEM),
                  pl.BlockSpec((None, tq, w), lambda bi, hi, i: (bi, i, q0 + hi)),
                  pl.BlockSpec((None, tt, w), lambda bi, hi, i: (bi, 0, k0 + hi)),
                  pl.BlockSpec((None, tt, w), lambda bi, hi, i: (bi, 0, v0 + hi)),
                  pl.BlockSpec((tq, w), lambda bi, hi, i: (i, 0)),
                  pl.BlockSpec((tq, w), lambda bi, hi, i: (i, 0)),
                  full((tt, w)), full((tt, w)), full((1, w)), full((1, w)), full((1, w)), full((w, w))],
        out_specs=pl.BlockSpec((None, tq, w), lambda bi, hi, i: (bi, i, hi)),
        out_shape=jax.ShapeDtypeStruct((b, tt, DF_WIDTH), BF16),
        scratch_shapes=[pltpu.VMEM((tt, w), BF16), pltpu.VMEM((tt, 2 * w), BF16)],
        compiler_params=_cparams(("parallel", "parallel", "arbitrary")),
        name="diff_attention",
    )(lam.reshape(1).astype(F32), z, z, z, cos, sin, cos, sin, two(p["qn"]), two(p["kn"]),
      p["subln"].reshape(1, w), _group_ones(w, DF_HEAD_DIM))


def _mix_out_kernel(yf_ref, yb_ref, bonus_ref, gate_ref, ob_ref, gnw_ref, gnb_ref, seg_ref, w_ref, h_ref, mod_ref, o_ref):
    seg = seg_ref[...]
    y = yf_ref[...] + yb_ref[...]
    mu = _group_sum(y, seg) * (1.0 / RW_HEAD_DIM)
    d = y - mu
    var = _group_sum(d * d, seg) * (1.0 / RW_HEAD_DIM)
    yn = d * lax.rsqrt(var + RW_GN_EPS) * gnw_ref[...] + gnb_ref[...]
    ya = (yn + bonus_ref[...]) * gate_ref[...]
    mix = _dot(ya.astype(BF16), w_ref[:RW_WIDTH, :]) + _dot(ob_ref[...], w_ref[RW_WIDTH:, :])
    o_ref[...] = h_ref[...] + mod_ref[2:3, :] * mix


def _mix_out(yf, yb, bonus, gate, ob, p, h, mods, n_ctx):
    b, tt, d = h.shape
    c = RW_WIDTH
    tm = ROW_TILE
    tok = lambda width: pl.BlockSpec((None, tm, width), lambda bi, i: (bi, i, 0))
    full = lambda shape: pl.BlockSpec(shape, lambda bi, i: (0,) * len(shape))
    return pl.pallas_call(
        _mix_out_kernel,
        grid=(b, tt // tm),
        in_specs=[tok(c), tok(c), tok(c), tok(c), tok(DF_WIDTH), full((1, c)), full((1, c)), full((c, c)),
                  full((c + DF_WIDTH, d)), tok(d),
                  pl.BlockSpec((None, None, 6, d), lambda bi, i: (bi, _kind(i, n_ctx), 0, 0))],
        out_specs=tok(d),
        out_shape=jax.ShapeDtypeStruct((b, tt, d), F32),
        compiler_params=_cparams(("parallel", "parallel")),
        name="mix_out_residual",
    )(yf, yb, bonus, gate, ob, p["gn_w"].reshape(1, c), p["gn_b"].reshape(1, c), _group_ones(c, RW_HEAD_DIM),
      p["w_out"].astype(BF16), h, mods)


def _sgu_kernel(h_ref, mod_ref, g_ref, w_in_ref, b_in_ref, lng_ref, lnb_ref, ws_ref, bst_ref, w_out_ref, o_ref):
    x = h_ref[...]
    tm = x.shape[0]
    y = _norm_mod(x, g_ref[...], mod_ref[1:2, :], mod_ref[0:1, :])
    z = _dot(y.astype(BF16), w_in_ref[...]) + b_in_ref[...]
    z = 0.5 * z * (1.0 + lax.erf(z * (2.0 ** -0.5)))
    u = z[:, :SGU_WIDTH]
    v = z[:, SGU_WIDTH:]
    mu = jnp.mean(v, axis=-1, keepdims=True)
    var = jnp.mean(jnp.square(v - mu), axis=-1, keepdims=True)
    v = ((v - mu) * lax.rsqrt(var + LN_EPS) * lng_ref[...] + lnb_ref[...]).astype(BF16)
    rows = []
    for ci in range(tm // SGU_CHUNK):
        cols = []
        for gi in range(SGU_GROUPS):
            vb = v[ci * SGU_CHUNK:(ci + 1) * SGU_CHUNK, gi * SGU_GROUP_DIM:(gi + 1) * SGU_GROUP_DIM]
            cols.append(_dot(ws_ref[gi], vb) + bst_ref[:, gi:gi + 1])
        rows.append(jnp.concatenate(cols, axis=1))
    sv = jnp.concatenate(rows, axis=0)
    o_ref[...] = x + mod_ref[2:3, :] * _dot((u * sv).astype(BF16), w_out_ref[...])


def _sgu_layer(h, mods, g, p, n_ctx):
    b, tt, d = h.shape
    tm = ROW_TILE
    full = lambda shape: pl.BlockSpec(shape, lambda bi, i: (0,) * len(shape))
    return pl.pallas_call(
        _sgu_kernel,
        grid=(b, tt // tm),
        in_specs=[pl.BlockSpec((None, tm, d), lambda bi, i: (bi, i, 0)),
                  pl.BlockSpec((None, None, 6, d), lambda bi, i: (bi, _kind(i, n_ctx), 0, 0)),
                  full((1, d)), full((d, 2 * SGU_WIDTH)), full((1, 2 * SGU_WIDTH)),
                  full((1, SGU_WIDTH)), full((1, SGU_WIDTH)),
                  full((SGU_GROUPS, SGU_CHUNK, SGU_CHUNK)), full((SGU_CHUNK, SGU_GROUPS)),
                  full((SGU_WIDTH, d))],
        out_specs=pl.BlockSpec((None, tm, d), lambda bi, i: (bi, i, 0)),
        out_shape=jax.ShapeDtypeStruct((b, tt, d), F32),
        compiler_params=_cparams(("parallel", "parallel")),
        name="sgu_layer",
    )(h, mods, g.reshape(1, d), p["w_in"].astype(BF16), p["b_in"].reshape(1, -1),
      p["ln_g"].reshape(1, -1), p["ln_b"].reshape(1, -1), p["ws"].astype(BF16), p["bs"].T,
      p["w_out"].astype(BF16))


def _router_kernel(h_ref, mod_ref, g_ref, rw_ref, rb_ref, f_ref, idx_ref, wt_ref, rank_ref, cnt_ref, base_ref):
    first = jnp.logical_and(pl.program_id(0) == 0, pl.program_id(1) == 0)

    @pl.when(first)
    def _():
        base_ref[...] = jnp.zeros_like(base_ref)

    x = h_ref[...]
    tm = x.shape[0]
    f = _norm_mod(x, g_ref[...], mod_ref[4:5, :], mod_ref[3:4, :])
    f_ref[...] = f.astype(f_ref.dtype)
    logits = _dot(f, rw_ref[...], HIGHEST) + rb_ref[...]
    lane = lax.broadcasted_iota(jnp.int32, (tm, LANES), 1)
    tri = (lax.broadcasted_iota(jnp.int32, (tm, tm), 1) < lax.broadcasted_iota(jnp.int32, (tm, tm), 0))
    vals, sels = [], []
    idx_out = jnp.zeros((tm, LANES), jnp.int32)
    l = logits
    for kk in range(TOP_K):
        m = jnp.max(l, axis=-1, keepdims=True)
        idx = jnp.min(jnp.where(l == m, lane, LANES), axis=-1, keepdims=True)
        sel = lane == idx
        l = jnp.where(sel, -jnp.inf, l)
        vals.append(m)
        sels.append(sel)
        idx_out = jnp.where(lane == kk, idx, idx_out)
    es = [jnp.exp(vv - vals[0]) for vv in vals]
    den = es[0] + es[1] + es[2] + es[3]
    wt_out = jnp.zeros((tm, LANES), F32)
    for kk in range(TOP_K):
        wt_out = jnp.where(lane == kk, es[kk] / den, wt_out)
    onehot = jnp.zeros((tm, LANES), F32)
    for sel in sels:
        onehot = onehot + sel.astype(F32)
    ahead = _dot(tri.astype(BF16), onehot.astype(BF16)) + base_ref[...]
    rank_out = jnp.zeros((tm, LANES), jnp.int32)
    for kk in range(TOP_K):
        rk = jnp.sum(jnp.where(sels[kk], ahead, 0.0), axis=-1, keepdims=True).astype(jnp.int32)
        rank_out = jnp.where(lane == kk, rk, rank_out)
    idx_ref[...] = idx_out
    wt_ref[...] = wt_out
    rank_ref[...] = rank_out
    base_ref[...] = base_ref[...] + jnp.sum(onehot, axis=0, keepdims=True)
    cnt_ref[...] = base_ref[...]


def _router(h, mods, g, rt_w, rt_b, n_ctx, b0, b):
    _, tt, d = h.shape
    tm = ROW_TILE
    rw = jnp.zeros((d, LANES), F32).at[:, :N_EXPERTS].set(rt_w)
    rb = jnp.full((1, LANES), -jnp.inf, F32).at[0, :N_EXPERTS].set(rt_b)
    tok = lambda width: pl.BlockSpec((None, tm, width), lambda bi, i: (bi, i, 0))
    return pl.pallas_call(
        _router_kernel,
        grid=(b, tt // tm),
        in_specs=[pl.BlockSpec((None, tm, d), lambda bi, i: (b0 + bi, i, 0)),
                  pl.BlockSpec((None, None, 6, d), lambda bi, i: (b0 + bi, _kind(i, n_ctx), 0, 0)),
                  pl.BlockSpec((1, d), lambda bi, i: (0, 0)),
                  pl.BlockSpec((d, LANES), lambda bi, i: (0, 0)),
                  pl.BlockSpec((1, LANES), lambda bi, i: (0, 0))],
        out_specs=[tok(d), tok(LANES), tok(LANES), tok(LANES), pl.BlockSpec((1, LANES), lambda bi, i: (0, 0))],
        out_shape=[jax.ShapeDtypeStruct((b, tt, d), BF16),
                   jax.ShapeDtypeStruct((b, tt, LANES), jnp.int32),
                   jax.ShapeDtypeStruct((b, tt, LANES), F32),
                   jax.ShapeDtypeStruct((b, tt, LANES), jnp.int32),
                   jax.ShapeDtypeStruct((1, LANES), F32)],
        scratch_shapes=[pltpu.VMEM((1, LANES), F32)],
        compiler_params=_cparams(("arbitrary", "arbitrary")),
        name="moe_router",
    )(h, mods, g.reshape(1, d), rw, rb)


def _expert_kernel(te_ref, nv_ref, x_ref, w1_ref, b1_ref, w2_ref, b2_ref, o_ref, w1b_ref, w2b_ref):
    j = pl.program_id(0)
    active = j < nv_ref[0]
    new_expert = jnp.logical_or(j == 0, te_ref[j] != te_ref[jnp.maximum(j - 1, 0)])

    @pl.when(jnp.logical_and(active, new_expert))
    def _():
        w1b_ref[...] = w1_ref[...].astype(BF16)
        w2b_ref[...] = w2_ref[...].astype(BF16)

    @pl.when(active)
    def _():
        f = w2b_ref.shape[0]
        fh = f // 2
        x = x_ref[...]
        y = b2_ref[...]
        for c in range(2):
            lo, hi = c * fh, (c + 1) * fh
            glu = _dot(x, w1b_ref[:, lo:hi]) + b1_ref[:, lo:hi]
            lin = _dot(x, w1b_ref[:, f + lo:f + hi]) + b1_ref[:, f + lo:f + hi]
            glu = jnp.minimum(glu, SWIGLU_LIMIT)
            lin = jnp.clip(lin, -SWIGLU_LIMIT, SWIGLU_LIMIT)
            act = glu * jax.nn.sigmoid(SWIGLU_ALPHA * glu) * (lin + 1.0)
            y = y + _dot(act.astype(BF16), w2b_ref[lo:hi, :])
        o_ref[...] = y.astype(o_ref.dtype)

    @pl.when(jnp.logical_not(active))
    def _():
        o_ref[...] = jnp.zeros_like(o_ref)


def _experts(xs, tile_expert, n_valid, w1, b1, w2, b2, layer):
    s, d = xs.shape
    depth, e, _, f2 = w1.shape
    f = f2 // 2
    tm = EXPERT_ROW_TILE
    grid_spec = pltpu.PrefetchScalarGridSpec(
        num_scalar_prefetch=2,
        grid=(s // tm,),
        in_specs=[pl.BlockSpec((tm, d), lambda j, te, nv: (j, 0)),
                  pl.BlockSpec((None, None, d, f2), lambda j, te, nv: (layer, te[j], 0, 0)),
                  pl.BlockSpec((None, None, 1, f2), lambda j, te, nv: (layer, te[j], 0, 0)),
                  pl.BlockSpec((None, None, f, d), lambda j, te, nv: (layer, te[j], 0, 0)),
                  pl.BlockSpec((None, None, 1, d), lambda j, te, nv: (layer, te[j], 0, 0))],
        out_specs=pl.BlockSpec((tm, d), lambda j, te, nv: (j, 0)),
        scratch_shapes=[pltpu.VMEM((d, f2), BF16), pltpu.VMEM((f, d), BF16)],
    )
    return pl.pallas_call(
        _expert_kernel,
        grid_spec=grid_spec,
        out_shape=jax.ShapeDtypeStruct((s, d), BF16),
        compiler_params=_cparams(("arbitrary",)),
        name="moe_experts",
    )(tile_expert, n_valid, xs, w1, b1.reshape(depth, e, 1, f2), w2, b2.reshape(depth, e, 1, d))


def _combine_kernel(p_ref, wt_ref, h_ref, mod_ref, o_ref):
    m = p_ref[0].astype(F32) * wt_ref[:, 0:1]
    for k in range(1, TOP_K):
        m = m + p_ref[k].astype(F32) * wt_ref[:, k:k + 1]
    o_ref[...] = h_ref[...] + mod_ref[5:6, :] * m


def _combine(picked, wt, h, mods, n_ctx, b0, b):
    _, tt, d = h.shape
    tm = ROW_TILE
    return pl.pallas_call(
        _combine_kernel,
        grid=(b, tt // tm),
        in_specs=[pl.BlockSpec((TOP_K, None, tm, d), lambda bi, i: (0, bi, i, 0)),
                  pl.BlockSpec((None, tm, LANES), lambda bi, i: (bi, i, 0)),
                  pl.BlockSpec((None, tm, d), lambda bi, i: (b0 + bi, i, 0)),
                  pl.BlockSpec((None, None, 6, d), lambda bi, i: (b0 + bi, _kind(i, n_ctx), 0, 0))],
        out_specs=pl.BlockSpec((None, tm, d), lambda bi, i: (bi, i, 0)),
        out_shape=jax.ShapeDtypeStruct((b, tt, d), F32),
        compiler_params=_cparams(("parallel", "parallel")),
        name="moe_combine",
    )(picked, wt, h, mods)


def _moe(h, mods, g, rt_w, rt_b, w1, b1, w2, b2, layer, n_ctx, b0, b):
    _, tt, d = h.shape
    n = b * tt
    tm = EXPERT_ROW_TILE
    f, idx, wt, rank, cnt = _router(h, mods, g, rt_w, rt_b, n_ctx, b0, b)
    top_i = idx.reshape(n, LANES)[:, :TOP_K]
    rank = rank.reshape(n, LANES)[:, :TOP_K]
    counts = cnt[0, :N_EXPERTS].astype(jnp.int32)
    padded = ((counts + tm - 1) // tm) * tm
    p_end = jnp.cumsum(padded)
    p_off = p_end - padded
    u_off = jnp.cumsum(counts) - counts
    slot = (p_off[top_i] + rank).reshape(-1)
    n_tiles = (n * TOP_K) // tm + N_EXPERTS
    s_rows = n_tiles * tm
    _, sorted_pair = lax.sort_key_val(slot, jnp.arange(n * TOP_K, dtype=jnp.int32))
    srow = jnp.arange(s_rows, dtype=jnp.int32)
    expert_of = lambda rows: jnp.minimum(jnp.sum((rows[:, None] >= p_end[None, :]).astype(jnp.int32), axis=1),
                                         N_EXPERTS - 1)
    row_e = expert_of(srow)
    within = srow - p_off[row_e]
    valid = within < counts[row_e]
    src = jnp.clip(u_off[row_e] + within, 0, n * TOP_K - 1)
    pair = sorted_pair[src]
    row_token = jnp.where(valid, pair // TOP_K, 0)
    n_valid = (p_end[-1] // tm).astype(jnp.int32).reshape(1)
    tile_start = jnp.arange(n_tiles, dtype=jnp.int32) * tm
    tile_expert = expert_of(tile_start)
    last_e = tile_expert[jnp.maximum(n_valid[0] - 1, 0)]
    tile_expert = jnp.where(tile_start < p_end[-1], tile_expert, last_e)
    xs = jnp.take(f.reshape(n, d), row_token, axis=0)
    ys = _experts(xs, tile_expert, n_valid, w1, b1, w2, b2, layer)
    picked = jnp.take(ys, slot.reshape(n, TOP_K).T.reshape(-1), axis=0).reshape(TOP_K, b, tt, d)
    return _combine(picked, wt, h, mods, n_ctx, b0, b)


def _rope_tables(n_tokens, n_ctx):
    rows = n_tokens // GRID_W
    row = jnp.repeat(jnp.arange(rows, dtype=F32), GRID_W)
    col = jnp.tile(jnp.arange(GRID_W, dtype=F32), rows)
    axis_dim = DF_HEAD_DIM // 2
    inv_freq = ROPE_THETA ** (-jnp.arange(0, axis_dim, 2, dtype=F32) / axis_dim)
    ar = row[:, None] * inv_freq
    ac = col[:, None] * inv_freq
    ang = jnp.concatenate([ar, ar, ac, ac], axis=-1)
    cos = jnp.concatenate([jnp.ones((n_ctx, DF_HEAD_DIM), F32), jnp.cos(ang)], axis=0)
    sin = jnp.concatenate([jnp.zeros((n_ctx, DF_HEAD_DIM), F32), jnp.sin(ang)], axis=0)
    return jnp.tile(cos, (1, 2)), jnp.tile(sin, (1, 2))


def _even_layer(h, mods, g1, p, li, n_ctx, rope):
    z = _nm_mm(h, mods, g1, p["w_in"].astype(BF16), n_ctx)
    r, v, kk, kd0, kd1, lw0, lw1, bb0, bb1, gate, bonus = _rw_features(z, p, n_ctx)
    yf, yb = _rw_scan(r, v, kk, kd0, lw0, bb0, kd1, lw1, bb1, n_ctx)
    lam_init = 0.8 - 0.6 * math.exp(-0.3 * li)
    lp = p["lam"].astype(F32)
    lam = jnp.exp(jnp.sum(lp[0] * lp[1])) - jnp.exp(jnp.sum(lp[2] * lp[3])) + lam_init
    ob = _diff_attention(z, p, lam, lam_init, rope[0], rope[1], n_ctx)
    return _mix_out(yf, yb, bonus, gate, ob, p, h, mods, n_ctx)


def kernel(x, c, ctx, c_ctx, norm1_g, norm2_g, ada_w, ada_b, ev_w_in, ev_w_out, rw_mu, rw_w0, rw_w2, rw_a0, rw_a2, rw_g2, rw_kk, rw_ka, rw_rk, rw_gn_w, rw_gn_b, df_qn, df_kn, df_lam, df_subln, sg_w_in, sg_b_in, sg_ln_g, sg_ln_b, sg_ws, sg_bs, sg_w_out, rt_w, rt_b, ex_w1, ex_b1, ex_w2, ex_b2):
    b, t, d = x.shape
    n_ctx = ctx.shape[1]
    depth = ada_w.shape[0]
    assert d == D_MODEL and t % ROW_TILE == 0 and n_ctx % ROW_TILE == 0 and t % GRID_W == 0
    rope = _rope_tables(t, n_ctx)
    r_pad = -(-(b + 1) // 8) * 8
    cs = jnp.zeros((r_pad, d), F32).at[:b].set(c).at[b].set(c_ctx)
    ada = _ada_table(cs, ada_w, ada_b).reshape(depth, r_pad, 6, d)
    h = jnp.concatenate([ctx, x], axis=1)
    for l in range(depth):
        ctx_out = any(m % 2 == 0 for m in range(l + 1, depth))
        j = l // 2
        mods = jnp.stack([jnp.broadcast_to(ada[l, b], (b, 6, d)), ada[l, :b]], axis=1)
        if l % 2 == 0:
            p = {"w_in": ev_w_in[j], "w_out": ev_w_out[j], "mu": rw_mu[j], "w0": rw_w0[j], "w2": rw_w2[j],
                 "a0": rw_a0[j], "a2": rw_a2[j], "g2": rw_g2[j], "k_k": rw_kk[j], "k_a": rw_ka[j],
                 "r_k": rw_rk[j], "gn_w": rw_gn_w[j], "gn_b": rw_gn_b[j], "qn": df_qn[j], "kn": df_kn[j],
                 "lam": df_lam[j], "subln": df_subln[j]}
            h = _even_layer(h, mods, norm1_g[l], p, l, n_ctx, rope)
        else:
            p = {"w_in": sg_w_in[j], "b_in": sg_b_in[j], "ln_g": sg_ln_g[j], "ln_b": sg_ln_b[j],
                 "ws": sg_ws[j], "bs": sg_bs[j], "w_out": sg_w_out[j]}
            h = _sgu_layer(h, mods, norm1_g[l], p, n_ctx)
        if n_ctx > 0 and not ctx_out:
            h = h[:, n_ctx:]
            n_ctx = 0
        splits = ((0, b // 2), (b // 2, b - b // 2)) if b > 1 else ((0, b),)
        h = jnp.concatenate([_moe(h, mods, norm2_g[l], rt_w[l], rt_b[l], ex_w1, ex_b1, ex_w2, ex_b2, l, n_ctx, b0, nb)
                             for b0, nb in splits], axis=0)
    return h[:, n_ctx:]
```

```python
import functools
import math

import jax
import jax.numpy as jnp
from jax import lax
from jax.experimental import pallas as pl
from jax.experimental.pallas import tpu as pltpu

F32 = jnp.float32
BF16 = jnp.bfloat16
HIGHEST = lax.Precision.HIGHEST

D_MODEL = 1024
GRID_W = 64
RMS_EPS = 1e-6
LN_EPS = 1e-5

RW_HEAD_DIM = 64
RW_WIDTH = 512
RW_HEADS = RW_WIDTH // RW_HEAD_DIM
DECAY_LORA = 64
ICLR_LORA = 64
GATE_LORA = 128
RW_COLS = 3 * RW_WIDTH + 2 * DECAY_LORA + 2 * ICLR_LORA + GATE_LORA
RW_GN_EPS = 64e-5

DF_HEAD_DIM = 64
DF_V_DIM = 128
DF_WIDTH = 512
DF_HEADS = DF_WIDTH // DF_V_DIM
DF_QK = DF_HEADS * 2 * DF_HEAD_DIM
ROPE_THETA = 10000.0
EVEN_COLS = RW_COLS + 2 * DF_QK + DF_WIDTH

SGU_CHUNK = 128
SGU_WIDTH = 1024
SGU_GROUPS = 8
SGU_GROUP_DIM = SGU_WIDTH // SGU_GROUPS

N_EXPERTS = 32
TOP_K = 4
SWIGLU_LIMIT = 7.0
SWIGLU_ALPHA = 1.702

LANES = 128
ROW_TILE = 256
SCAN_CHUNK = 64
EXPERT_ROW_TILE = 512
VMEM_LIMIT = 56 * 1024 * 1024


def _cparams(sem):
    return pltpu.CompilerParams(dimension_semantics=sem, vmem_limit_bytes=VMEM_LIMIT)


def _dot(a, b, precision=None):
    return jnp.dot(a, b, preferred_element_type=F32, precision=precision)


def _dot_nt(a, b, precision=None):
    return lax.dot_general(a, b, (((1,), (1,)), ((), ())), preferred_element_type=F32, precision=precision)


def _dot_tn(a, b, precision=None):
    return lax.dot_general(a, b, (((0,), (0,)), ((), ())), preferred_element_type=F32, precision=precision)


def _norm_mod(x, g, scale, shift):
    ms = jnp.mean(x * x, axis=-1, keepdims=True)
    return (x * lax.rsqrt(ms + RMS_EPS) * g) * (1.0 + scale) + shift


def _ada_kernel(x_ref, w_ref, b_ref, o_ref):
    x = x_ref[...]
    s = x * jax.nn.sigmoid(x)
    o_ref[...] = _dot(s, w_ref[...], HIGHEST) + b_ref[...]


def _ada_table(cs, ada_w, ada_b):
    depth, d, n = ada_w.shape
    r = cs.shape[0]
    tn = 1024
    return pl.pallas_call(
        _ada_kernel,
        grid=(depth, n // tn),
        in_specs=[pl.BlockSpec((r, d), lambda l, j: (0, 0)),
                  pl.BlockSpec((None, d, tn), lambda l, j: (l, 0, j)),
                  pl.BlockSpec((None, 1, tn), lambda l, j: (l, 0, j))],
        out_specs=pl.BlockSpec((None, r, tn), lambda l, j: (l, 0, j)),
        out_shape=jax.ShapeDtypeStruct((depth, r, n), F32),
        compiler_params=_cparams(("parallel", "parallel")),
        name="ada_table",
    )(cs, ada_w, ada_b.reshape(depth, 1, n))


def _kind(i, n_ctx):
    return (i * ROW_TILE >= n_ctx).astype(jnp.int32) if n_ctx > 0 else 1


def _nm_mm_kernel(h_ref, mod_ref, g_ref, w_ref, o_ref):
    y = _norm_mod(h_ref[...], g_ref[...], mod_ref[1:2, :], mod_ref[0:1, :])
    o_ref[...] = _dot(y.astype(BF16), w_ref[...]).astype(o_ref.dtype)


def _nm_mm(h, mods, g, w, n_ctx, out_dtype=F32):
    b, tt, d = h.shape
    n = w.shape[1]
    tm = ROW_TILE
    return pl.pallas_call(
        _nm_mm_kernel,
        grid=(b, tt // tm),
        in_specs=[pl.BlockSpec((None, tm, d), lambda bi, i: (bi, i, 0)),
                  pl.BlockSpec((None, None, 6, d), lambda bi, i: (bi, _kind(i, n_ctx), 0, 0)),
                  pl.BlockSpec((1, d), lambda bi, i: (0, 0)),
                  pl.BlockSpec((d, n), lambda bi, i: (0, 0))],
        out_specs=pl.BlockSpec((None, tm, n), lambda bi, i: (bi, i, 0)),
        out_shape=jax.ShapeDtypeStruct((b, tt, n), out_dtype),
        compiler_params=_cparams(("parallel", "parallel")),
        name="norm_mod_proj",
    )(h, mods, g.reshape(1, d), w)


def _group_ones(width, group):
    idx = jnp.arange(width) // group
    return (idx[:, None] == idx[None, :]).astype(BF16)


def _group_sum(x, ones_bf16):
    hi = x.astype(BF16)
    lo = (x - hi.astype(F32)).astype(BF16)
    return _dot(hi, ones_bf16) + _dot(lo, ones_bf16)


def _softplus(x):
    return jnp.maximum(x, 0.0) + jnp.log1p(jnp.exp(-jnp.abs(x)))


def _rw_feat_kernel(z_ref, zprev_ref, znext_ref, mu_ref, w0_ref, w2_ref, a0_ref, a2_ref, g2_ref, kk_ref, ka_ref,
                    rk_ref, seg_ref,
                    r_ref, v_ref, kko_ref, kd0_ref, kd1_ref, lw0_ref, lw1_ref, bb0_ref, bb1_ref, g_ref, bonus_ref,
                    *, n_ctx, tt):
    c = RW_WIDTH
    tm = z_ref.shape[0]
    i = pl.program_id(1)
    z = z_ref[...]
    seq_first = jnp.logical_or(i * tm == 0, i * tm == n_ctx)
    seq_last = jnp.logical_or((i + 1) * tm == n_ctx, (i + 1) * tm == tt)
    prev_row = jnp.where(seq_first, 0.0, zprev_ref[7:8, :])
    next_row = jnp.where(seq_last, 0.0, znext_ref[0:1, :])
    rows = lax.broadcasted_iota(jnp.int32, z.shape, 0)
    zp = jnp.where(rows == 0, prev_row, pltpu.roll(z, 1, 0))
    zn = jnp.where(rows == tm - 1, next_row, pltpu.roll(z, tm - 1, 0))
    za = z + mu_ref[0:1, :] * (zp - z) + mu_ref[1:2, :] * (zn - z)
    r = za[:, :c]
    k = za[:, c:2 * c]
    v = za[:, 2 * c:3 * c]
    wl = za[:, 3 * c:3 * c + 2 * DECAY_LORA]
    al = za[:, 3 * c + 2 * DECAY_LORA:3 * c + 2 * DECAY_LORA + 2 * ICLR_LORA]
    gl = za[:, 3 * c + 2 * DECAY_LORA + 2 * ICLR_LORA:]
    r_ref[...] = r
    v_ref[...] = v
    twl = jnp.tanh(wl)
    seg = seg_ref[...]
    kkr = k * kk_ref[...]
    kk = kkr / jnp.maximum(jnp.sqrt(_group_sum(kkr * kkr, seg)), 1e-12)
    kko_ref[...] = kk
    lw_refs = (lw0_ref, lw1_ref)
    bb_refs = (bb0_ref, bb1_ref)
    kd_refs = (kd0_ref, kd1_ref)
    kd_sum = None
    for d in range(2):
        wd = w0_ref[d:d + 1, :] + _dot(twl[:, d * DECAY_LORA:(d + 1) * DECAY_LORA], w2_ref[d], HIGHEST)
        w = -_softplus(-wd) - 0.5
        lw_refs[d][...] = -jnp.exp(w)
        a = jax.nn.sigmoid(a0_ref[d:d + 1, :] + _dot(al[:, d * ICLR_LORA:(d + 1) * ICLR_LORA], a2_ref[d], HIGHEST))
        bb_refs[d][...] = kk * a
        kd = k * (1.0 + (a - 1.0) * ka_ref[...])
        kd_refs[d][...] = kd
        kd_sum = kd if kd_sum is None else kd_sum + kd
    g_ref[...] = _dot(jax.nn.sigmoid(gl), g2_ref[...], HIGHEST)
    bonus_ref[...] = _group_sum(r * kd_sum * rk_ref[...], seg) * v


def _rw_features(z, p, n_ctx):
    b, tt, _ = z.shape
    c = RW_WIDTH
    tm = ROW_TILE
    hb = tm // 8
    last_halo = tt // 8 - 1
    full = lambda shape: pl.BlockSpec(shape, lambda bi, i: (0,) * len(shape))
    out_spec = pl.BlockSpec((None, tm, c), lambda bi, i: (bi, i, 0))
    return pl.pallas_call(
        functools.partial(_rw_feat_kernel, n_ctx=n_ctx, tt=tt),
        grid=(b, tt // tm),
        in_specs=[pl.BlockSpec((None, tm, RW_COLS), lambda bi, i: (bi, i, 0)),
                  pl.BlockSpec((None, 8, RW_COLS), lambda bi, i: (bi, jnp.maximum(i * hb - 1, 0), 0)),
                  pl.BlockSpec((None, 8, RW_COLS), lambda bi, i: (bi, jnp.minimum((i + 1) * hb, last_halo), 0)),
                  full((2, RW_COLS)), full((2, c)), full((2, DECAY_LORA, c)), full((2, c)), full((2, ICLR_LORA, c)),
                  full((GATE_LORA, c)), full((1, c)), full((1, c)), full((1, c)), full((c, c))],
        out_specs=[out_spec] * 11,
        out_shape=[jax.ShapeDtypeStruct((b, tt, c), F32)] * 11,
        compiler_params=_cparams(("parallel", "parallel")),
        name="rwkv_features",
    )(z, z, z, p["mu"], p["w0"], p["w2"], p["a0"], p["a2"], p["g2"], p["k_k"].reshape(1, c), p["k_a"].reshape(1, c),
      p["r_k"].reshape(1, c), _group_ones(c, RW_HEAD_DIM))


def _cumsum_rows(tri_bf16, x):
    hi = x.astype(BF16)
    r1 = x - hi.astype(F32)
    mid = r1.astype(BF16)
    lo = (r1 - mid.astype(F32)).astype(BF16)
    return _dot(tri_bf16, hi) + _dot(tri_bf16, mid) + _dot(tri_bf16, lo)


def _scan_masks(reverse):
    L = SCAN_CHUNK
    row = lax.broadcasted_iota(jnp.int32, (L, L), 0)
    col = lax.broadcasted_iota(jnp.int32, (L, L), 1)
    incl = (col >= row) if reverse else (col <= row)
    strict = (col > row) if reverse else (col < row)
    levels = []
    for lg in range(int(math.log2(L))):
        same_pair = jnp.right_shift(row, lg + 1) == jnp.right_shift(col, lg + 1)
        other_half = jnp.right_shift(row, lg) != jnp.right_shift(col, lg)
        levels.append(jnp.logical_and(jnp.logical_and(same_pair, other_half), strict))
    return incl, strict, row == col, levels


def _scan_operands(r_ref, v_ref, kk_ref, kd_ref, lw_ref, bb_ref, incl, reverse):
    L = SCAN_CHUNK
    lw = lw_ref[...]
    cum = _cumsum_rows(incl.astype(BF16), lw)
    ctot = cum[0:1, :] if reverse else cum[L - 1:L, :]
    g_inv = jnp.exp(-cum)
    g_end = jnp.exp(ctot - cum)
    kd = kd_ref[...]
    bb = bb_ref[...]
    r_f = r_ref[...] * jnp.exp(cum)
    return dict(a=(-kk_ref[...] * jnp.exp(cum - lw)).astype(BF16), b=(bb * g_inv).astype(BF16),
                k=(kd * g_inv).astype(BF16), r_f=r_f, r=r_f.astype(BF16), b_e=(bb * g_end).astype(BF16),
                k_e=(kd * g_end).astype(BF16), v=v_ref[...].astype(BF16), g_tot=jnp.exp(ctot))


def _scan_kernel(rf_ref, vf_ref, kkf_ref, kdf_ref, lwf_ref, bbf_ref, rb_ref, vb_ref, kkb_ref, kdb_ref, lwb_ref, bbb_ref,
                 yf_ref, yb_ref, s_ref):
    L = SCAN_CHUNK
    n = RW_HEAD_DIM

    @pl.when(pl.program_id(1) == 0)
    def _():
        s_ref[...] = jnp.zeros_like(s_ref)

    masks = (_scan_masks(False), _scan_masks(True))
    ops = (_scan_operands(rf_ref, vf_ref, kkf_ref, kdf_ref, lwf_ref, bbf_ref, masks[0][0], False),
           _scan_operands(rb_ref, vb_ref, kkb_ref, kdb_ref, lwb_ref, bbb_ref, masks[1][0], True))
    y_refs = (yf_ref, yb_ref)
    chains = [(d, h) for h in range(RW_HEADS) for d in range(2)]
    cut = lambda d, h, name: ops[d][name][:, h * n:(h + 1) * n]

    gram = [_dot_nt(jnp.concatenate([cut(d, h, "a"), cut(d, h, "r")], axis=0),
                    jnp.concatenate([cut(d, h, "b"), cut(d, h, "k")], axis=0)) for d, h in chains]
    n_ab = [g[:L, :L] for g in gram]
    lower = [jnp.concatenate([jnp.where(masks[d][1], g[:L, L:], 0.0), jnp.where(masks[d][0], g[L:, L:], 0.0)],
                             axis=0).astype(BF16) for (d, h), g in zip(chains, gram)]
    m_rb = [jnp.where(masks[d][0], g[L:, :L], 0.0).astype(BF16) for (d, h), g in zip(chains, gram)]
    nv = [_dot(lo, cut(d, h, "v")) for (d, h), lo in zip(chains, lower)]
    t_inv = [jnp.where(masks[d][2], 1.0, jnp.where(masks[d][3][0], nab, 0.0)) for (d, h), nab in zip(chains, n_ab)]
    for lv in range(1, len(masks[0][3])):
        tb = [t.astype(BF16) for t in t_inv]
        half = [_dot(t, jnp.where(masks[d][3][lv], nab, 0.0).astype(BF16)).astype(BF16)
                for (d, h), t, nab in zip(chains, tb, n_ab)]
        t_inv = [t + _dot(hf, t16) for t, hf, t16 in zip(t_inv, half, tb)]
    au = [_dot(t.astype(BF16), jnp.concatenate([cut(d, h, "a"), x[:L].astype(BF16)], axis=1)).astype(BF16)
          for (d, h), t, x in zip(chains, t_inv, nv)]
    ry = [_dot(m, x) for m, x in zip(m_rb, au)]
    pq = [_dot_tn(x, cut(d, h, "b_e")) for (d, h), x in zip(chains, au)]
    vk = [_dot_tn(cut(d, h, "v"), cut(d, h, "k_e")) for d, h in chains]
    for i, (d, h) in enumerate(chains):
        sl = slice(h * n, (h + 1) * n)
        r_hat = (ops[d]["r_f"][:, sl] + ry[i][:, :n]).astype(BF16)
        y_hat = ry[i][:, n:] + nv[i][L:]
        p_mat = jnp.where(masks[d][2], ops[d]["g_tot"][:, sl], 0.0) + pq[i][:n]
        q_mat = pq[i][n:] + vk[i]
        s0b = s_ref[d, h].astype(BF16)
        y_refs[d][:, sl] = _dot_nt(r_hat, s0b) + y_hat
        s_ref[d, h] = _dot(s0b, p_mat.astype(BF16)) + q_mat


def _rw_scan(r, v, kk, kd0, lw0, bb0, kd1, lw1, bb1, n_ctx):
    b, tt, c = r.shape
    L = SCAN_CHUNK
    nc = tt // L
    ncc = n_ctx // L
    fwd = pl.BlockSpec((None, L, c), lambda bi, i: (bi, i, 0))
    bwd = pl.BlockSpec((None, L, c), lambda bi, i: (bi, jnp.where(i < ncc, ncc - 1 - i, nc - 1 - (i - ncc)), 0))
    return pl.pallas_call(
        _scan_kernel,
        grid=(b, nc),
        in_specs=[fwd] * 6 + [bwd] * 6,
        out_specs=[fwd, bwd],
        out_shape=[jax.ShapeDtypeStruct((b, tt, c), F32)] * 2,
        scratch_shapes=[pltpu.VMEM((2, RW_HEADS, RW_HEAD_DIM, RW_HEAD_DIM), F32)],
        compiler_params=_cparams(("parallel", "arbitrary")),
        name="rwkv_scan",
    )(r, v, kk, kd0, lw0, bb0, r, v, kk, kd1, lw1, bb1)


def _qk_prep(x, gain, cos, sin, ones):
    xn = x * lax.rsqrt(_group_sum(x * x, ones) * (1.0 / DF_HEAD_DIM) + RMS_EPS) * gain
    lane = lax.broadcasted_iota(jnp.int32, x.shape, 1)
    quarter = DF_HEAD_DIM // 4
    rot = jnp.where(jnp.bitwise_and(lane, 2 * quarter - 1) < quarter, -pltpu.roll(xn, x.shape[1] - quarter, 1), pltpu.roll(xn, quarter, 1))
    return xn * cos + rot * sin


def _attn_kernel(lam_ref, q_ref, k_ref, v_ref, cosq_ref, sinq_ref, cosk_ref, sink_ref, qn_ref, kn_ref, sub_ref, ones_ref,
                 o_ref, kb_ref, vb_ref, *, n_ctx, tq, out_scale):
    e = DF_HEAD_DIM
    tt = k_ref.shape[0]
    qi = pl.program_id(2)
    ones = ones_ref[...]

    @pl.when(qi == 0)
    def _():
        kb_ref[...] = _qk_prep(k_ref[...], kn_ref[...], cosk_ref[...], sink_ref[...], ones).astype(BF16)
        vb_ref[...] = v_ref[...].astype(BF16)

    q = (_qk_prep(q_ref[...], qn_ref[...], cosq_ref[...], sinq_ref[...], ones) * (DF_HEAD_DIM ** -0.5)).astype(BF16)
    k = kb_ref[...]
    v = vb_ref[...]
    key_pos = lax.broadcasted_iota(jnp.int32, (tq, tt), 1)
    kv_len = jnp.where(qi * tq < n_ctx, n_ctx, tt)
    visible = key_pos < kv_len
    outs = []
    for m in range(2):
        s = _dot_nt(q[:, m * e:(m + 1) * e], k[:, m * e:(m + 1) * e])
        s = jnp.where(visible, s, -jnp.inf)
        p = jnp.exp(s - jnp.max(s, axis=-1, keepdims=True))
        l = jnp.sum(p, axis=-1, keepdims=True)
        outs.append(_dot(p.astype(BF16), v) / l)
    o = outs[0] - lam_ref[0] * outs[1]
    o = o * lax.rsqrt(jnp.mean(o * o, axis=-1, keepdims=True) + RMS_EPS) * sub_ref[...] * out_scale
    o_ref[...] = o.astype(o_ref.dtype)


def _diff_attention(z, p, lam, lam_init, cos, sin, n_ctx):
    b, tt, _ = z.shape
    tq = ROW_TILE
    w = DF_V_DIM
    q0 = RW_COLS // w
    k0 = q0 + DF_HEADS
    v0 = k0 + DF_HEADS
    two = lambda g: jnp.tile(g, 2).reshape(1, w)
    full = lambda shape: pl.BlockSpec(shape, lambda bi, hi, i: (0,) * len(shape))
    return pl.pallas_call(
        functools.partial(_attn_kernel, n_ctx=n_ctx, tq=tq, out_scale=1.0 - lam_init),
        grid=(b, DF_HEADS, tt // tq),
        in_specs=[pl.BlockSpec(memory_space=pltpu.SMEM),
                  pl.BlockSpec((None, tq, w), lambda bi, hi, i: (bi, i, q0 + hi)),
                  pl.BlockSpec((None, tt, w), lambda bi, hi, i: (bi, 0, k0 + hi)),
                  pl.BlockSpec((None, tt, w), lambda bi, hi, i: (bi, 0, v0 + hi)),
                  pl.BlockSpec((tq, w), lambda bi, hi, i: (i, 0)),
                  pl.BlockSpec((tq, w), lambda bi, hi, i: (i, 0)),
                  full((tt, w)), full((tt, w)), full((1, w)), full((1, w)), full((1, w)), full((w, w))],
        out_specs=pl.BlockSpec((None, tq, w), lambda bi, hi, i: (bi, i, hi)),
        out_shape=jax.ShapeDtypeStruct((b, tt, DF_WIDTH), BF16),
        scratch_shapes=[pltpu.VMEM((tt, w), BF16), pltpu.VMEM((tt, w), BF16)],
        compiler_params=_cparams(("parallel", "parallel", "arbitrary")),
        name="diff_attention",
    )(lam.reshape(1).astype(F32), z, z, z, cos, sin, cos, sin, two(p["qn"]), two(p["kn"]),
      p["subln"].reshape(1, w), _group_ones(w, DF_HEAD_DIM))


def _mix_out_kernel(yf_ref, yb_ref, bonus_ref, gate_ref, ob_ref, gnw_ref, gnb_ref, seg_ref, w_ref, h_ref, mod_ref, o_ref):
    seg = seg_ref[...]
    y = yf_ref[...] + yb_ref[...]
    mu = _group_sum(y, seg) * (1.0 / RW_HEAD_DIM)
    d = y - mu
    var = _group_sum(d * d, seg) * (1.0 / RW_HEAD_DIM)
    yn = d * lax.rsqrt(var + RW_GN_EPS) * gnw_ref[...] + gnb_ref[...]
    ya = (yn + bonus_ref[...]) * gate_ref[...]
    mix = _dot(ya.astype(BF16), w_ref[:RW_WIDTH, :]) + _dot(ob_ref[...], w_ref[RW_WIDTH:, :])
    o_ref[...] = h_ref[...] + mod_ref[2:3, :] * mix


def _mix_out(yf, yb, bonus, gate, ob, p, h, mods, n_ctx):
    b, tt, d = h.shape
    c = RW_WIDTH
    tm = ROW_TILE
    tok = lambda width: pl.BlockSpec((None, tm, width), lambda bi, i: (bi, i, 0))
    full = lambda shape: pl.BlockSpec(shape, lambda bi, i: (0,) * len(shape))
    return pl.pallas_call(
        _mix_out_kernel,
        grid=(b, tt // tm),
        in_specs=[tok(c), tok(c), tok(c), tok(c), tok(DF_WIDTH), full((1, c)), full((1, c)), full((c, c)),
                  full((c + DF_WIDTH, d)), tok(d),
                  pl.BlockSpec((None, None, 6, d), lambda bi, i: (bi, _kind(i, n_ctx), 0, 0))],
        out_specs=tok(d),
        out_shape=jax.ShapeDtypeStruct((b, tt, d), F32),
        compiler_params=_cparams(("parallel", "parallel")),
        name="mix_out_residual",
    )(yf, yb, bonus, gate, ob, p["gn_w"].reshape(1, c), p["gn_b"].reshape(1, c), _group_ones(c, RW_HEAD_DIM),
      p["w_out"].astype(BF16), h, mods)


def _sgu_kernel(h_ref, mod_ref, g_ref, w_in_ref, b_in_ref, lng_ref, lnb_ref, ws_ref, bst_ref, w_out_ref, o_ref):
    x = h_ref[...]
    tm = x.shape[0]
    y = _norm_mod(x, g_ref[...], mod_ref[1:2, :], mod_ref[0:1, :])
    z = _dot(y.astype(BF16), w_in_ref[...]) + b_in_ref[...]
    z = 0.5 * z * (1.0 + lax.erf(z * (2.0 ** -0.5)))
    u = z[:, :SGU_WIDTH]
    v = z[:, SGU_WIDTH:]
    mu = jnp.mean(v, axis=-1, keepdims=True)
    var = jnp.mean(jnp.square(v - mu), axis=-1, keepdims=True)
    v = ((v - mu) * lax.rsqrt(var + LN_EPS) * lng_ref[...] + lnb_ref[...]).astype(BF16)
    rows = []
    for ci in range(tm // SGU_CHUNK):
        cols = []
        for gi in range(SGU_GROUPS):
            vb = v[ci * SGU_CHUNK:(ci + 1) * SGU_CHUNK, gi * SGU_GROUP_DIM:(gi + 1) * SGU_GROUP_DIM]
            cols.append(_dot(ws_ref[gi], vb) + bst_ref[:, gi:gi + 1])
        rows.append(jnp.concatenate(cols, axis=1))
    sv = jnp.concatenate(rows, axis=0)
    o_ref[...] = x + mod_ref[2:3, :] * _dot((u * sv).astype(BF16), w_out_ref[...])


def _sgu_layer(h, mods, g, p, n_ctx):
    b, tt, d = h.shape
    tm = ROW_TILE
    full = lambda shape: pl.BlockSpec(shape, lambda bi, i: (0,) * len(shape))
    return pl.pallas_call(
        _sgu_kernel,
        grid=(b, tt // tm),
        in_specs=[pl.BlockSpec((None, tm, d), lambda bi, i: (bi, i, 0)),
                  pl.BlockSpec((None, None, 6, d), lambda bi, i: (bi, _kind(i, n_ctx), 0, 0)),
                  full((1, d)), full((d, 2 * SGU_WIDTH)), full((1, 2 * SGU_WIDTH)),
                  full((1, SGU_WIDTH)), full((1, SGU_WIDTH)),
                  full((SGU_GROUPS, SGU_CHUNK, SGU_CHUNK)), full((SGU_CHUNK, SGU_GROUPS)),
                  full((SGU_WIDTH, d))],
        out_specs=pl.BlockSpec((None, tm, d), lambda bi, i: (bi, i, 0)),
        out_shape=jax.ShapeDtypeStruct((b, tt, d), F32),
        compiler_params=_cparams(("parallel", "parallel")),
        name="sgu_layer",
    )(h, mods, g.reshape(1, d), p["w_in"].astype(BF16), p["b_in"].reshape(1, -1),
      p["ln_g"].reshape(1, -1), p["ln_b"].reshape(1, -1), p["ws"].astype(BF16), p["bs"].T,
      p["w_out"].astype(BF16))


def _router_kernel(h_ref, mod_ref, g_ref, rw_ref, rb_ref, f_ref, idx_ref, wt_ref, rank_ref, cnt_ref, base_ref):
    first = jnp.logical_and(pl.program_id(0) == 0, pl.program_id(1) == 0)

    @pl.when(first)
    def _():
        base_ref[...] = jnp.zeros_like(base_ref)

    x = h_ref[...]
    tm = x.shape[0]
    f = _norm_mod(x, g_ref[...], mod_ref[4:5, :], mod_ref[3:4, :])
    f_ref[...] = f.astype(f_ref.dtype)
    logits = _dot(f, rw_ref[...], HIGHEST) + rb_ref[...]
    lane = lax.broadcasted_iota(jnp.int32, (tm, LANES), 1)
    tri = (lax.broadcasted_iota(jnp.int32, (tm, tm), 1) < lax.broadcasted_iota(jnp.int32, (tm, tm), 0))
    vals, sels = [], []
    idx_out = jnp.zeros((tm, LANES), jnp.int32)
    l = logits
    for kk in range(TOP_K):
        m = jnp.max(l, axis=-1, keepdims=True)
        idx = jnp.min(jnp.where(l == m, lane, LANES), axis=-1, keepdims=True)
        sel = lane == idx
        l = jnp.where(sel, -jnp.inf, l)
        vals.append(m)
        sels.append(sel)
        idx_out = jnp.where(lane == kk, idx, idx_out)
    es = [jnp.exp(vv - vals[0]) for vv in vals]
    den = es[0] + es[1] + es[2] + es[3]
    wt_out = jnp.zeros((tm, LANES), F32)
    for kk in range(TOP_K):
        wt_out = jnp.where(lane == kk, es[kk] / den, wt_out)
    onehot = jnp.zeros((tm, LANES), F32)
    for sel in sels:
        onehot = onehot + sel.astype(F32)
    ahead = _dot(tri.astype(BF16), onehot.astype(BF16)) + base_ref[...]
    rank_out = jnp.zeros((tm, LANES), jnp.int32)
    for kk in range(TOP_K):
        rk = jnp.sum(jnp.where(sels[kk], ahead, 0.0), axis=-1, keepdims=True).astype(jnp.int32)
        rank_out = jnp.where(lane == kk, rk, rank_out)
    idx_ref[...] = idx_out
    wt_ref[...] = wt_out
    rank_ref[...] = rank_out
    base_ref[...] = base_ref[...] + jnp.sum(onehot, axis=0, keepdims=True)
    cnt_ref[...] = base_ref[...]


def _router(h, mods, g, rt_w, rt_b, n_ctx, b0, b):
    _, tt, d = h.shape
    tm = ROW_TILE
    rw = jnp.zeros((d, LANES), F32).at[:, :N_EXPERTS].set(rt_w)
    rb = jnp.full((1, LANES), -jnp.inf, F32).at[0, :N_EXPERTS].set(rt_b)
    tok = lambda width: pl.BlockSpec((None, tm, width), lambda bi, i: (bi, i, 0))
    return pl.pallas_call(
        _router_kernel,
        grid=(b, tt // tm),
        in_specs=[pl.BlockSpec((None, tm, d), lambda bi, i: (b0 + bi, i, 0)),
                  pl.BlockSpec((None, None, 6, d), lambda bi, i: (b0 + bi, _kind(i, n_ctx), 0, 0)),
                  pl.BlockSpec((1, d), lambda bi, i: (0, 0)),
                  pl.BlockSpec((d, LANES), lambda bi, i: (0, 0)),
                  pl.BlockSpec((1, LANES), lambda bi, i: (0, 0))],
        out_specs=[tok(d), tok(LANES), tok(LANES), tok(LANES), pl.BlockSpec((1, LANES), lambda bi, i: (0, 0))],
        out_shape=[jax.ShapeDtypeStruct((b, tt, d), BF16),
                   jax.ShapeDtypeStruct((b, tt, LANES), jnp.int32),
                   jax.ShapeDtypeStruct((b, tt, LANES), F32),
                   jax.ShapeDtypeStruct((b, tt, LANES), jnp.int32),
                   jax.ShapeDtypeStruct((1, LANES), F32)],
        scratch_shapes=[pltpu.VMEM((1, LANES), F32)],
        compiler_params=_cparams(("arbitrary", "arbitrary")),
        name="moe_router",
    )(h, mods, g.reshape(1, d), rw, rb)


def _expert_kernel(te_ref, nv_ref, x_ref, w1_ref, b1_ref, w2_ref, b2_ref, o_ref, w1b_ref, w2b_ref):
    j = pl.program_id(0)
    active = j < nv_ref[0]
    new_expert = jnp.logical_or(j == 0, te_ref[j] != te_ref[jnp.maximum(j - 1, 0)])

    @pl.when(jnp.logical_and(active, new_expert))
    def _():
        w1b_ref[...] = w1_ref[...].astype(BF16)
        w2b_ref[...] = w2_ref[...].astype(BF16)

    @pl.when(active)
    def _():
        f = w2b_ref.shape[0]
        fh = f // 2
        x = x_ref[...]
        y = b2_ref[...]
        for c in range(2):
            lo, hi = c * fh, (c + 1) * fh
            glu = _dot(x, w1b_ref[:, lo:hi]) + b1_ref[:, lo:hi]
            lin = _dot(x, w1b_ref[:, f + lo:f + hi]) + b1_ref[:, f + lo:f + hi]
            glu = jnp.minimum(glu, SWIGLU_LIMIT)
            lin = jnp.clip(lin, -SWIGLU_LIMIT, SWIGLU_LIMIT)
            act = glu * jax.nn.sigmoid(SWIGLU_ALPHA * glu) * (lin + 1.0)
            y = y + _dot(act.astype(BF16), w2b_ref[lo:hi, :])
        o_ref[...] = y.astype(o_ref.dtype)

    @pl.when(jnp.logical_not(active))
    def _():
        o_ref[...] = jnp.zeros_like(o_ref)


def _experts(xs, tile_expert, n_valid, w1, b1, w2, b2, layer):
    s, d = xs.shape
    depth, e, _, f2 = w1.shape
    f = f2 // 2
    tm = EXPERT_ROW_TILE
    grid_spec = pltpu.PrefetchScalarGridSpec(
        num_scalar_prefetch=2,
        grid=(s // tm,),
        in_specs=[pl.BlockSpec((tm, d), lambda j, te, nv: (j, 0)),
                  pl.BlockSpec((None, None, d, f2), lambda j, te, nv: (layer, te[j], 0, 0)),
                  pl.BlockSpec((None, None, 1, f2), lambda j, te, nv: (layer, te[j], 0, 0)),
                  pl.BlockSpec((None, None, f, d), lambda j, te, nv: (layer, te[j], 0, 0)),
                  pl.BlockSpec((None, None, 1, d), lambda j, te, nv: (layer, te[j], 0, 0))],
        out_specs=pl.BlockSpec((tm, d), lambda j, te, nv: (j, 0)),
        scratch_shapes=[pltpu.VMEM((d, f2), BF16), pltpu.VMEM((f, d), BF16)],
    )
    return pl.pallas_call(
        _expert_kernel,
        grid_spec=grid_spec,
        out_shape=jax.ShapeDtypeStruct((s, d), BF16),
        compiler_params=_cparams(("arbitrary",)),
        name="moe_experts",
    )(tile_expert, n_valid, xs, w1, b1.reshape(depth, e, 1, f2), w2, b2.reshape(depth, e, 1, d))


def _combine_kernel(p_ref, wt_ref, h_ref, mod_ref, o_ref):
    m = p_ref[0].astype(F32) * wt_ref[:, 0:1]
    for k in range(1, TOP_K):
        m = m + p_ref[k].astype(F32) * wt_ref[:, k:k + 1]
    o_ref[...] = h_ref[...] + mod_ref[5:6, :] * m


def _combine(picked, wt, h, mods, n_ctx, b0, b):
    _, tt, d = h.shape
    tm = ROW_TILE
    return pl.pallas_call(
        _combine_kernel,
        grid=(b, tt // tm),
        in_specs=[pl.BlockSpec((TOP_K, None, tm, d), lambda bi, i: (0, bi, i, 0)),
                  pl.BlockSpec((None, tm, LANES), lambda bi, i: (bi, i, 0)),
                  pl.BlockSpec((None, tm, d), lambda bi, i: (b0 + bi, i, 0)),
                  pl.BlockSpec((None, None, 6, d), lambda bi, i: (b0 + bi, _kind(i, n_ctx), 0, 0))],
        out_specs=pl.BlockSpec((None, tm, d), lambda bi, i: (bi, i, 0)),
        out_shape=jax.ShapeDtypeStruct((b, tt, d), F32),
        compiler_params=_cparams(("parallel", "parallel")),
        name="moe_combine",
    )(picked, wt, h, mods)


def _moe(h, mods, g, rt_w, rt_b, w1, b1, w2, b2, layer, n_ctx, b0, b):
    _, tt, d = h.shape
    n = b * tt
    tm = EXPERT_ROW_TILE
    f, idx, wt, rank, cnt = _router(h, mods, g, rt_w, rt_b, n_ctx, b0, b)
    top_i = idx.reshape(n, LANES)[:, :TOP_K]
    rank = rank.reshape(n, LANES)[:, :TOP_K]
    counts = cnt[0, :N_EXPERTS].astype(jnp.int32)
    padded = ((counts + tm - 1) // tm) * tm
    p_end = jnp.cumsum(padded)
    p_off = p_end - padded
    u_off = jnp.cumsum(counts) - counts
    slot = (p_off[top_i] + rank).reshape(-1)
    n_tiles = (n * TOP_K) // tm + N_EXPERTS
    s_rows = n_tiles * tm
    _, sorted_pair = lax.sort_key_val(slot, jnp.arange(n * TOP_K, dtype=jnp.int32))
    srow = jnp.arange(s_rows, dtype=jnp.int32)
    expert_of = lambda rows: jnp.minimum(jnp.sum((rows[:, None] >= p_end[None, :]).astype(jnp.int32), axis=1),
                                         N_EXPERTS - 1)
    row_e = expert_of(srow)
    within = srow - p_off[row_e]
    valid = within < counts[row_e]
    src = jnp.clip(u_off[row_e] + within, 0, n * TOP_K - 1)
    pair = sorted_pair[src]
    row_token = jnp.where(valid, pair // TOP_K, 0)
    n_valid = (p_end[-1] // tm).astype(jnp.int32).reshape(1)
    tile_start = jnp.arange(n_tiles, dtype=jnp.int32) * tm
    tile_expert = expert_of(tile_start)
    last_e = tile_expert[jnp.maximum(n_valid[0] - 1, 0)]
    tile_expert = jnp.where(tile_start < p_end[-1], tile_expert, last_e)
    xs = jnp.take(f.reshape(n, d), row_token, axis=0)
    ys = _experts(xs, tile_expert, n_valid, w1, b1, w2, b2, layer)
    picked = jnp.take(ys, slot.reshape(n, TOP_K).T.reshape(-1), axis=0).reshape(TOP_K, b, tt, d)
    return _combine(picked, wt, h, mods, n_ctx, b0, b)


def _rope_tables(n_tokens, n_ctx):
    rows = n_tokens // GRID_W
    row = jnp.repeat(jnp.arange(rows, dtype=F32), GRID_W)
    col = jnp.tile(jnp.arange(GRID_W, dtype=F32), rows)
    axis_dim = DF_HEAD_DIM // 2
    inv_freq = ROPE_THETA ** (-jnp.arange(0, axis_dim, 2, dtype=F32) / axis_dim)
    ar = row[:, None] * inv_freq
    ac = col[:, None] * inv_freq
    ang = jnp.concatenate([ar, ar, ac, ac], axis=-1)
    cos = jnp.concatenate([jnp.ones((n_ctx, DF_HEAD_DIM), F32), jnp.cos(ang)], axis=0)
    sin = jnp.concatenate([jnp.zeros((n_ctx, DF_HEAD_DIM), F32), jnp.sin(ang)], axis=0)
    return jnp.tile(cos, (1, 2)), jnp.tile(sin, (1, 2))


def _even_layer(h, mods, g1, p, li, n_ctx, rope):
    z = _nm_mm(h, mods, g1, p["w_in"].astype(BF16), n_ctx)
    r, v, kk, kd0, kd1, lw0, lw1, bb0, bb1, gate, bonus = _rw_features(z, p, n_ctx)
    yf, yb = _rw_scan(r, v, kk, kd0, lw0, bb0, kd1, lw1, bb1, n_ctx)
    lam_init = 0.8 - 0.6 * math.exp(-0.3 * li)
    lp = p["lam"].astype(F32)
    lam = jnp.exp(jnp.sum(lp[0] * lp[1])) - jnp.exp(jnp.sum(lp[2] * lp[3])) + lam_init
    ob = _diff_attention(z, p, lam, lam_init, rope[0], rope[1], n_ctx)
    return _mix_out(yf, yb, bonus, gate, ob, p, h, mods, n_ctx)


def kernel(x, c, ctx, c_ctx, norm1_g, norm2_g, ada_w, ada_b, ev_w_in, ev_w_out, rw_mu, rw_w0, rw_w2, rw_a0, rw_a2, rw_g2, rw_kk, rw_ka, rw_rk, rw_gn_w, rw_gn_b, df_qn, df_kn, df_lam, df_subln, sg_w_in, sg_b_in, sg_ln_g, sg_ln_b, sg_ws, sg_bs, sg_w_out, rt_w, rt_b, ex_w1, ex_b1, ex_w2, ex_b2):
    b, t, d = x.shape
    n_ctx = ctx.shape[1]
    depth = ada_w.shape[0]
    assert d == D_MODEL and t % ROW_TILE == 0 and n_ctx % ROW_TILE == 0 and t % GRID_W == 0
    rope = _rope_tables(t, n_ctx)
    r_pad = -(-(b + 1) // 8) * 8
    cs = jnp.zeros((r_pad, d), F32).at[:b].set(c).at[b].set(c_ctx)
    ada = _ada_table(cs, ada_w, ada_b).reshape(depth, r_pad, 6, d)
    h = jnp.concatenate([ctx, x], axis=1)
    for l in range(depth):
        ctx_out = any(m % 2 == 0 for m in range(l + 1, depth))
        j = l // 2
        mods = jnp.stack([jnp.broadcast_to(ada[l, b], (b, 6, d)), ada[l, :b]], axis=1)
        if l % 2 == 0:
            p = {"w_in": ev_w_in[j], "w_out": ev_w_out[j], "mu": rw_mu[j], "w0": rw_w0[j], "w2": rw_w2[j],
                 "a0": rw_a0[j], "a2": rw_a2[j], "g2": rw_g2[j], "k_k": rw_kk[j], "k_a": rw_ka[j],
                 "r_k": rw_rk[j], "gn_w": rw_gn_w[j], "gn_b": rw_gn_b[j], "qn": df_qn[j], "kn": df_kn[j],
                 "lam": df_lam[j], "subln": df_subln[j]}
            h = _even_layer(h, mods, norm1_g[l], p, l, n_ctx, rope)
        else:
            p = {"w_in": sg_w_in[j], "b_in": sg_b_in[j], "ln_g": sg_ln_g[j], "ln_b": sg_ln_b[j],
                 "ws": sg_ws[j], "bs": sg_bs[j], "w_out": sg_w_out[j]}
            h = _sgu_layer(h, mods, norm1_g[l], p, n_ctx)
        if n_ctx > 0 and not ctx_out:
            h = h[:, n_ctx:]
            n_ctx = 0
        h = _moe(h, mods, norm2_g[l], rt_w[l], rt_b[l], ex_w1, ex_b1, ex_w2, ex_b2, l, n_ctx, 0, b)
    return h[:, n_ctx:]
```

```python
import functools
import math

import jax
import jax.numpy as jnp
from jax import lax
from jax.experimental import pallas as pl
from jax.experimental.pallas import tpu as pltpu

F32 = jnp.float32
BF16 = jnp.bfloat16
HIGHEST = lax.Precision.HIGHEST

D_MODEL = 1024
GRID_W = 64
RMS_EPS = 1e-6
LN_EPS = 1e-5

RW_HEAD_DIM = 64
RW_WIDTH = 512
RW_HEADS = RW_WIDTH // RW_HEAD_DIM
DECAY_LORA = 64
ICLR_LORA = 64
GATE_LORA = 128
RW_COLS = 3 * RW_WIDTH + 2 * DECAY_LORA + 2 * ICLR_LORA + GATE_LORA
RW_GN_EPS = 64e-5

DF_HEAD_DIM = 64
DF_V_DIM = 128
DF_WIDTH = 512
DF_HEADS = DF_WIDTH // DF_V_DIM
DF_QK = DF_HEADS * 2 * DF_HEAD_DIM
ROPE_THETA = 10000.0
EVEN_COLS = RW_COLS + 2 * DF_QK + DF_WIDTH

SGU_CHUNK = 128
SGU_WIDTH = 1024
SGU_GROUPS = 8
SGU_GROUP_DIM = SGU_WIDTH // SGU_GROUPS

N_EXPERTS = 32
TOP_K = 4
SWIGLU_LIMIT = 7.0
SWIGLU_ALPHA = 1.702

LANES = 128
ROW_TILE = 256
SCAN_CHUNK = 64
EXPERT_ROW_TILE = 512
VMEM_LIMIT = 56 * 1024 * 1024


def _cparams(sem):
    return pltpu.CompilerParams(dimension_semantics=sem, vmem_limit_bytes=VMEM_LIMIT)


def _dot(a, b, precision=None):
    return jnp.dot(a, b, preferred_element_type=F32, precision=precision)


def _dot_nt(a, b, precision=None):
    return lax.dot_general(a, b, (((1,), (1,)), ((), ())), preferred_element_type=F32, precision=precision)


def _dot_tn(a, b, precision=None):
    return lax.dot_general(a, b, (((0,), (0,)), ((), ())), preferred_element_type=F32, precision=precision)


def _norm_mod(x, g, scale, shift):
    ms = jnp.mean(x * x, axis=-1, keepdims=True)
    return (x * lax.rsqrt(ms + RMS_EPS) * g) * (1.0 + scale) + shift


def _ada_kernel(x_ref, w_ref, b_ref, o_ref):
    x = x_ref[...]
    s = x * jax.nn.sigmoid(x)
    o_ref[...] = _dot(s, w_ref[...], HIGHEST) + b_ref[...]


def _ada_table(cs, ada_w, ada_b):
    depth, d, n = ada_w.shape
    r = cs.shape[0]
    tn = 1024
    return pl.pallas_call(
        _ada_kernel,
        grid=(depth, n // tn),
        in_specs=[pl.BlockSpec((r, d), lambda l, j: (0, 0)),
                  pl.BlockSpec((None, d, tn), lambda l, j: (l, 0, j)),
                  pl.BlockSpec((None, 1, tn), lambda l, j: (l, 0, j))],
        out_specs=pl.BlockSpec((None, r, tn), lambda l, j: (l, 0, j)),
        out_shape=jax.ShapeDtypeStruct((depth, r, n), F32),
        compiler_params=_cparams(("parallel", "parallel")),
        name="ada_table",
    )(cs, ada_w, ada_b.reshape(depth, 1, n))


def _kind(i, n_ctx):
    return (i * ROW_TILE >= n_ctx).astype(jnp.int32) if n_ctx > 0 else 1


def _nm_mm_kernel(h_ref, mod_ref, g_ref, w_ref, o_ref):
    y = _norm_mod(h_ref[...], g_ref[...], mod_ref[1:2, :], mod_ref[0:1, :])
    o_ref[...] = _dot(y.astype(BF16), w_ref[...]).astype(o_ref.dtype)


def _nm_mm(h, mods, g, w, n_ctx, out_dtype=F32):
    b, tt, d = h.shape
    n = w.shape[1]
    tm = ROW_TILE
    return pl.pallas_call(
        _nm_mm_kernel,
        grid=(b, tt // tm),
        in_specs=[pl.BlockSpec((None, tm, d), lambda bi, i: (bi, i, 0)),
                  pl.BlockSpec((None, None, 6, d), lambda bi, i: (bi, _kind(i, n_ctx), 0, 0)),
                  pl.BlockSpec((1, d), lambda bi, i: (0, 0)),
                  pl.BlockSpec((d, n), lambda bi, i: (0, 0))],
        out_specs=pl.BlockSpec((None, tm, n), lambda bi, i: (bi, i, 0)),
        out_shape=jax.ShapeDtypeStruct((b, tt, n), out_dtype),
        compiler_params=_cparams(("parallel", "parallel")),
        name="norm_mod_proj",
    )(h, mods, g.reshape(1, d), w)


def _group_ones(width, group):
    idx = jnp.arange(width) // group
    return (idx[:, None] == idx[None, :]).astype(BF16)


def _group_sum(x, ones_bf16):
    hi = x.astype(BF16)
    lo = (x - hi.astype(F32)).astype(BF16)
    return _dot(hi, ones_bf16) + _dot(lo, ones_bf16)


def _softplus(x):
    return jnp.maximum(x, 0.0) + jnp.log1p(jnp.exp(-jnp.abs(x)))


def _rw_feat_kernel(z_ref, zprev_ref, znext_ref, mu_ref, w0_ref, w2_ref, a0_ref, a2_ref, g2_ref, kk_ref, ka_ref,
                    rk_ref, seg_ref,
                    r_ref, v_ref, kko_ref, kd0_ref, kd1_ref, lw0_ref, lw1_ref, bb0_ref, bb1_ref, g_ref, bonus_ref,
                    *, n_ctx, tt):
    c = RW_WIDTH
    tm = z_ref.shape[0]
    i = pl.program_id(1)
    z = z_ref[...]
    seq_first = jnp.logical_or(i * tm == 0, i * tm == n_ctx)
    seq_last = jnp.logical_or((i + 1) * tm == n_ctx, (i + 1) * tm == tt)
    prev_row = jnp.where(seq_first, 0.0, zprev_ref[7:8, :])
    next_row = jnp.where(seq_last, 0.0, znext_ref[0:1, :])
    rows = lax.broadcasted_iota(jnp.int32, z.shape, 0)
    zp = jnp.where(rows == 0, prev_row, pltpu.roll(z, 1, 0))
    zn = jnp.where(rows == tm - 1, next_row, pltpu.roll(z, tm - 1, 0))
    za = z + mu_ref[0:1, :] * (zp - z) + mu_ref[1:2, :] * (zn - z)
    r = za[:, :c]
    k = za[:, c:2 * c]
    v = za[:, 2 * c:3 * c]
    wl = za[:, 3 * c:3 * c + 2 * DECAY_LORA]
    al = za[:, 3 * c + 2 * DECAY_LORA:3 * c + 2 * DECAY_LORA + 2 * ICLR_LORA]
    gl = za[:, 3 * c + 2 * DECAY_LORA + 2 * ICLR_LORA:]
    r_ref[...] = r.astype(r_ref.dtype)
    v_ref[...] = v.astype(v_ref.dtype)
    twl = jnp.tanh(wl)
    seg = seg_ref[...]
    kkr = k * kk_ref[...]
    kk = kkr / jnp.maximum(jnp.sqrt(_group_sum(kkr * kkr, seg)), 1e-12)
    kko_ref[...] = kk.astype(kko_ref.dtype)
    lw_refs = (lw0_ref, lw1_ref)
    bb_refs = (bb0_ref, bb1_ref)
    kd_refs = (kd0_ref, kd1_ref)
    kd_sum = None
    for d in range(2):
        wd = w0_ref[d:d + 1, :] + _dot(twl[:, d * DECAY_LORA:(d + 1) * DECAY_LORA], w2_ref[d], HIGHEST)
        w = -_softplus(-wd) - 0.5
        lw_refs[d][...] = -jnp.exp(w)
        a = jax.nn.sigmoid(a0_ref[d:d + 1, :] + _dot(al[:, d * ICLR_LORA:(d + 1) * ICLR_LORA], a2_ref[d], HIGHEST))
        bb_refs[d][...] = (kk * a).astype(bb_refs[d].dtype)
        kd = k * (1.0 + (a - 1.0) * ka_ref[...])
        kd_refs[d][...] = kd.astype(kd_refs[d].dtype)
        kd_sum = kd if kd_sum is None else kd_sum + kd
    g_ref[...] = _dot(jax.nn.sigmoid(gl), g2_ref[...], HIGHEST)
    bonus_ref[...] = _group_sum(r * kd_sum * rk_ref[...], seg) * v


def _rw_features(z, p, n_ctx):
    b, tt, _ = z.shape
    c = RW_WIDTH
    tm = ROW_TILE
    hb = tm // 8
    last_halo = tt // 8 - 1
    full = lambda shape: pl.BlockSpec(shape, lambda bi, i: (0,) * len(shape))
    out_spec = pl.BlockSpec((None, tm, c), lambda bi, i: (bi, i, 0))
    return pl.pallas_call(
        functools.partial(_rw_feat_kernel, n_ctx=n_ctx, tt=tt),
        grid=(b, tt // tm),
        in_specs=[pl.BlockSpec((None, tm, RW_COLS), lambda bi, i: (bi, i, 0)),
                  pl.BlockSpec((None, 8, RW_COLS), lambda bi, i: (bi, jnp.maximum(i * hb - 1, 0), 0)),
                  pl.BlockSpec((None, 8, RW_COLS), lambda bi, i: (bi, jnp.minimum((i + 1) * hb, last_halo), 0)),
                  full((2, RW_COLS)), full((2, c)), full((2, DECAY_LORA, c)), full((2, c)), full((2, ICLR_LORA, c)),
                  full((GATE_LORA, c)), full((1, c)), full((1, c)), full((1, c)), full((c, c))],
        out_specs=[out_spec] * 11,
        out_shape=[jax.ShapeDtypeStruct((b, tt, c), dt) for dt in (BF16,) * 5 + (F32, F32, BF16, BF16, F32, F32)],
        compiler_params=_cparams(("parallel", "parallel")),
        name="rwkv_features",
    )(z, z, z, p["mu"], p["w0"], p["w2"], p["a0"], p["a2"], p["g2"], p["k_k"].reshape(1, c), p["k_a"].reshape(1, c),
      p["r_k"].reshape(1, c), _group_ones(c, RW_HEAD_DIM))


def _cumsum_rows(tri_bf16, x):
    hi = x.astype(BF16)
    r1 = x - hi.astype(F32)
    mid = r1.astype(BF16)
    lo = (r1 - mid.astype(F32)).astype(BF16)
    return _dot(tri_bf16, hi) + _dot(tri_bf16, mid) + _dot(tri_bf16, lo)


def _scan_masks(reverse):
    L = SCAN_CHUNK
    row = lax.broadcasted_iota(jnp.int32, (L, L), 0)
    col = lax.broadcasted_iota(jnp.int32, (L, L), 1)
    incl = (col >= row) if reverse else (col <= row)
    strict = (col > row) if reverse else (col < row)
    levels = []
    for lg in range(int(math.log2(L))):
        same_pair = jnp.right_shift(row, lg + 1) == jnp.right_shift(col, lg + 1)
        other_half = jnp.right_shift(row, lg) != jnp.right_shift(col, lg)
        levels.append(jnp.logical_and(jnp.logical_and(same_pair, other_half), strict))
    return incl, strict, row == col, levels


def _scan_operands(r_ref, v_ref, kk_ref, kd_ref, lw_ref, bb_ref, incl, reverse):
    L = SCAN_CHUNK
    lw = lw_ref[...]
    cum = _cumsum_rows(incl.astype(BF16), lw)
    ctot = cum[0:1, :] if reverse else cum[L - 1:L, :]
    g_inv = jnp.exp(-cum)
    g_end = jnp.exp(ctot - cum)
    kd = kd_ref[...].astype(F32)
    bb = bb_ref[...].astype(F32)
    r_f = r_ref[...].astype(F32) * jnp.exp(cum)
    return dict(a=(-kk_ref[...].astype(F32) * jnp.exp(cum - lw)).astype(BF16), b=(bb * g_inv).astype(BF16),
                k=(kd * g_inv).astype(BF16), r_f=r_f, r=r_f.astype(BF16), b_e=(bb * g_end).astype(BF16),
                k_e=(kd * g_end).astype(BF16), v=v_ref[...].astype(BF16), g_tot=jnp.exp(ctot))


def _scan_kernel(rf_ref, vf_ref, kkf_ref, kdf_ref, lwf_ref, bbf_ref, rb_ref, vb_ref, kkb_ref, kdb_ref, lwb_ref, bbb_ref,
                 yf_ref, yb_ref, s_ref):
    L = SCAN_CHUNK
    n = RW_HEAD_DIM

    @pl.when(pl.program_id(1) == 0)
    def _():
        s_ref[...] = jnp.zeros_like(s_ref)

    masks = (_scan_masks(False), _scan_masks(True))
    ops = (_scan_operands(rf_ref, vf_ref, kkf_ref, kdf_ref, lwf_ref, bbf_ref, masks[0][0], False),
           _scan_operands(rb_ref, vb_ref, kkb_ref, kdb_ref, lwb_ref, bbb_ref, masks[1][0], True))
    y_refs = (yf_ref, yb_ref)
    chains = [(d, h) for h in range(RW_HEADS) for d in range(2)]
    cut = lambda d, h, name: ops[d][name][:, h * n:(h + 1) * n]

    gram = [_dot_nt(jnp.concatenate([cut(d, h, "a"), cut(d, h, "r")], axis=0),
                    jnp.concatenate([cut(d, h, "b"), cut(d, h, "k")], axis=0)) for d, h in chains]
    n_ab = [g[:L, :L] for g in gram]
    lower = [jnp.concatenate([jnp.where(masks[d][1], g[:L, L:], 0.0), jnp.where(masks[d][0], g[L:, L:], 0.0)],
                             axis=0).astype(BF16) for (d, h), g in zip(chains, gram)]
    m_rb = [jnp.where(masks[d][0], g[L:, :L], 0.0).astype(BF16) for (d, h), g in zip(chains, gram)]
    nv = [_dot(lo, cut(d, h, "v")) for (d, h), lo in zip(chains, lower)]
    t_inv = [jnp.where(masks[d][2], 1.0, jnp.where(masks[d][3][0], nab, 0.0)) for (d, h), nab in zip(chains, n_ab)]
    for lv in range(1, len(masks[0][3])):
        tb = [t.astype(BF16) for t in t_inv]
        half = [_dot(t, jnp.where(masks[d][3][lv], nab, 0.0).astype(BF16)).astype(BF16)
                for (d, h), t, nab in zip(chains, tb, n_ab)]
        t_inv = [t + _dot(hf, t16) for t, hf, t16 in zip(t_inv, half, tb)]
    au = [_dot(t.astype(BF16), jnp.concatenate([cut(d, h, "a"), x[:L].astype(BF16)], axis=1)).astype(BF16)
          for (d, h), t, x in zip(chains, t_inv, nv)]
    ry = [_dot(m, x) for m, x in zip(m_rb, au)]
    pq = [_dot_tn(x, cut(d, h, "b_e")) for (d, h), x in zip(chains, au)]
    vk = [_dot_tn(cut(d, h, "v"), cut(d, h, "k_e")) for d, h in chains]
    for i, (d, h) in enumerate(chains):
        sl = slice(h * n, (h + 1) * n)
        r_hat = (ops[d]["r_f"][:, sl] + ry[i][:, :n]).astype(BF16)
        y_hat = ry[i][:, n:] + nv[i][L:]
        p_mat = jnp.where(masks[d][2], ops[d]["g_tot"][:, sl], 0.0) + pq[i][:n]
        q_mat = pq[i][n:] + vk[i]
        s0b = s_ref[d, h].astype(BF16)
        y_refs[d][:, sl] = _dot_nt(r_hat, s0b) + y_hat
        s_ref[d, h] = _dot(s0b, p_mat.astype(BF16)) + q_mat


def _rw_scan(r, v, kk, kd0, lw0, bb0, kd1, lw1, bb1, n_ctx):
    b, tt, c = r.shape
    L = SCAN_CHUNK
    nc = tt // L
    ncc = n_ctx // L
    fwd = pl.BlockSpec((None, L, c), lambda bi, i: (bi, i, 0))
    bwd = pl.BlockSpec((None, L, c), lambda bi, i: (bi, jnp.where(i < ncc, ncc - 1 - i, nc - 1 - (i - ncc)), 0))
    return pl.pallas_call(
        _scan_kernel,
        grid=(b, nc),
        in_specs=[fwd] * 6 + [bwd] * 6,
        out_specs=[fwd, bwd],
        out_shape=[jax.ShapeDtypeStruct((b, tt, c), F32)] * 2,
        scratch_shapes=[pltpu.VMEM((2, RW_HEADS, RW_HEAD_DIM, RW_HEAD_DIM), F32)],
        compiler_params=_cparams(("parallel", "arbitrary")),
        name="rwkv_scan",
    )(r, v, kk, kd0, lw0, bb0, r, v, kk, kd1, lw1, bb1)


def _qk_prep(x, gain, cos, sin, ones):
    xn = x * lax.rsqrt(_group_sum(x * x, ones) * (1.0 / DF_HEAD_DIM) + RMS_EPS) * gain
    lane = lax.broadcasted_iota(jnp.int32, x.shape, 1)
    quarter = DF_HEAD_DIM // 4
    rot = jnp.where(jnp.bitwise_and(lane, 2 * quarter - 1) < quarter, -pltpu.roll(xn, x.shape[1] - quarter, 1), pltpu.roll(xn, quarter, 1))
    return xn * cos + rot * sin


def _attn_kernel(lam_ref, q_ref, k_ref, v_ref, cosq_ref, sinq_ref, cosk_ref, sink_ref, qn_ref, kn_ref, sub_ref, ones_ref,
                 o_ref, kb_ref, vb_ref, *, n_ctx, tq, out_scale):
    e = DF_HEAD_DIM
    tt = k_ref.shape[0]
    qi = pl.program_id(2)
    ones = ones_ref[...]

    @pl.when(qi == 0)
    def _():
        kb_ref[...] = _qk_prep(k_ref[...], kn_ref[...], cosk_ref[...], sink_ref[...], ones).astype(BF16)
        vb_ref[...] = v_ref[...].astype(BF16)

    q = (_qk_prep(q_ref[...], qn_ref[...], cosq_ref[...], sinq_ref[...], ones) * (DF_HEAD_DIM ** -0.5)).astype(BF16)
    k = kb_ref[...]
    v = vb_ref[...]
    key_pos = lax.broadcasted_iota(jnp.int32, (tq, tt), 1)
    kv_len = jnp.where(qi * tq < n_ctx, n_ctx, tt)
    visible = key_pos < kv_len
    outs = []
    for m in range(2):
        s = _dot_nt(q[:, m * e:(m + 1) * e], k[:, m * e:(m + 1) * e])
        s = jnp.where(visible, s, -jnp.inf)
        p = jnp.exp(s - jnp.max(s, axis=-1, keepdims=True))
        l = jnp.sum(p, axis=-1, keepdims=True)
        outs.append(_dot(p.astype(BF16), v) / l)
    o = outs[0] - lam_ref[0] * outs[1]
    o = o * lax.rsqrt(jnp.mean(o * o, axis=-1, keepdims=True) + RMS_EPS) * sub_ref[...] * out_scale
    o_ref[...] = o.astype(o_ref.dtype)


def _diff_attention(z, p, lam, lam_init, cos, sin, n_ctx):
    b, tt, _ = z.shape
    tq = ROW_TILE
    w = DF_V_DIM
    q0 = RW_COLS // w
    k0 = q0 + DF_HEADS
    v0 = k0 + DF_HEADS
    two = lambda g: jnp.tile(g, 2).reshape(1, w)
    full = lambda shape: pl.BlockSpec(shape, lambda bi, hi, i: (0,) * len(shape))
    return pl.pallas_call(
        functools.partial(_attn_kernel, n_ctx=n_ctx, tq=tq, out_scale=1.0 - lam_init),
        grid=(b, DF_HEADS, tt // tq),
        in_specs=[pl.BlockSpec(memory_space=pltpu.SMEM),
                  pl.BlockSpec((None, tq, w), lambda bi, hi, i: (bi, i, q0 + hi)),
                  pl.BlockSpec((None, tt, w), lambda bi, hi, i: (bi, 0, k0 + hi)),
                  pl.BlockSpec((None, tt, w), lambda bi, hi, i: (bi, 0, v0 + hi)),
                  pl.BlockSpec((tq, w), lambda bi, hi, i: (i, 0)),
                  pl.BlockSpec((tq, w), lambda bi, hi, i: (i, 0)),
                  full((tt, w)), full((tt, w)), full((1, w)), full((1, w)), full((1, w)), full((w, w))],
        out_specs=pl.BlockSpec((None, tq, w), lambda bi, hi, i: (bi, i, hi)),
        out_shape=jax.ShapeDtypeStruct((b, tt, DF_WIDTH), BF16),
        scratch_shapes=[pltpu.VMEM((tt, w), BF16), pltpu.VMEM((tt, w), BF16)],
        compiler_params=_cparams(("parallel", "parallel", "arbitrary")),
        name="diff_attention",
    )(lam.reshape(1).astype(F32), z, z, z, cos, sin, cos, sin, two(p["qn"]), two(p["kn"]),
      p["subln"].reshape(1, w), _group_ones(w, DF_HEAD_DIM))


def _mix_out_kernel(yf_ref, yb_ref, bonus_ref, gate_ref, ob_ref, gnw_ref, gnb_ref, seg_ref, w_ref, h_ref, mod_ref, o_ref):
    seg = seg_ref[...]
    y = yf_ref[...] + yb_ref[...]
    mu = _group_sum(y, seg) * (1.0 / RW_HEAD_DIM)
    d = y - mu
    var = _group_sum(d * d, seg) * (1.0 / RW_HEAD_DIM)
    yn = d * lax.rsqrt(var + RW_GN_EPS) * gnw_ref[...] + gnb_ref[...]
    ya = (yn + bonus_ref[...]) * gate_ref[...]
    mix = _dot(ya.astype(BF16), w_ref[:RW_WIDTH, :]) + _dot(ob_ref[...], w_ref[RW_WIDTH:, :])
    o_ref[...] = h_ref[...] + mod_ref[2:3, :] * mix


def _mix_out(yf, yb, bonus, gate, ob, p, h, mods, n_ctx):
    b, tt, d = h.shape
    c = RW_WIDTH
    tm = ROW_TILE
    tok = lambda width: pl.BlockSpec((None, tm, width), lambda bi, i: (bi, i, 0))
    full = lambda shape: pl.BlockSpec(shape, lambda bi, i: (0,) * len(shape))
    return pl.pallas_call(
        _mix_out_kernel,
        grid=(b, tt // tm),
        in_specs=[tok(c), tok(c), tok(c), tok(c), tok(DF_WIDTH), full((1, c)), full((1, c)), full((c, c)),
                  full((c + DF_WIDTH, d)), tok(d),
                  pl.BlockSpec((None, None, 6, d), lambda bi, i: (bi, _kind(i, n_ctx), 0, 0))],
        out_specs=tok(d),
        out_shape=jax.ShapeDtypeStruct((b, tt, d), F32),
        compiler_params=_cparams(("parallel", "parallel")),
        name="mix_out_residual",
    )(yf, yb, bonus, gate, ob, p["gn_w"].reshape(1, c), p["gn_b"].reshape(1, c), _group_ones(c, RW_HEAD_DIM),
      p["w_out"].astype(BF16), h, mods)


def _sgu_kernel(h_ref, mod_ref, g_ref, w_in_ref, b_in_ref, lng_ref, lnb_ref, ws_ref, bst_ref, w_out_ref, o_ref):
    x = h_ref[...]
    tm = x.shape[0]
    y = _norm_mod(x, g_ref[...], mod_ref[1:2, :], mod_ref[0:1, :])
    z = _dot(y.astype(BF16), w_in_ref[...]) + b_in_ref[...]
    z = 0.5 * z * (1.0 + lax.erf(z * (2.0 ** -0.5)))
    u = z[:, :SGU_WIDTH]
    v = z[:, SGU_WIDTH:]
    mu = jnp.mean(v, axis=-1, keepdims=True)
    var = jnp.mean(jnp.square(v - mu), axis=-1, keepdims=True)
    v = ((v - mu) * lax.rsqrt(var + LN_EPS) * lng_ref[...] + lnb_ref[...]).astype(BF16)
    rows = []
    for ci in range(tm // SGU_CHUNK):
        cols = []
        for gi in range(SGU_GROUPS):
            vb = v[ci * SGU_CHUNK:(ci + 1) * SGU_CHUNK, gi * SGU_GROUP_DIM:(gi + 1) * SGU_GROUP_DIM]
            cols.append(_dot(ws_ref[gi], vb) + bst_ref[:, gi:gi + 1])
        rows.append(jnp.concatenate(cols, axis=1))
    sv = jnp.concatenate(rows, axis=0)
    o_ref[...] = x + mod_ref[2:3, :] * _dot((u * sv).astype(BF16), w_out_ref[...])


def _sgu_layer(h, mods, g, p, n_ctx):
    b, tt, d = h.shape
    tm = ROW_TILE
    full = lambda shape: pl.BlockSpec(shape, lambda bi, i: (0,) * len(shape))
    return pl.pallas_call(
        _sgu_kernel,
        grid=(b, tt // tm),
        in_specs=[pl.BlockSpec((None, tm, d), lambda bi, i: (bi, i, 0)),
                  pl.BlockSpec((None, None, 6, d), lambda bi, i: (bi, _kind(i, n_ctx), 0, 0)),
                  full((1, d)), full((d, 2 * SGU_WIDTH)), full((1, 2 * SGU_WIDTH)),
                  full((1, SGU_WIDTH)), full((1, SGU_WIDTH)),
                  full((SGU_GROUPS, SGU_CHUNK, SGU_CHUNK)), full((SGU_CHUNK, SGU_GROUPS)),
                  full((SGU_WIDTH, d))],
        out_specs=pl.BlockSpec((None, tm, d), lambda bi, i: (bi, i, 0)),
        out_shape=jax.ShapeDtypeStruct((b, tt, d), F32),
        compiler_params=_cparams(("parallel", "parallel")),
        name="sgu_layer",
    )(h, mods, g.reshape(1, d), p["w_in"].astype(BF16), p["b_in"].reshape(1, -1),
      p["ln_g"].reshape(1, -1), p["ln_b"].reshape(1, -1), p["ws"].astype(BF16), p["bs"].T,
      p["w_out"].astype(BF16))


def _router_kernel(h_ref, mod_ref, g_ref, rwh_ref, rwl_ref, rb_ref, f_ref, idx_ref, wt_ref, rank_ref, cnt_ref, base_ref):
    first = jnp.logical_and(pl.program_id(0) == 0, pl.program_id(1) == 0)

    @pl.when(first)
    def _():
        base_ref[...] = jnp.zeros_like(base_ref)

    x = h_ref[...]
    tm = x.shape[0]
    f = _norm_mod(x, g_ref[...], mod_ref[4:5, :], mod_ref[3:4, :])
    fh = f.astype(BF16)
    fl = (f - fh.astype(F32)).astype(BF16)
    f_ref[...] = fh
    logits = _dot(fh, rwh_ref[...]) + _dot(fl, rwh_ref[...]) + _dot(fh, rwl_ref[...]) + rb_ref[...]
    lane = lax.broadcasted_iota(jnp.int32, (tm, LANES), 1)
    tri = (lax.broadcasted_iota(jnp.int32, (tm, tm), 1) < lax.broadcasted_iota(jnp.int32, (tm, tm), 0))
    vals, sels = [], []
    idx_out = jnp.zeros((tm, LANES), jnp.int32)
    l = logits
    for kk in range(TOP_K):
        m = jnp.max(l, axis=-1, keepdims=True)
        idx = jnp.min(jnp.where(l == m, lane, LANES), axis=-1, keepdims=True)
        sel = lane == idx
        l = jnp.where(sel, -jnp.inf, l)
        vals.append(m)
        sels.append(sel)
        idx_out = jnp.where(lane == kk, idx, idx_out)
    es = [jnp.exp(vv - vals[0]) for vv in vals]
    den = es[0] + es[1] + es[2] + es[3]
    wt_out = jnp.zeros((tm, LANES), F32)
    for kk in range(TOP_K):
        wt_out = jnp.where(lane == kk, es[kk] / den, wt_out)
    onehot = jnp.zeros((tm, LANES), F32)
    for sel in sels:
        onehot = onehot + sel.astype(F32)
    ahead = _dot(tri.astype(BF16), onehot.astype(BF16)) + base_ref[...]
    rank_out = jnp.zeros((tm, LANES), jnp.int32)
    for kk in range(TOP_K):
        rk = jnp.sum(jnp.where(sels[kk], ahead, 0.0), axis=-1, keepdims=True).astype(jnp.int32)
        rank_out = jnp.where(lane == kk, rk, rank_out)
    idx_ref[...] = idx_out
    wt_ref[...] = wt_out
    rank_ref[...] = rank_out
    base_ref[...] = base_ref[...] + jnp.sum(onehot, axis=0, keepdims=True)
    cnt_ref[...] = base_ref[...]


def _router(h, mods, g, rt_w, rt_b, n_ctx, b0, b):
    _, tt, d = h.shape
    tm = ROW_TILE
    rw = jnp.zeros((d, LANES), F32).at[:, :N_EXPERTS].set(rt_w)
    rw_hi = rw.astype(BF16)
    rw_lo = (rw - rw_hi.astype(F32)).astype(BF16)
    rb = jnp.full((1, LANES), -jnp.inf, F32).at[0, :N_EXPERTS].set(rt_b)
    tok = lambda width: pl.BlockSpec((None, tm, width), lambda bi, i: (bi, i, 0))
    return pl.pallas_call(
        _router_kernel,
        grid=(b, tt // tm),
        in_specs=[pl.BlockSpec((None, tm, d), lambda bi, i: (b0 + bi, i, 0)),
                  pl.BlockSpec((None, None, 6, d), lambda bi, i: (b0 + bi, _kind(i, n_ctx), 0, 0)),
                  pl.BlockSpec((1, d), lambda bi, i: (0, 0)),
                  pl.BlockSpec((d, LANES), lambda bi, i: (0, 0)),
                  pl.BlockSpec((d, LANES), lambda bi, i: (0, 0)),
                  pl.BlockSpec((1, LANES), lambda bi, i: (0, 0))],
        out_specs=[tok(d), tok(LANES), tok(LANES), tok(LANES), pl.BlockSpec((1, LANES), lambda bi, i: (0, 0))],
        out_shape=[jax.ShapeDtypeStruct((b, tt, d), BF16),
                   jax.ShapeDtypeStruct((b, tt, LANES), jnp.int32),
                   jax.ShapeDtypeStruct((b, tt, LANES), F32),
                   jax.ShapeDtypeStruct((b, tt, LANES), jnp.int32),
                   jax.ShapeDtypeStruct((1, LANES), F32)],
        scratch_shapes=[pltpu.VMEM((1, LANES), F32)],
        compiler_params=_cparams(("arbitrary", "arbitrary")),
        name="moe_router",
    )(h, mods, g.reshape(1, d), rw_hi, rw_lo, rb)


def _expert_kernel(te_ref, nv_ref, x_ref, w1_ref, b1_ref, w2_ref, b2_ref, o_ref, w1b_ref, w2b_ref):
    j = pl.program_id(0)
    active = j < nv_ref[0]
    new_expert = jnp.logical_or(j == 0, te_ref[j] != te_ref[jnp.maximum(j - 1, 0)])

    @pl.when(jnp.logical_and(active, new_expert))
    def _():
        w1b_ref[...] = w1_ref[...].astype(BF16)
        w2b_ref[...] = w2_ref[...].astype(BF16)

    @pl.when(active)
    def _():
        f = w2b_ref.shape[0]
        fh = f // 2
        x = x_ref[...]
        y = b2_ref[...]
        for c in range(2):
            lo, hi = c * fh, (c + 1) * fh
            glu = _dot(x, w1b_ref[:, lo:hi]) + b1_ref[:, lo:hi]
            lin = _dot(x, w1b_ref[:, f + lo:f + hi]) + b1_ref[:, f + lo:f + hi]
            glu = jnp.minimum(glu, SWIGLU_LIMIT)
            lin = jnp.clip(lin, -SWIGLU_LIMIT, SWIGLU_LIMIT)
            act = glu * jax.nn.sigmoid(SWIGLU_ALPHA * glu) * (lin + 1.0)
            y = y + _dot(act.astype(BF16), w2b_ref[lo:hi, :])
        o_ref[...] = y.astype(o_ref.dtype)

    @pl.when(jnp.logical_not(active))
    def _():
        o_ref[...] = jnp.zeros_like(o_ref)


def _experts(xs, tile_expert, n_valid, w1, b1, w2, b2, layer):
    s, d = xs.shape
    depth, e, _, f2 = w1.shape
    f = f2 // 2
    tm = EXPERT_ROW_TILE
    grid_spec = pltpu.PrefetchScalarGridSpec(
        num_scalar_prefetch=2,
        grid=(s // tm,),
        in_specs=[pl.BlockSpec((tm, d), lambda j, te, nv: (j, 0)),
                  pl.BlockSpec((None, None, d, f2), lambda j, te, nv: (layer, te[j], 0, 0)),
                  pl.BlockSpec((None, None, 1, f2), lambda j, te, nv: (layer, te[j], 0, 0)),
                  pl.BlockSpec((None, None, f, d), lambda j, te, nv: (layer, te[j], 0, 0)),
                  pl.BlockSpec((None, None, 1, d), lambda j, te, nv: (layer, te[j], 0, 0))],
        out_specs=pl.BlockSpec((tm, d), lambda j, te, nv: (j, 0)),
        scratch_shapes=[pltpu.VMEM((d, f2), BF16), pltpu.VMEM((f, d), BF16)],
    )
    return pl.pallas_call(
        _expert_kernel,
        grid_spec=grid_spec,
        out_shape=jax.ShapeDtypeStruct((s, d), BF16),
        compiler_params=_cparams(("arbitrary",)),
        name="moe_experts",
    )(tile_expert, n_valid, xs, w1, b1.reshape(depth, e, 1, f2), w2, b2.reshape(depth, e, 1, d))


def _combine_kernel(p_ref, wt_ref, h_ref, mod_ref, o_ref):
    m = p_ref[0].astype(F32) * wt_ref[:, 0:1]
    for k in range(1, TOP_K):
        m = m + p_ref[k].astype(F32) * wt_ref[:, k:k + 1]
    o_ref[...] = h_ref[...] + mod_ref[5:6, :] * m


def _combine(picked, wt, h, mods, n_ctx, b0, b):
    _, tt, d = h.shape
    tm = ROW_TILE
    return pl.pallas_call(
        _combine_kernel,
        grid=(b, tt // tm),
        in_specs=[pl.BlockSpec((TOP_K, None, tm, d), lambda bi, i: (0, bi, i, 0)),
                  pl.BlockSpec((None, tm, LANES), lambda bi, i: (bi, i, 0)),
                  pl.BlockSpec((None, tm, d), lambda bi, i: (b0 + bi, i, 0)),
                  pl.BlockSpec((None, None, 6, d), lambda bi, i: (b0 + bi, _kind(i, n_ctx), 0, 0))],
        out_specs=pl.BlockSpec((None, tm, d), lambda bi, i: (bi, i, 0)),
        out_shape=jax.ShapeDtypeStruct((b, tt, d), F32),
        compiler_params=_cparams(("parallel", "parallel")),
        name="moe_combine",
    )(picked, wt, h, mods)


def _moe(h, mods, g, rt_w, rt_b, w1, b1, w2, b2, layer, n_ctx, b0, b):
    _, tt, d = h.shape
    n = b * tt
    tm = EXPERT_ROW_TILE
    f, idx, wt, rank, cnt = _router(h, mods, g, rt_w, rt_b, n_ctx, b0, b)
    top_i = idx.reshape(n, LANES)[:, :TOP_K]
    rank = rank.reshape(n, LANES)[:, :TOP_K]
    counts = cnt[0, :N_EXPERTS].astype(jnp.int32)
    padded = ((counts + tm - 1) // tm) * tm
    p_end = jnp.cumsum(padded)
    p_off = p_end - padded
    u_off = jnp.cumsum(counts) - counts
    slot = (p_off[top_i] + rank).reshape(-1)
    n_tiles = (n * TOP_K) // tm + N_EXPERTS
    s_rows = n_tiles * tm
    _, sorted_pair = lax.sort_key_val(slot, jnp.arange(n * TOP_K, dtype=jnp.int32))
    srow = jnp.arange(s_rows, dtype=jnp.int32)
    expert_of = lambda rows: jnp.minimum(jnp.sum((rows[:, None] >= p_end[None, :]).astype(jnp.int32), axis=1),
                                         N_EXPERTS - 1)
    row_e = expert_of(srow)
    within = srow - p_off[row_e]
    valid = within < counts[row_e]
    src = jnp.clip(u_off[row_e] + within, 0, n * TOP_K - 1)
    pair = sorted_pair[src]
    row_token = jnp.where(valid, pair // TOP_K, 0)
    n_valid = (p_end[-1] // tm).astype(jnp.int32).reshape(1)
    tile_start = jnp.arange(n_tiles, dtype=jnp.int32) * tm
    tile_expert = expert_of(tile_start)
    last_e = tile_expert[jnp.maximum(n_valid[0] - 1, 0)]
    tile_expert = jnp.where(tile_start < p_end[-1], tile_expert, last_e)
    xs = jnp.take(f.reshape(n, d), row_token, axis=0, mode="clip")
    ys = _experts(xs, tile_expert, n_valid, w1, b1, w2, b2, layer)
    picked = jnp.take(ys, slot.reshape(n, TOP_K).T.reshape(-1), axis=0, mode="clip").reshape(TOP_K, b, tt, d)
    return _combine(picked, wt, h, mods, n_ctx, b0, b)


def _rope_tables(n_tokens, n_ctx):
    rows = n_tokens // GRID_W
    row = jnp.repeat(jnp.arange(rows, dtype=F32), GRID_W)
    col = jnp.tile(jnp.arange(GRID_W, dtype=F32), rows)
    axis_dim = DF_HEAD_DIM // 2
    inv_freq = ROPE_THETA ** (-jnp.arange(0, axis_dim, 2, dtype=F32) / axis_dim)
    ar = row[:, None] * inv_freq
    ac = col[:, None] * inv_freq
    ang = jnp.concatenate([ar, ar, ac, ac], axis=-1)
    cos = jnp.concatenate([jnp.ones((n_ctx, DF_HEAD_DIM), F32), jnp.cos(ang)], axis=0)
    sin = jnp.concatenate([jnp.zeros((n_ctx, DF_HEAD_DIM), F32), jnp.sin(ang)], axis=0)
    return jnp.tile(cos, (1, 2)), jnp.tile(sin, (1, 2))


def _even_layer(h, mods, g1, p, li, n_ctx, rope):
    z = _nm_mm(h, mods, g1, p["w_in"].astype(BF16), n_ctx)
    r, v, kk, kd0, kd1, lw0, lw1, bb0, bb1, gate, bonus = _rw_features(z, p, n_ctx)
    yf, yb = _rw_scan(r, v, kk, kd0, lw0, bb0, kd1, lw1, bb1, n_ctx)
    lam_init = 0.8 - 0.6 * math.exp(-0.3 * li)
    lp = p["lam"].astype(F32)
    lam = jnp.exp(jnp.sum(lp[0] * lp[1])) - jnp.exp(jnp.sum(lp[2] * lp[3])) + lam_init
    ob = _diff_attention(z, p, lam, lam_init, rope[0], rope[1], n_ctx)
    return _mix_out(yf, yb, bonus, gate, ob, p, h, mods, n_ctx)


def kernel(x, c, ctx, c_ctx, norm1_g, norm2_g, ada_w, ada_b, ev_w_in, ev_w_out, rw_mu, rw_w0, rw_w2, rw_a0, rw_a2, rw_g2, rw_kk, rw_ka, rw_rk, rw_gn_w, rw_gn_b, df_qn, df_kn, df_lam, df_subln, sg_w_in, sg_b_in, sg_ln_g, sg_ln_b, sg_ws, sg_bs, sg_w_out, rt_w, rt_b, ex_w1, ex_b1, ex_w2, ex_b2):
    b, t, d = x.shape
    n_ctx = ctx.shape[1]
    depth = ada_w.shape[0]
    assert d == D_MODEL and t % ROW_TILE == 0 and n_ctx % ROW_TILE == 0 and t % GRID_W == 0
    rope = _rope_tables(t, n_ctx)
    r_pad = -(-(b + 1) // 8) * 8
    cs = jnp.zeros((r_pad, d), F32).at[:b].set(c).at[b].set(c_ctx)
    ada = _ada_table(cs, ada_w, ada_b).reshape(depth, r_pad, 6, d)
    h = jnp.concatenate([ctx, x], axis=1)
    for l in range(depth):
        ctx_out = any(m % 2 == 0 for m in range(l + 1, depth))
        j = l // 2
        mods = jnp.stack([jnp.broadcast_to(ada[l, b], (b, 6, d)), ada[l, :b]], axis=1)
        if l % 2 == 0:
            p = {"w_in": ev_w_in[j], "w_out": ev_w_out[j], "mu": rw_mu[j], "w0": rw_w0[j], "w2": rw_w2[j],
                 "a0": rw_a0[j], "a2": rw_a2[j], "g2": rw_g2[j], "k_k": rw_kk[j], "k_a": rw_ka[j],
                 "r_k": rw_rk[j], "gn_w": rw_gn_w[j], "gn_b": rw_gn_b[j], "qn": df_qn[j], "kn": df_kn[j],
                 "lam": df_lam[j], "subln": df_subln[j]}
            h = _even_layer(h, mods, norm1_g[l], p, l, n_ctx, rope)
        else:
            p = {"w_in": sg_w_in[j], "b_in": sg_b_in[j], "ln_g": sg_ln_g[j], "ln_b": sg_ln_b[j],
                 "ws": sg_ws[j], "bs": sg_bs[j], "w_out": sg_w_out[j]}
            h = _sgu_layer(h, mods, norm1_g[l], p, n_ctx)
        if n_ctx > 0 and not ctx_out:
            h = h[:, n_ctx:]
            n_ctx = 0
        h = _moe(h, mods, norm2_g[l], rt_w[l], rt_b[l], ex_w1, ex_b1, ex_w2, ex_b2, l, n_ctx, 0, b)
    return h[:, n_ctx:]
```

```python
import functools
import math

import jax
import jax.numpy as jnp
from jax import lax
from jax.experimental import pallas as pl
from jax.experimental.pallas import tpu as pltpu

F32 = jnp.float32
BF16 = jnp.bfloat16
HIGHEST = lax.Precision.HIGHEST

D_MODEL = 1024
GRID_W = 64
RMS_EPS = 1e-6
LN_EPS = 1e-5

RW_HEAD_DIM = 64
RW_WIDTH = 512
RW_HEADS = RW_WIDTH // RW_HEAD_DIM
DECAY_LORA = 64
ICLR_LORA = 64
GATE_LORA = 128
RW_COLS = 3 * RW_WIDTH + 2 * DECAY_LORA + 2 * ICLR_LORA + GATE_LORA
RW_GN_EPS = 64e-5

DF_HEAD_DIM = 64
DF_V_DIM = 128
DF_WIDTH = 512
DF_HEADS = DF_WIDTH // DF_V_DIM
DF_QK = DF_HEADS * 2 * DF_HEAD_DIM
ROPE_THETA = 10000.0
EVEN_COLS = RW_COLS + 2 * DF_QK + DF_WIDTH

SGU_CHUNK = 128
SGU_WIDTH = 1024
SGU_GROUPS = 8
SGU_GROUP_DIM = SGU_WIDTH // SGU_GROUPS

N_EXPERTS = 32
TOP_K = 4
SWIGLU_LIMIT = 7.0
SWIGLU_ALPHA = 1.702

LANES = 128
ROW_TILE = 256
SCAN_CHUNK = 64
EXPERT_ROW_TILE = 512
VMEM_LIMIT = 56 * 1024 * 1024


def _cparams(sem):
    return pltpu.CompilerParams(dimension_semantics=sem, vmem_limit_bytes=VMEM_LIMIT)


def _dot(a, b, precision=None):
    return jnp.dot(a, b, preferred_element_type=F32, precision=precision)


def _dot_nt(a, b, precision=None):
    return lax.dot_general(a, b, (((1,), (1,)), ((), ())), preferred_element_type=F32, precision=precision)


def _dot_tn(a, b, precision=None):
    return lax.dot_general(a, b, (((0,), (0,)), ((), ())), preferred_element_type=F32, precision=precision)


def _norm_mod(x, g, scale, shift):
    ms = jnp.mean(x * x, axis=-1, keepdims=True)
    return (x * lax.rsqrt(ms + RMS_EPS) * g) * (1.0 + scale) + shift


def _ada_kernel(x_ref, w_ref, b_ref, o_ref):
    x = x_ref[...]
    s = x * jax.nn.sigmoid(x)
    o_ref[...] = _dot(s, w_ref[...], HIGHEST) + b_ref[...]


def _ada_table(cs, ada_w, ada_b):
    depth, d, n = ada_w.shape
    r = cs.shape[0]
    tn = 1024
    return pl.pallas_call(
        _ada_kernel,
        grid=(depth, n // tn),
        in_specs=[pl.BlockSpec((r, d), lambda l, j: (0, 0)),
                  pl.BlockSpec((None, d, tn), lambda l, j: (l, 0, j)),
                  pl.BlockSpec((None, 1, tn), lambda l, j: (l, 0, j))],
        out_specs=pl.BlockSpec((None, r, tn), lambda l, j: (l, 0, j)),
        out_shape=jax.ShapeDtypeStruct((depth, r, n), F32),
        compiler_params=_cparams(("parallel", "parallel")),
        name="ada_table",
    )(cs, ada_w, ada_b.reshape(depth, 1, n))


def _kind(i, n_ctx):
    return (i * ROW_TILE >= n_ctx).astype(jnp.int32) if n_ctx > 0 else 1


def _nm_mm_kernel(h_ref, mod_ref, g_ref, w_ref, o_ref):
    y = _norm_mod(h_ref[...], g_ref[...], mod_ref[1:2, :], mod_ref[0:1, :])
    o_ref[...] = _dot(y.astype(BF16), w_ref[...]).astype(o_ref.dtype)


def _nm_mm(h, mods, g, w, n_ctx, out_dtype=F32):
    b, tt, d = h.shape
    n = w.shape[1]
    tm = ROW_TILE
    return pl.pallas_call(
        _nm_mm_kernel,
        grid=(b, tt // tm),
        in_specs=[pl.BlockSpec((None, tm, d), lambda bi, i: (bi, i, 0)),
                  pl.BlockSpec((None, None, 6, d), lambda bi, i: (bi, _kind(i, n_ctx), 0, 0)),
                  pl.BlockSpec((1, d), lambda bi, i: (0, 0)),
                  pl.BlockSpec((d, n), lambda bi, i: (0, 0))],
        out_specs=pl.BlockSpec((None, tm, n), lambda bi, i: (bi, i, 0)),
        out_shape=jax.ShapeDtypeStruct((b, tt, n), out_dtype),
        compiler_params=_cparams(("parallel", "parallel")),
        name="norm_mod_proj",
    )(h, mods, g.reshape(1, d), w)


def _group_ones(width, group):
    idx = jnp.arange(width) // group
    return (idx[:, None] == idx[None, :]).astype(BF16)


def _split_hi_lo(w):
    hi = w.astype(BF16)
    return jnp.stack([hi, (w - hi.astype(F32)).astype(BF16)])


def _dot3(x, w_hi, w_lo):
    xh = x.astype(BF16)
    xl = (x - xh.astype(F32)).astype(BF16)
    return _dot(xh, w_hi) + _dot(xl, w_hi) + _dot(xh, w_lo)


def _group_sum(x, ones_bf16):
    hi = x.astype(BF16)
    lo = (x - hi.astype(F32)).astype(BF16)
    return _dot(hi, ones_bf16) + _dot(lo, ones_bf16)


def _softplus(x):
    return jnp.maximum(x, 0.0) + jnp.log1p(jnp.exp(-jnp.abs(x)))


def _rw_feat_kernel(z_ref, zprev_ref, znext_ref, mu_ref, w0_ref, w2_ref, a0_ref, a2_ref, g2_ref, kk_ref, ka_ref,
                    rk_ref, seg_ref,
                    r_ref, v_ref, kko_ref, kd0_ref, kd1_ref, lw0_ref, lw1_ref, bb0_ref, bb1_ref, g_ref, bonus_ref,
                    *, n_ctx, tt):
    c = RW_WIDTH
    tm = z_ref.shape[0]
    i = pl.program_id(1)
    z = z_ref[...]
    seq_first = jnp.logical_or(i * tm == 0, i * tm == n_ctx)
    seq_last = jnp.logical_or((i + 1) * tm == n_ctx, (i + 1) * tm == tt)
    prev_row = jnp.where(seq_first, 0.0, zprev_ref[7:8, :])
    next_row = jnp.where(seq_last, 0.0, znext_ref[0:1, :])
    rows = lax.broadcasted_iota(jnp.int32, z.shape, 0)
    zp = jnp.where(rows == 0, prev_row, pltpu.roll(z, 1, 0))
    zn = jnp.where(rows == tm - 1, next_row, pltpu.roll(z, tm - 1, 0))
    za = z + mu_ref[0:1, :] * (zp - z) + mu_ref[1:2, :] * (zn - z)
    r = za[:, :c]
    k = za[:, c:2 * c]
    v = za[:, 2 * c:3 * c]
    wl = za[:, 3 * c:3 * c + 2 * DECAY_LORA]
    al = za[:, 3 * c + 2 * DECAY_LORA:3 * c + 2 * DECAY_LORA + 2 * ICLR_LORA]
    gl = za[:, 3 * c + 2 * DECAY_LORA + 2 * ICLR_LORA:]
    r_ref[...] = r.astype(r_ref.dtype)
    v_ref[...] = v.astype(v_ref.dtype)
    twl = jnp.tanh(wl)
    seg = seg_ref[...]
    kkr = k * kk_ref[...]
    kk = kkr / jnp.maximum(jnp.sqrt(_group_sum(kkr * kkr, seg)), 1e-12)
    kko_ref[...] = kk.astype(kko_ref.dtype)
    lw_refs = (lw0_ref, lw1_ref)
    bb_refs = (bb0_ref, bb1_ref)
    kd_refs = (kd0_ref, kd1_ref)
    kd_sum = None
    for d in range(2):
        wd = w0_ref[d:d + 1, :] + _dot3(twl[:, d * DECAY_LORA:(d + 1) * DECAY_LORA], w2_ref[0, d], w2_ref[1, d])
        w = -_softplus(-wd) - 0.5
        lw_refs[d][...] = -jnp.exp(w)
        a = jax.nn.sigmoid(a0_ref[d:d + 1, :] + _dot3(al[:, d * ICLR_LORA:(d + 1) * ICLR_LORA], a2_ref[0, d], a2_ref[1, d]))
        bb_refs[d][...] = (kk * a).astype(bb_refs[d].dtype)
        kd = k * (1.0 + (a - 1.0) * ka_ref[...])
        kd_refs[d][...] = kd.astype(kd_refs[d].dtype)
        kd_sum = kd if kd_sum is None else kd_sum + kd
    g_ref[...] = _dot3(jax.nn.sigmoid(gl), g2_ref[0], g2_ref[1])
    bonus_ref[...] = _group_sum(r * kd_sum * rk_ref[...], seg) * v


def _rw_features(z, p, n_ctx):
    b, tt, _ = z.shape
    c = RW_WIDTH
    tm = ROW_TILE
    hb = tm // 8
    last_halo = tt // 8 - 1
    full = lambda shape: pl.BlockSpec(shape, lambda bi, i: (0,) * len(shape))
    out_spec = pl.BlockSpec((None, tm, c), lambda bi, i: (bi, i, 0))
    return pl.pallas_call(
        functools.partial(_rw_feat_kernel, n_ctx=n_ctx, tt=tt),
        grid=(b, tt // tm),
        in_specs=[pl.BlockSpec((None, tm, RW_COLS), lambda bi, i: (bi, i, 0)),
                  pl.BlockSpec((None, 8, RW_COLS), lambda bi, i: (bi, jnp.maximum(i * hb - 1, 0), 0)),
                  pl.BlockSpec((None, 8, RW_COLS), lambda bi, i: (bi, jnp.minimum((i + 1) * hb, last_halo), 0)),
                  full((2, RW_COLS)), full((2, c)), full((2, 2, DECAY_LORA, c)), full((2, c)), full((2, 2, ICLR_LORA, c)),
                  full((2, GATE_LORA, c)), full((1, c)), full((1, c)), full((1, c)), full((c, c))],
        out_specs=[out_spec] * 11,
        out_shape=[jax.ShapeDtypeStruct((b, tt, c), dt) for dt in (BF16,) * 5 + (F32, F32, BF16, BF16, F32, F32)],
        compiler_params=_cparams(("parallel", "parallel")),
        name="rwkv_features",
    )(z, z, z, p["mu"], p["w0"], _split_hi_lo(p["w2"]), p["a0"], _split_hi_lo(p["a2"]), _split_hi_lo(p["g2"]), p["k_k"].reshape(1, c), p["k_a"].reshape(1, c),
      p["r_k"].reshape(1, c), _group_ones(c, RW_HEAD_DIM))


def _cumsum_rows(tri_bf16, x):
    hi = x.astype(BF16)
    r1 = x - hi.astype(F32)
    mid = r1.astype(BF16)
    lo = (r1 - mid.astype(F32)).astype(BF16)
    return _dot(tri_bf16, hi) + _dot(tri_bf16, mid) + _dot(tri_bf16, lo)


def _scan_masks(reverse):
    L = SCAN_CHUNK
    row = lax.broadcasted_iota(jnp.int32, (L, L), 0)
    col = lax.broadcasted_iota(jnp.int32, (L, L), 1)
    incl = (col >= row) if reverse else (col <= row)
    strict = (col > row) if reverse else (col < row)
    levels = []
    for lg in range(int(math.log2(L))):
        same_pair = jnp.right_shift(row, lg + 1) == jnp.right_shift(col, lg + 1)
        other_half = jnp.right_shift(row, lg) != jnp.right_shift(col, lg)
        levels.append(jnp.logical_and(jnp.logical_and(same_pair, other_half), strict))
    return incl, strict, row == col, levels


def _scan_operands(r_ref, v_ref, kk_ref, kd_ref, lw_ref, bb_ref, incl, reverse):
    L = SCAN_CHUNK
    lw = lw_ref[...]
    cum = _cumsum_rows(incl.astype(BF16), lw)
    ctot = cum[0:1, :] if reverse else cum[L - 1:L, :]
    g_inv = jnp.exp(-cum)
    g_end = jnp.exp(ctot - cum)
    kd = kd_ref[...].astype(F32)
    bb = bb_ref[...].astype(F32)
    r_f = r_ref[...].astype(F32) * jnp.exp(cum)
    return dict(a=(-kk_ref[...].astype(F32) * jnp.exp(cum - lw)).astype(BF16), b=(bb * g_inv).astype(BF16),
                k=(kd * g_inv).astype(BF16), r_f=r_f, r=r_f.astype(BF16), b_e=(bb * g_end).astype(BF16),
                k_e=(kd * g_end).astype(BF16), v=v_ref[...].astype(BF16), g_tot=jnp.exp(ctot))


def _scan_kernel(rf_ref, vf_ref, kkf_ref, kdf_ref, lwf_ref, bbf_ref, rb_ref, vb_ref, kkb_ref, kdb_ref, lwb_ref, bbb_ref,
                 yf_ref, yb_ref, s_ref):
    L = SCAN_CHUNK
    n = RW_HEAD_DIM

    @pl.when(pl.program_id(1) == 0)
    def _():
        s_ref[...] = jnp.zeros_like(s_ref)

    masks = (_scan_masks(False), _scan_masks(True))
    ops = (_scan_operands(rf_ref, vf_ref, kkf_ref, kdf_ref, lwf_ref, bbf_ref, masks[0][0], False),
           _scan_operands(rb_ref, vb_ref, kkb_ref, kdb_ref, lwb_ref, bbb_ref, masks[1][0], True))
    y_refs = (yf_ref, yb_ref)
    chains = [(d, h) for h in range(RW_HEADS) for d in range(2)]
    cut = lambda d, h, name: ops[d][name][:, h * n:(h + 1) * n]

    gram = [_dot_nt(jnp.concatenate([cut(d, h, "a"), cut(d, h, "r")], axis=0),
                    jnp.concatenate([cut(d, h, "b"), cut(d, h, "k")], axis=0)) for d, h in chains]
    n_ab = [g[:L, :L] for g in gram]
    lower = [jnp.concatenate([jnp.where(masks[d][1], g[:L, L:], 0.0), jnp.where(masks[d][0], g[L:, L:], 0.0)],
                             axis=0).astype(BF16) for (d, h), g in zip(chains, gram)]
    m_rb = [jnp.where(masks[d][0], g[L:, :L], 0.0).astype(BF16) for (d, h), g in zip(chains, gram)]
    nv = [_dot(lo, cut(d, h, "v")) for (d, h), lo in zip(chains, lower)]
    t_inv = [jnp.where(masks[d][2], 1.0, jnp.where(masks[d][3][0], nab, 0.0)) for (d, h), nab in zip(chains, n_ab)]
    for lv in range(1, len(masks[0][3])):
        tb = [t.astype(BF16) for t in t_inv]
        half = [_dot(t, jnp.where(masks[d][3][lv], nab, 0.0).astype(BF16)).astype(BF16)
                for (d, h), t, nab in zip(chains, tb, n_ab)]
        t_inv = [t + _dot(hf, t16) for t, hf, t16 in zip(t_inv, half, tb)]
    au = [_dot(t.astype(BF16), jnp.concatenate([cut(d, h, "a"), x[:L].astype(BF16)], axis=1)).astype(BF16)
          for (d, h), t, x in zip(chains, t_inv, nv)]
    ry = [_dot(m, x) for m, x in zip(m_rb, au)]
    pq = [_dot_tn(x, cut(d, h, "b_e")) for (d, h), x in zip(chains, au)]
    vk = [_dot_tn(cut(d, h, "v"), cut(d, h, "k_e")) for d, h in chains]
    for i, (d, h) in enumerate(chains):
        sl = slice(h * n, (h + 1) * n)
        r_hat = (ops[d]["r_f"][:, sl] + ry[i][:, :n]).astype(BF16)
        y_hat = ry[i][:, n:] + nv[i][L:]
        p_mat = jnp.where(masks[d][2], ops[d]["g_tot"][:, sl], 0.0) + pq[i][:n]
        q_mat = pq[i][n:] + vk[i]
        s0b = s_ref[d, h].astype(BF16)
        y_refs[d][:, sl] = _dot_nt(r_hat, s0b) + y_hat
        s_ref[d, h] = _dot(s0b, p_mat.astype(BF16)) + q_mat


def _rw_scan(r, v, kk, kd0, lw0, bb0, kd1, lw1, bb1, n_ctx):
    b, tt, c = r.shape
    L = SCAN_CHUNK
    nc = tt // L
    ncc = n_ctx // L
    fwd = pl.BlockSpec((None, L, c), lambda bi, i: (bi, i, 0))
    bwd = pl.BlockSpec((None, L, c), lambda bi, i: (bi, jnp.where(i < ncc, ncc - 1 - i, nc - 1 - (i - ncc)), 0))
    return pl.pallas_call(
        _scan_kernel,
        grid=(b, nc),
        in_specs=[fwd] * 6 + [bwd] * 6,
        out_specs=[fwd, bwd],
        out_shape=[jax.ShapeDtypeStruct((b, tt, c), F32)] * 2,
        scratch_shapes=[pltpu.VMEM((2, RW_HEADS, RW_HEAD_DIM, RW_HEAD_DIM), F32)],
        compiler_params=_cparams(("parallel", "arbitrary")),
        name="rwkv_scan",
    )(r, v, kk, kd0, lw0, bb0, r, v, kk, kd1, lw1, bb1)


def _qk_prep(x, gain, cos, sin, ones):
    xn = x * lax.rsqrt(_group_sum(x * x, ones) * (1.0 / DF_HEAD_DIM) + RMS_EPS) * gain
    lane = lax.broadcasted_iota(jnp.int32, x.shape, 1)
    quarter = DF_HEAD_DIM // 4
    rot = jnp.where(jnp.bitwise_and(lane, 2 * quarter - 1) < quarter, -pltpu.roll(xn, x.shape[1] - quarter, 1), pltpu.roll(xn, quarter, 1))
    return xn * cos + rot * sin


def _attn_kernel(lam_ref, q_ref, k_ref, v_ref, cosq_ref, sinq_ref, cosk_ref, sink_ref, qn_ref, kn_ref, sub_ref, ones_ref,
                 o_ref, kb_ref, vb_ref, *, n_ctx, tq, out_scale):
    e = DF_HEAD_DIM
    tt = k_ref.shape[0]
    qi = pl.program_id(2)
    ones = ones_ref[...]

    @pl.when(qi == 0)
    def _():
        kb_ref[...] = _qk_prep(k_ref[...], kn_ref[...], cosk_ref[...], sink_ref[...], ones).astype(BF16)
        vb_ref[...] = v_ref[...].astype(BF16)

    q = (_qk_prep(q_ref[...], qn_ref[...], cosq_ref[...], sinq_ref[...], ones)
         * (DF_HEAD_DIM ** -0.5 * math.log2(math.e))).astype(BF16)

    def attend(kv_len):
        k = kb_ref[:kv_len, :]
        v = vb_ref[:kv_len, :]
        outs = []
        for m in range(2):
            s = _dot_nt(q[:, m * e:(m + 1) * e], k[:, m * e:(m + 1) * e])
            p = jnp.exp2(s - jnp.max(s, axis=-1, keepdims=True))
            l = jnp.sum(p, axis=-1, keepdims=True)
            outs.append(_dot(p.astype(BF16), v) / l)
        o = outs[0] - lam_ref[0] * outs[1]
        o = o * lax.rsqrt(jnp.mean(o * o, axis=-1, keepdims=True) + RMS_EPS) * sub_ref[...] * out_scale
        o_ref[...] = o.astype(o_ref.dtype)

    if n_ctx > 0:
        pl.when(qi * tq < n_ctx)(lambda: attend(n_ctx))
        pl.when(qi * tq >= n_ctx)(lambda: attend(tt))
    else:
        attend(tt)


def _diff_attention(z, p, lam, lam_init, cos, sin, n_ctx):
    b, tt, _ = z.shape
    tq = ROW_TILE
    w = DF_V_DIM
    q0 = RW_COLS // w
    k0 = q0 + DF_HEADS
    v0 = k0 + DF_HEADS
    two = lambda g: jnp.tile(g, 2).reshape(1, w)
    full = lambda shape: pl.BlockSpec(shape, lambda bi, hi, i: (0,) * len(shape))
    return pl.pallas_call(
        functools.partial(_attn_kernel, n_ctx=n_ctx, tq=tq, out_scale=1.0 - lam_init),
        grid=(b, DF_HEADS, tt // tq),
        in_specs=[pl.BlockSpec(memory_space=pltpu.SMEM),
                  pl.BlockSpec((None, tq, w), lambda bi, hi, i: (bi, i, q0 + hi)),
                  pl.BlockSpec((None, tt, w), lambda bi, hi, i: (bi, 0, k0 + hi)),
                  pl.BlockSpec((None, tt, w), lambda bi, hi, i: (bi, 0, v0 + hi)),
                  pl.BlockSpec((tq, w), lambda bi, hi, i: (i, 0)),
                  pl.BlockSpec((tq, w), lambda bi, hi, i: (i, 0)),
                  full((tt, w)), full((tt, w)), full((1, w)), full((1, w)), full((1, w)), full((w, w))],
        out_specs=pl.BlockSpec((None, tq, w), lambda bi, hi, i: (bi, i, hi)),
        out_shape=jax.ShapeDtypeStruct((b, tt, DF_WIDTH), BF16),
        scratch_shapes=[pltpu.VMEM((tt, w), BF16), pltpu.VMEM((tt, w), BF16)],
        compiler_params=_cparams(("parallel", "parallel", "arbitrary")),
        name="diff_attention",
    )(lam.reshape(1).astype(F32), z, z, z, cos, sin, cos, sin, two(p["qn"]), two(p["kn"]),
      p["subln"].reshape(1, w), _group_ones(w, DF_HEAD_DIM))


def _mix_out_kernel(yf_ref, yb_ref, bonus_ref, gate_ref, ob_ref, gnw_ref, gnb_ref, seg_ref, w_ref, h_ref, mod_ref, o_ref):
    seg = seg_ref[...]
    y = yf_ref[...] + yb_ref[...]
    mu = _group_sum(y, seg) * (1.0 / RW_HEAD_DIM)
    d = y - mu
    var = _group_sum(d * d, seg) * (1.0 / RW_HEAD_DIM)
    yn = d * lax.rsqrt(var + RW_GN_EPS) * gnw_ref[...] + gnb_ref[...]
    ya = (yn + bonus_ref[...]) * gate_ref[...]
    mix = _dot(ya.astype(BF16), w_ref[:RW_WIDTH, :]) + _dot(ob_ref[...], w_ref[RW_WIDTH:, :])
    o_ref[...] = h_ref[...] + mod_ref[2:3, :] * mix


def _mix_out(yf, yb, bonus, gate, ob, p, h, mods, n_ctx):
    b, tt, d = h.shape
    c = RW_WIDTH
    tm = ROW_TILE
    tok = lambda width: pl.BlockSpec((None, tm, width), lambda bi, i: (bi, i, 0))
    full = lambda shape: pl.BlockSpec(shape, lambda bi, i: (0,) * len(shape))
    return pl.pallas_call(
        _mix_out_kernel,
        grid=(b, tt // tm),
        in_specs=[tok(c), tok(c), tok(c), tok(c), tok(DF_WIDTH), full((1, c)), full((1, c)), full((c, c)),
                  full((c + DF_WIDTH, d)), tok(d),
                  pl.BlockSpec((None, None, 6, d), lambda bi, i: (bi, _kind(i, n_ctx), 0, 0))],
        out_specs=tok(d),
        out_shape=jax.ShapeDtypeStruct((b, tt, d), F32),
        compiler_params=_cparams(("parallel", "parallel")),
        name="mix_out_residual",
    )(yf, yb, bonus, gate, ob, p["gn_w"].reshape(1, c), p["gn_b"].reshape(1, c), _group_ones(c, RW_HEAD_DIM),
      p["w_out"].astype(BF16), h, mods)


def _sgu_kernel(h_ref, mod_ref, g_ref, w_in_ref, b_in_ref, lng_ref, lnb_ref, ws_ref, bst_ref, w_out_ref, o_ref):
    x = h_ref[...]
    tm = x.shape[0]
    y = _norm_mod(x, g_ref[...], mod_ref[1:2, :], mod_ref[0:1, :])
    z = _dot(y.astype(BF16), w_in_ref[...]) + b_in_ref[...]
    z = 0.5 * z * (1.0 + lax.erf(z * (2.0 ** -0.5)))
    u = z[:, :SGU_WIDTH]
    v = z[:, SGU_WIDTH:]
    mu = jnp.mean(v, axis=-1, keepdims=True)
    var = jnp.mean(jnp.square(v - mu), axis=-1, keepdims=True)
    v = ((v - mu) * lax.rsqrt(var + LN_EPS) * lng_ref[...] + lnb_ref[...]).astype(BF16)
    rows = []
    for ci in range(tm // SGU_CHUNK):
        cols = []
        for gi in range(SGU_GROUPS):
            vb = v[ci * SGU_CHUNK:(ci + 1) * SGU_CHUNK, gi * SGU_GROUP_DIM:(gi + 1) * SGU_GROUP_DIM]
            cols.append(_dot(ws_ref[gi], vb) + bst_ref[:, gi:gi + 1])
        rows.append(jnp.concatenate(cols, axis=1))
    sv = jnp.concatenate(rows, axis=0)
    o_ref[...] = x + mod_ref[2:3, :] * _dot((u * sv).astype(BF16), w_out_ref[...])


def _sgu_layer(h, mods, g, p, n_ctx):
    b, tt, d = h.shape
    tm = ROW_TILE
    full = lambda shape: pl.BlockSpec(shape, lambda bi, i: (0,) * len(shape))
    return pl.pallas_call(
        _sgu_kernel,
        grid=(b, tt // tm),
        in_specs=[pl.BlockSpec((None, tm, d), lambda bi, i: (bi, i, 0)),
                  pl.BlockSpec((None, None, 6, d), lambda bi, i: (bi, _kind(i, n_ctx), 0, 0)),
                  full((1, d)), full((d, 2 * SGU_WIDTH)), full((1, 2 * SGU_WIDTH)),
                  full((1, SGU_WIDTH)), full((1, SGU_WIDTH)),
                  full((SGU_GROUPS, SGU_CHUNK, SGU_CHUNK)), full((SGU_CHUNK, SGU_GROUPS)),
                  full((SGU_WIDTH, d))],
        out_specs=pl.BlockSpec((None, tm, d), lambda bi, i: (bi, i, 0)),
        out_shape=jax.ShapeDtypeStruct((b, tt, d), F32),
        compiler_params=_cparams(("parallel", "parallel")),
        name="sgu_layer",
    )(h, mods, g.reshape(1, d), p["w_in"].astype(BF16), p["b_in"].reshape(1, -1),
      p["ln_g"].reshape(1, -1), p["ln_b"].reshape(1, -1), p["ws"].astype(BF16), p["bs"].T,
      p["w_out"].astype(BF16))


def _router_kernel(h_ref, mod_ref, g_ref, rwh_ref, rwl_ref, rb_ref, f_ref, idx_ref, wt_ref, rank_ref, cnt_ref, base_ref):
    first = jnp.logical_and(pl.program_id(0) == 0, pl.program_id(1) == 0)

    @pl.when(first)
    def _():
        base_ref[...] = jnp.zeros_like(base_ref)

    x = h_ref[...]
    tm = x.shape[0]
    f = _norm_mod(x, g_ref[...], mod_ref[4:5, :], mod_ref[3:4, :])
    fh = f.astype(BF16)
    fl = (f - fh.astype(F32)).astype(BF16)
    f_ref[...] = fh
    logits = _dot(fh, rwh_ref[...]) + _dot(fl, rwh_ref[...]) + _dot(fh, rwl_ref[...]) + rb_ref[...]
    lane = lax.broadcasted_iota(jnp.int32, (tm, LANES), 1)
    tri = (lax.broadcasted_iota(jnp.int32, (tm, tm), 1) < lax.broadcasted_iota(jnp.int32, (tm, tm), 0))
    vals, sels = [], []
    idx_out = jnp.zeros((tm, LANES), jnp.int32)
    l = logits
    for kk in range(TOP_K):
        m = jnp.max(l, axis=-1, keepdims=True)
        idx = jnp.min(jnp.where(l == m, lane, LANES), axis=-1, keepdims=True)
        sel = lane == idx
        l = jnp.where(sel, -jnp.inf, l)
        vals.append(m)
        sels.append(sel)
        idx_out = jnp.where(lane == kk, idx, idx_out)
    es = [jnp.exp(vv - vals[0]) for vv in vals]
    den = es[0] + es[1] + es[2] + es[3]
    wt_out = jnp.zeros((tm, LANES), F32)
    for kk in range(TOP_K):
        wt_out = jnp.where(lane == kk, es[kk] / den, wt_out)
    onehot = jnp.zeros((tm, LANES), F32)
    for sel in sels:
        onehot = onehot + sel.astype(F32)
    ahead = _dot(tri.astype(BF16), onehot.astype(BF16)) + base_ref[...]
    rank_out = jnp.zeros((tm, LANES), jnp.int32)
    for kk in range(TOP_K):
        rk = jnp.sum(jnp.where(sels[kk], ahead, 0.0), axis=-1, keepdims=True).astype(jnp.int32)
        rank_out = jnp.where(lane == kk, rk, rank_out)
    idx_ref[...] = idx_out
    wt_ref[...] = wt_out
    rank_ref[...] = rank_out
    base_ref[...] = base_ref[...] + jnp.sum(onehot, axis=0, keepdims=True)
    cnt_ref[...] = base_ref[...]


def _router(h, mods, g, rt_w, rt_b, n_ctx, b0, b):
    _, tt, d = h.shape
    tm = ROW_TILE
    rw = jnp.zeros((d, LANES), F32).at[:, :N_EXPERTS].set(rt_w)
    rw_hi = rw.astype(BF16)
    rw_lo = (rw - rw_hi.astype(F32)).astype(BF16)
    rb = jnp.full((1, LANES), -jnp.inf, F32).at[0, :N_EXPERTS].set(rt_b)
    tok = lambda width: pl.BlockSpec((None, tm, width), lambda bi, i: (bi, i, 0))
    return pl.pallas_call(
        _router_kernel,
        grid=(b, tt // tm),
        in_specs=[pl.BlockSpec((None, tm, d), lambda bi, i: (b0 + bi, i, 0)),
                  pl.BlockSpec((None, None, 6, d), lambda bi, i: (b0 + bi, _kind(i, n_ctx), 0, 0)),
                  pl.BlockSpec((1, d), lambda bi, i: (0, 0)),
                  pl.BlockSpec((d, LANES), lambda bi, i: (0, 0)),
                  pl.BlockSpec((d, LANES), lambda bi, i: (0, 0)),
                  pl.BlockSpec((1, LANES), lambda bi, i: (0, 0))],
        out_specs=[tok(d), tok(LANES), tok(LANES), tok(LANES), pl.BlockSpec((1, LANES), lambda bi, i: (0, 0))],
        out_shape=[jax.ShapeDtypeStruct((b, tt, d), BF16),
                   jax.ShapeDtypeStruct((b, tt, LANES), jnp.int32),
                   jax.ShapeDtypeStruct((b, tt, LANES), F32),
                   jax.ShapeDtypeStruct((b, tt, LANES), jnp.int32),
                   jax.ShapeDtypeStruct((1, LANES), F32)],
        scratch_shapes=[pltpu.VMEM((1, LANES), F32)],
        compiler_params=_cparams(("arbitrary", "arbitrary")),
        name="moe_router",
    )(h, mods, g.reshape(1, d), rw_hi, rw_lo, rb)


def _expert_kernel(te_ref, nv_ref, x_ref, w1_ref, b1_ref, w2_ref, b2_ref, o_ref, w1b_ref, w2b_ref):
    j = pl.program_id(0)
    active = j < nv_ref[0]
    new_expert = jnp.logical_or(j == 0, te_ref[j] != te_ref[jnp.maximum(j - 1, 0)])

    @pl.when(jnp.logical_and(active, new_expert))
    def _():
        w1b_ref[...] = w1_ref[...].astype(BF16)
        w2b_ref[...] = w2_ref[...].astype(BF16)

    @pl.when(active)
    def _():
        f = w2b_ref.shape[0]
        fh = f // 2
        x = x_ref[...]
        y = b2_ref[...]
        for c in range(2):
            lo, hi = c * fh, (c + 1) * fh
            glu = _dot(x, w1b_ref[:, lo:hi]) + b1_ref[:, lo:hi]
            lin = _dot(x, w1b_ref[:, f + lo:f + hi]) + b1_ref[:, f + lo:f + hi]
            glu = jnp.minimum(glu, SWIGLU_LIMIT)
            lin = jnp.clip(lin, -SWIGLU_LIMIT, SWIGLU_LIMIT)
            act = glu * jax.nn.sigmoid(SWIGLU_ALPHA * glu) * (lin + 1.0)
            y = y + _dot(act.astype(BF16), w2b_ref[lo:hi, :])
        o_ref[...] = y.astype(o_ref.dtype)

    @pl.when(jnp.logical_not(active))
    def _():
        o_ref[...] = jnp.zeros_like(o_ref)


def _experts(xs, tile_expert, n_valid, w1, b1, w2, b2, layer):
    s, d = xs.shape
    depth, e, _, f2 = w1.shape
    f = f2 // 2
    tm = EXPERT_ROW_TILE
    grid_spec = pltpu.PrefetchScalarGridSpec(
        num_scalar_prefetch=2,
        grid=(s // tm,),
        in_specs=[pl.BlockSpec((tm, d), lambda j, te, nv: (j, 0)),
                  pl.BlockSpec((None, None, d, f2), lambda j, te, nv: (layer, te[j], 0, 0)),
                  pl.BlockSpec((None, None, 1, f2), lambda j, te, nv: (layer, te[j], 0, 0)),
                  pl.BlockSpec((None, None, f, d), lambda j, te, nv: (layer, te[j], 0, 0)),
                  pl.BlockSpec((None, None, 1, d), lambda j, te, nv: (layer, te[j], 0, 0))],
        out_specs=pl.BlockSpec((tm, d), lambda j, te, nv: (j, 0)),
        scratch_shapes=[pltpu.VMEM((d, f2), BF16), pltpu.VMEM((f, d), BF16)],
    )
    return pl.pallas_call(
        _expert_kernel,
        grid_spec=grid_spec,
        out_shape=jax.ShapeDtypeStruct((s, d), BF16),
        compiler_params=_cparams(("arbitrary",)),
        name="moe_experts",
    )(tile_expert, n_valid, xs, w1, b1.reshape(depth, e, 1, f2), w2, b2.reshape(depth, e, 1, d))


def _combine_kernel(p_ref, wt_ref, h_ref, mod_ref, o_ref):
    m = p_ref[0].astype(F32) * wt_ref[:, 0:1]
    for k in range(1, TOP_K):
        m = m + p_ref[k].astype(F32) * wt_ref[:, k:k + 1]
    o_ref[...] = h_ref[...] + mod_ref[5:6, :] * m


def _combine(picked, wt, h, mods, n_ctx, b0, b):
    _, tt, d = h.shape
    tm = ROW_TILE
    return pl.pallas_call(
        _combine_kernel,
        grid=(b, tt // tm),
        in_specs=[pl.BlockSpec((TOP_K, None, tm, d), lambda bi, i: (0, bi, i, 0)),
                  pl.BlockSpec((None, tm, LANES), lambda bi, i: (bi, i, 0)),
                  pl.BlockSpec((None, tm, d), lambda bi, i: (b0 + bi, i, 0)),
                  pl.BlockSpec((None, None, 6, d), lambda bi, i: (b0 + bi, _kind(i, n_ctx), 0, 0))],
        out_specs=pl.BlockSpec((None, tm, d), lambda bi, i: (bi, i, 0)),
        out_shape=jax.ShapeDtypeStruct((b, tt, d), F32),
        compiler_params=_cparams(("parallel", "parallel")),
        name="moe_combine",
    )(picked, wt, h, mods)


def _moe(h, mods, g, rt_w, rt_b, w1, b1, w2, b2, layer, n_ctx, b0, b):
    _, tt, d = h.shape
    n = b * tt
    tm = EXPERT_ROW_TILE
    f, idx, wt, rank, cnt = _router(h, mods, g, rt_w, rt_b, n_ctx, b0, b)
    top_i = idx.reshape(n, LANES)[:, :TOP_K]
    rank = rank.reshape(n, LANES)[:, :TOP_K]
    counts = cnt[0, :N_EXPERTS].astype(jnp.int32)
    padded = ((counts + tm - 1) // tm) * tm
    p_end = jnp.cumsum(padded)
    p_off = p_end - padded
    u_off = jnp.cumsum(counts) - counts
    slot = (p_off[top_i] + rank).reshape(-1)
    n_tiles = (n * TOP_K) // tm + N_EXPERTS
    s_rows = n_tiles * tm
    _, sorted_pair = lax.sort_key_val(slot, jnp.arange(n * TOP_K, dtype=jnp.int32))
    srow = jnp.arange(s_rows, dtype=jnp.int32)
    expert_of = lambda rows: jnp.minimum(jnp.sum((rows[:, None] >= p_end[None, :]).astype(jnp.int32), axis=1),
                                         N_EXPERTS - 1)
    row_e = expert_of(srow)
    within = srow - p_off[row_e]
    valid = within < counts[row_e]
    src = jnp.clip(u_off[row_e] + within, 0, n * TOP_K - 1)
    pair = sorted_pair[src]
    row_token = jnp.where(valid, pair // TOP_K, 0)
    n_valid = (p_end[-1] // tm).astype(jnp.int32).reshape(1)
    tile_start = jnp.arange(n_tiles, dtype=jnp.int32) * tm
    tile_expert = expert_of(tile_start)
    last_e = tile_expert[jnp.maximum(n_valid[0] - 1, 0)]
    tile_expert = jnp.where(tile_start < p_end[-1], tile_expert, last_e)
    xs = jnp.take(f.reshape(n, d), row_token, axis=0, mode="clip")
    ys = _experts(xs, tile_expert, n_valid, w1, b1, w2, b2, layer)
    picked = jnp.take(ys, slot.reshape(n, TOP_K).T.reshape(-1), axis=0, mode="clip").reshape(TOP_K, b, tt, d)
    return _combine(picked, wt, h, mods, n_ctx, b0, b)


def _rope_tables(n_tokens, n_ctx):
    rows = n_tokens // GRID_W
    row = jnp.repeat(jnp.arange(rows, dtype=F32), GRID_W)
    col = jnp.tile(jnp.arange(GRID_W, dtype=F32), rows)
    axis_dim = DF_HEAD_DIM // 2
    inv_freq = ROPE_THETA ** (-jnp.arange(0, axis_dim, 2, dtype=F32) / axis_dim)
    ar = row[:, None] * inv_freq
    ac = col[:, None] * inv_freq
    ang = jnp.concatenate([ar, ar, ac, ac], axis=-1)
    cos = jnp.concatenate([jnp.ones((n_ctx, DF_HEAD_DIM), F32), jnp.cos(ang)], axis=0)
    sin = jnp.concatenate([jnp.zeros((n_ctx, DF_HEAD_DIM), F32), jnp.sin(ang)], axis=0)
    return jnp.tile(cos, (1, 2)), jnp.tile(sin, (1, 2))


def _even_layer(h, mods, g1, p, li, n_ctx, rope):
    z = _nm_mm(h, mods, g1, p["w_in"].astype(BF16), n_ctx)
    r, v, kk, kd0, kd1, lw0, lw1, bb0, bb1, gate, bonus = _rw_features(z, p, n_ctx)
    yf, yb = _rw_scan(r, v, kk, kd0, lw0, bb0, kd1, lw1, bb1, n_ctx)
    lam_init = 0.8 - 0.6 * math.exp(-0.3 * li)
    lp = p["lam"].astype(F32)
    lam = jnp.exp(jnp.sum(lp[0] * lp[1])) - jnp.exp(jnp.sum(lp[2] * lp[3])) + lam_init
    ob = _diff_attention(z, p, lam, lam_init, rope[0], rope[1], n_ctx)
    return _mix_out(yf, yb, bonus, gate, ob, p, h, mods, n_ctx)


def kernel(x, c, ctx, c_ctx, norm1_g, norm2_g, ada_w, ada_b, ev_w_in, ev_w_out, rw_mu, rw_w0, rw_w2, rw_a0, rw_a2, rw_g2, rw_kk, rw_ka, rw_rk, rw_gn_w, rw_gn_b, df_qn, df_kn, df_lam, df_subln, sg_w_in, sg_b_in, sg_ln_g, sg_ln_b, sg_ws, sg_bs, sg_w_out, rt_w, rt_b, ex_w1, ex_b1, ex_w2, ex_b2):
    b, t, d = x.shape
    n_ctx = ctx.shape[1]
    depth = ada_w.shape[0]
    assert d == D_MODEL and t % ROW_TILE == 0 and n_ctx % ROW_TILE == 0 and t % GRID_W == 0
    rope = _rope_tables(t, n_ctx)
    r_pad = -(-(b + 1) // 8) * 8
    cs = jnp.zeros((r_pad, d), F32).at[:b].set(c).at[b].set(c_ctx)
    ada = _ada_table(cs, ada_w, ada_b).reshape(depth, r_pad, 6, d)
    h = jnp.concatenate([ctx, x], axis=1)
    for l in range(depth):
        ctx_out = any(m % 2 == 0 for m in range(l + 1, depth))
        j = l // 2
        mods = jnp.stack([jnp.broadcast_to(ada[l, b], (b, 6, d)), ada[l, :b]], axis=1)
        if l % 2 == 0:
            p = {"w_in": ev_w_in[j], "w_out": ev_w_out[j], "mu": rw_mu[j], "w0": rw_w0[j], "w2": rw_w2[j],
                 "a0": rw_a0[j], "a2": rw_a2[j], "g2": rw_g2[j], "k_k": rw_kk[j], "k_a": rw_ka[j],
                 "r_k": rw_rk[j], "gn_w": rw_gn_w[j], "gn_b": rw_gn_b[j], "qn": df_qn[j], "kn": df_kn[j],
                 "lam": df_lam[j], "subln": df_subln[j]}
            h = _even_layer(h, mods, norm1_g[l], p, l, n_ctx, rope)
        else:
            p = {"w_in": sg_w_in[j], "b_in": sg_b_in[j], "ln_g": sg_ln_g[j], "ln_b": sg_ln_b[j],
                 "ws": sg_ws[j], "bs": sg_bs[j], "w_out": sg_w_out[j]}
            h = _sgu_layer(h, mods, norm1_g[l], p, n_ctx)
        if n_ctx > 0 and not ctx_out:
            h = h[:, n_ctx:]
            n_ctx = 0
        h = _moe(h, mods, norm2_g[l], rt_w[l], rt_b[l], ex_w1, ex_b1, ex_w2, ex_b2, l, n_ctx, 0, b)
    return h[:, n_ctx:]
```

```python
import functools
import math

import jax
import jax.numpy as jnp
from jax import lax
from jax.experimental import pallas as pl
from jax.experimental.pallas import tpu as pltpu

F32 = jnp.float32
BF16 = jnp.bfloat16
HIGHEST = lax.Precision.HIGHEST

D_MODEL = 1024
GRID_W = 64
RMS_EPS = 1e-6
LN_EPS = 1e-5

RW_HEAD_DIM = 64
RW_WIDTH = 512
RW_HEADS = RW_WIDTH // RW_HEAD_DIM
DECAY_LORA = 64
ICLR_LORA = 64
GATE_LORA = 128
RW_COLS = 3 * RW_WIDTH + 2 * DECAY_LORA + 2 * ICLR_LORA + GATE_LORA
RW_GN_EPS = 64e-5

DF_HEAD_DIM = 64
DF_V_DIM = 128
DF_WIDTH = 512
DF_HEADS = DF_WIDTH // DF_V_DIM
DF_QK = DF_HEADS * 2 * DF_HEAD_DIM
ROPE_THETA = 10000.0
EVEN_COLS = RW_COLS + 2 * DF_QK + DF_WIDTH

SGU_CHUNK = 128
SGU_WIDTH = 1024
SGU_GROUPS = 8
SGU_GROUP_DIM = SGU_WIDTH // SGU_GROUPS

N_EXPERTS = 32
TOP_K = 4
SWIGLU_LIMIT = 7.0
SWIGLU_ALPHA = 1.702

LANES = 128
ROW_TILE = 256
SCAN_CHUNK = 64
EXPERT_ROW_TILE = 512
VMEM_LIMIT = 56 * 1024 * 1024


def _cparams(sem):
    return pltpu.CompilerParams(dimension_semantics=sem, vmem_limit_bytes=VMEM_LIMIT)


def _dot(a, b, precision=None):
    return jnp.dot(a, b, preferred_element_type=F32, precision=precision)


def _dot_nt(a, b, precision=None):
    return lax.dot_general(a, b, (((1,), (1,)), ((), ())), preferred_element_type=F32, precision=precision)


def _dot_tn(a, b, precision=None):
    return lax.dot_general(a, b, (((0,), (0,)), ((), ())), preferred_element_type=F32, precision=precision)


def _norm_mod(x, g, scale, shift):
    ms = jnp.mean(x * x, axis=-1, keepdims=True)
    return (x * lax.rsqrt(ms + RMS_EPS) * g) * (1.0 + scale) + shift


def _ada_kernel(x_ref, w_ref, b_ref, o_ref):
    x = x_ref[...]
    s = x * jax.nn.sigmoid(x)
    o_ref[...] = _dot(s, w_ref[...], HIGHEST) + b_ref[...]


def _ada_table(cs, ada_w, ada_b):
    depth, d, n = ada_w.shape
    r = cs.shape[0]
    tn = 1024
    return pl.pallas_call(
        _ada_kernel,
        grid=(depth, n // tn),
        in_specs=[pl.BlockSpec((r, d), lambda l, j: (0, 0)),
                  pl.BlockSpec((None, d, tn), lambda l, j: (l, 0, j)),
                  pl.BlockSpec((None, 1, tn), lambda l, j: (l, 0, j))],
        out_specs=pl.BlockSpec((None, r, tn), lambda l, j: (l, 0, j)),
        out_shape=jax.ShapeDtypeStruct((depth, r, n), F32),
        compiler_params=_cparams(("parallel", "parallel")),
        name="ada_table",
    )(cs, ada_w, ada_b.reshape(depth, 1, n))


def _kind(i, n_ctx):
    return (i * ROW_TILE >= n_ctx).astype(jnp.int32) if n_ctx > 0 else 1


def _nm_mm_kernel(h_ref, mod_ref, g_ref, w_ref, o_ref):
    y = _norm_mod(h_ref[...], g_ref[...], mod_ref[1:2, :], mod_ref[0:1, :])
    o_ref[...] = _dot(y.astype(BF16), w_ref[...]).astype(o_ref.dtype)


def _nm_mm(h, mods, g, w, n_ctx, out_dtype=F32):
    b, tt, d = h.shape
    n = w.shape[1]
    tm = ROW_TILE
    return pl.pallas_call(
        _nm_mm_kernel,
        grid=(b, tt // tm),
        in_specs=[pl.BlockSpec((None, tm, d), lambda bi, i: (bi, i, 0)),
                  pl.BlockSpec((None, None, 6, d), lambda bi, i: (bi, _kind(i, n_ctx), 0, 0)),
                  pl.BlockSpec((1, d), lambda bi, i: (0, 0)),
                  pl.BlockSpec((d, n), lambda bi, i: (0, 0))],
        out_specs=pl.BlockSpec((None, tm, n), lambda bi, i: (bi, i, 0)),
        out_shape=jax.ShapeDtypeStruct((b, tt, n), out_dtype),
        compiler_params=_cparams(("parallel", "parallel")),
        name="norm_mod_proj",
    )(h, mods, g.reshape(1, d), w)


def _group_ones(width, group):
    idx = jnp.arange(width) // group
    return (idx[:, None] == idx[None, :]).astype(BF16)


def _split_hi_lo(w):
    hi = w.astype(BF16)
    return jnp.stack([hi, (w - hi.astype(F32)).astype(BF16)])


def _dot3(x, w_hi, w_lo):
    xh = x.astype(BF16)
    xl = (x - xh.astype(F32)).astype(BF16)
    return _dot(xh, w_hi) + _dot(xl, w_hi) + _dot(xh, w_lo)


def _group_sum(x, ones_bf16):
    hi = x.astype(BF16)
    lo = (x - hi.astype(F32)).astype(BF16)
    return _dot(hi, ones_bf16) + _dot(lo, ones_bf16)


def _softplus(x):
    return jnp.maximum(x, 0.0) + jnp.log1p(jnp.exp(-jnp.abs(x)))


def _rw_feat_kernel(z_ref, zprev_ref, znext_ref, mu_ref, w0_ref, w2_ref, a0_ref, a2_ref, g2_ref, kk_ref, ka_ref,
                    rk_ref, seg_ref,
                    r_ref, v_ref, kko_ref, kd0_ref, kd1_ref, lw0_ref, lw1_ref, bb0_ref, bb1_ref, g_ref, bonus_ref,
                    *, n_ctx, tt):
    c = RW_WIDTH
    tm = z_ref.shape[0]
    i = pl.program_id(1)
    z = z_ref[...]
    seq_first = jnp.logical_or(i * tm == 0, i * tm == n_ctx)
    seq_last = jnp.logical_or((i + 1) * tm == n_ctx, (i + 1) * tm == tt)
    prev_row = jnp.where(seq_first, 0.0, zprev_ref[7:8, :])
    next_row = jnp.where(seq_last, 0.0, znext_ref[0:1, :])
    rows = lax.broadcasted_iota(jnp.int32, z.shape, 0)
    zp = jnp.where(rows == 0, prev_row, pltpu.roll(z, 1, 0))
    zn = jnp.where(rows == tm - 1, next_row, pltpu.roll(z, tm - 1, 0))
    za = z + mu_ref[0:1, :] * (zp - z) + mu_ref[1:2, :] * (zn - z)
    r = za[:, :c]
    k = za[:, c:2 * c]
    v = za[:, 2 * c:3 * c]
    wl = za[:, 3 * c:3 * c + 2 * DECAY_LORA]
    al = za[:, 3 * c + 2 * DECAY_LORA:3 * c + 2 * DECAY_LORA + 2 * ICLR_LORA]
    gl = za[:, 3 * c + 2 * DECAY_LORA + 2 * ICLR_LORA:]
    r_ref[...] = r.astype(r_ref.dtype)
    v_ref[...] = v.astype(v_ref.dtype)
    twl = jnp.tanh(wl)
    seg = seg_ref[...]
    kkr = k * kk_ref[...]
    kk = kkr / jnp.maximum(jnp.sqrt(_group_sum(kkr * kkr, seg)), 1e-12)
    kko_ref[...] = kk.astype(kko_ref.dtype)
    lw_refs = (lw0_ref, lw1_ref)
    bb_refs = (bb0_ref, bb1_ref)
    kd_refs = (kd0_ref, kd1_ref)
    kd_sum = None
    for d in range(2):
        wd = w0_ref[d:d + 1, :] + _dot3(twl[:, d * DECAY_LORA:(d + 1) * DECAY_LORA], w2_ref[0, d], w2_ref[1, d])
        w = -_softplus(-wd) - 0.5
        lw_refs[d][...] = -jnp.exp(w)
        a = jax.nn.sigmoid(a0_ref[d:d + 1, :] + _dot3(al[:, d * ICLR_LORA:(d + 1) * ICLR_LORA], a2_ref[0, d], a2_ref[1, d]))
        bb_refs[d][...] = (kk * a).astype(bb_refs[d].dtype)
        kd = k * (1.0 + (a - 1.0) * ka_ref[...])
        kd_refs[d][...] = kd.astype(kd_refs[d].dtype)
        kd_sum = kd if kd_sum is None else kd_sum + kd
    g_ref[...] = _dot3(jax.nn.sigmoid(gl), g2_ref[0], g2_ref[1])
    bonus_ref[...] = _group_sum(r * kd_sum * rk_ref[...], seg) * v


def _rw_features(z, p, n_ctx):
    b, tt, _ = z.shape
    c = RW_WIDTH
    tm = ROW_TILE
    hb = tm // 8
    last_halo = tt // 8 - 1
    full = lambda shape: pl.BlockSpec(shape, lambda bi, i: (0,) * len(shape))
    out_spec = pl.BlockSpec((None, tm, c), lambda bi, i: (bi, i, 0))
    return pl.pallas_call(
        functools.partial(_rw_feat_kernel, n_ctx=n_ctx, tt=tt),
        grid=(b, tt // tm),
        in_specs=[pl.BlockSpec((None, tm, RW_COLS), lambda bi, i: (bi, i, 0)),
                  pl.BlockSpec((None, 8, RW_COLS), lambda bi, i: (bi, jnp.maximum(i * hb - 1, 0), 0)),
                  pl.BlockSpec((None, 8, RW_COLS), lambda bi, i: (bi, jnp.minimum((i + 1) * hb, last_halo), 0)),
                  full((2, RW_COLS)), full((2, c)), full((2, 2, DECAY_LORA, c)), full((2, c)), full((2, 2, ICLR_LORA, c)),
                  full((2, GATE_LORA, c)), full((1, c)), full((1, c)), full((1, c)), full((c, c))],
        out_specs=[out_spec] * 11,
        out_shape=[jax.ShapeDtypeStruct((b, tt, c), dt) for dt in (BF16,) * 5 + (F32, F32, BF16, BF16, F32, F32)],
        compiler_params=_cparams(("parallel", "parallel")),
        name="rwkv_features",
    )(z, z, z, p["mu"], p["w0"], _split_hi_lo(p["w2"]), p["a0"], _split_hi_lo(p["a2"]), _split_hi_lo(p["g2"]), p["k_k"].reshape(1, c), p["k_a"].reshape(1, c),
      p["r_k"].reshape(1, c), _group_ones(c, RW_HEAD_DIM))


def _cumsum_rows(tri_bf16, x):
    hi = x.astype(BF16)
    r1 = x - hi.astype(F32)
    mid = r1.astype(BF16)
    lo = (r1 - mid.astype(F32)).astype(BF16)
    return _dot(tri_bf16, hi) + _dot(tri_bf16, mid) + _dot(tri_bf16, lo)


def _scan_masks(reverse):
    L = SCAN_CHUNK
    row = lax.broadcasted_iota(jnp.int32, (L, L), 0)
    col = lax.broadcasted_iota(jnp.int32, (L, L), 1)
    incl = (col >= row) if reverse else (col <= row)
    strict = (col > row) if reverse else (col < row)
    levels = []
    for lg in range(int(math.log2(L))):
        same_pair = jnp.right_shift(row, lg + 1) == jnp.right_shift(col, lg + 1)
        other_half = jnp.right_shift(row, lg) != jnp.right_shift(col, lg)
        levels.append(jnp.logical_and(jnp.logical_and(same_pair, other_half), strict))
    return incl, strict, row == col, levels


def _scan_operands(r_ref, v_ref, kk_ref, kd_ref, lw_ref, bb_ref, incl, reverse):
    L = SCAN_CHUNK
    lw = lw_ref[...]
    cum = _cumsum_rows(incl.astype(BF16), lw)
    ctot = cum[0:1, :] if reverse else cum[L - 1:L, :]
    g_inv = jnp.exp(-cum)
    g_end = jnp.exp(ctot - cum)
    kd = kd_ref[...].astype(F32)
    bb = bb_ref[...].astype(F32)
    r_f = r_ref[...].astype(F32) * jnp.exp(cum)
    return dict(a=(-kk_ref[...].astype(F32) * jnp.exp(cum - lw)).astype(BF16), b=(bb * g_inv).astype(BF16),
                k=(kd * g_inv).astype(BF16), r_f=r_f, r=r_f.astype(BF16), b_e=(bb * g_end).astype(BF16),
                k_e=(kd * g_end).astype(BF16), v=v_ref[...].astype(BF16), g_tot=jnp.exp(ctot))


def _scan_kernel(rf_ref, vf_ref, kkf_ref, kdf_ref, lwf_ref, bbf_ref, rb_ref, vb_ref, kkb_ref, kdb_ref, lwb_ref, bbb_ref,
                 yf_ref, yb_ref, s_ref):
    L = SCAN_CHUNK
    n = RW_HEAD_DIM

    @pl.when(pl.program_id(1) == 0)
    def _():
        s_ref[...] = jnp.zeros_like(s_ref)

    masks = (_scan_masks(False), _scan_masks(True))
    ops = (_scan_operands(rf_ref, vf_ref, kkf_ref, kdf_ref, lwf_ref, bbf_ref, masks[0][0], False),
           _scan_operands(rb_ref, vb_ref, kkb_ref, kdb_ref, lwb_ref, bbb_ref, masks[1][0], True))
    y_refs = (yf_ref, yb_ref)
    chains = [(d, h) for h in range(RW_HEADS) for d in range(2)]
    cut = lambda d, h, name: ops[d][name][:, h * n:(h + 1) * n]

    gram = [_dot_nt(jnp.concatenate([cut(d, h, "a"), cut(d, h, "r")], axis=0),
                    jnp.concatenate([cut(d, h, "b"), cut(d, h, "k")], axis=0)) for d, h in chains]
    n_ab = [g[:L, :L] for g in gram]
    lower = [jnp.concatenate([jnp.where(masks[d][1], g[:L, L:], 0.0), jnp.where(masks[d][0], g[L:, L:], 0.0)],
                             axis=0).astype(BF16) for (d, h), g in zip(chains, gram)]
    m_rb = [jnp.where(masks[d][0], g[L:, :L], 0.0).astype(BF16) for (d, h), g in zip(chains, gram)]
    nv = [_dot(lo, cut(d, h, "v")) for (d, h), lo in zip(chains, lower)]
    t_inv = [jnp.where(masks[d][2], 1.0, jnp.where(masks[d][3][0], nab, 0.0)) for (d, h), nab in zip(chains, n_ab)]
    for lv in range(1, len(masks[0][3])):
        tb = [t.astype(BF16) for t in t_inv]
        half = [_dot(t, jnp.where(masks[d][3][lv], nab, 0.0).astype(BF16)).astype(BF16)
                for (d, h), t, nab in zip(chains, tb, n_ab)]
        t_inv = [t + _dot(hf, t16) for t, hf, t16 in zip(t_inv, half, tb)]
    au = [_dot(t.astype(BF16), jnp.concatenate([cut(d, h, "a"), x[:L].astype(BF16)], axis=1)).astype(BF16)
          for (d, h), t, x in zip(chains, t_inv, nv)]
    ry = [_dot(m, x) for m, x in zip(m_rb, au)]
    pq = [_dot_tn(x, cut(d, h, "b_e")) for (d, h), x in zip(chains, au)]
    vk = [_dot_tn(cut(d, h, "v"), cut(d, h, "k_e")) for d, h in chains]
    for i, (d, h) in enumerate(chains):
        sl = slice(h * n, (h + 1) * n)
        r_hat = (ops[d]["r_f"][:, sl] + ry[i][:, :n]).astype(BF16)
        y_hat = ry[i][:, n:] + nv[i][L:]
        p_mat = jnp.where(masks[d][2], ops[d]["g_tot"][:, sl], 0.0) + pq[i][:n]
        q_mat = pq[i][n:] + vk[i]
        s0b = s_ref[d, h].astype(BF16)
        y_refs[d][:, sl] = _dot_nt(r_hat, s0b) + y_hat
        s_ref[d, h] = _dot(s0b, p_mat.astype(BF16)) + q_mat


def _rw_scan(r, v, kk, kd0, lw0, bb0, kd1, lw1, bb1, n_ctx):
    b, tt, c = r.shape
    L = SCAN_CHUNK
    nc = tt // L
    ncc = n_ctx // L
    fwd = pl.BlockSpec((None, L, c), lambda bi, i: (bi, i, 0))
    bwd = pl.BlockSpec((None, L, c), lambda bi, i: (bi, jnp.where(i < ncc, ncc - 1 - i, nc - 1 - (i - ncc)), 0))
    return pl.pallas_call(
        _scan_kernel,
        grid=(b, nc),
        in_specs=[fwd] * 6 + [bwd] * 6,
        out_specs=[fwd, bwd],
        out_shape=[jax.ShapeDtypeStruct((b, tt, c), F32)] * 2,
        scratch_shapes=[pltpu.VMEM((2, RW_HEADS, RW_HEAD_DIM, RW_HEAD_DIM), F32)],
        compiler_params=_cparams(("parallel", "arbitrary")),
        name="rwkv_scan",
    )(r, v, kk, kd0, lw0, bb0, r, v, kk, kd1, lw1, bb1)


def _qk_prep(x, gain, cos, sin, ones):
    xn = x * lax.rsqrt(_group_sum(x * x, ones) * (1.0 / DF_HEAD_DIM) + RMS_EPS) * gain
    lane = lax.broadcasted_iota(jnp.int32, x.shape, 1)
    quarter = DF_HEAD_DIM // 4
    rot = jnp.where(jnp.bitwise_and(lane, 2 * quarter - 1) < quarter, -pltpu.roll(xn, x.shape[1] - quarter, 1), pltpu.roll(xn, quarter, 1))
    return xn * cos + rot * sin


def _attn_kernel(lam_ref, q_ref, k_ref, v_ref, cosq_ref, sinq_ref, cosk_ref, sink_ref, qn_ref, kn_ref, sub_ref, ones_ref,
                 o_ref, kb_ref, vb_ref, *, n_ctx, tq, out_scale):
    e = DF_HEAD_DIM
    tt = k_ref.shape[0]
    qi = pl.program_id(2)
    ones = ones_ref[...]

    @pl.when(qi == 0)
    def _():
        kb_ref[...] = _qk_prep(k_ref[...], kn_ref[...], cosk_ref[...], sink_ref[...], ones).astype(BF16)
        vb_ref[...] = v_ref[...].astype(BF16)

    q = (_qk_prep(q_ref[...], qn_ref[...], cosq_ref[...], sinq_ref[...], ones)
         * (DF_HEAD_DIM ** -0.5 * math.log2(math.e))).astype(BF16)

    def attend(kv_len):
        k = kb_ref[:kv_len, :]
        v = vb_ref[:kv_len, :]
        outs = []
        for m in range(2):
            s = _dot_nt(q[:, m * e:(m + 1) * e], k[:, m * e:(m + 1) * e])
            p = jnp.exp2(s - jnp.max(s, axis=-1, keepdims=True))
            l = jnp.sum(p, axis=-1, keepdims=True)
            outs.append(_dot(p.astype(BF16), v) / l)
        o = outs[0] - lam_ref[0] * outs[1]
        o = o * lax.rsqrt(jnp.mean(o * o, axis=-1, keepdims=True) + RMS_EPS) * sub_ref[...] * out_scale
        o_ref[...] = o.astype(o_ref.dtype)

    if n_ctx > 0:
        pl.when(qi * tq < n_ctx)(lambda: attend(n_ctx))
        pl.when(qi * tq >= n_ctx)(lambda: attend(tt))
    else:
        attend(tt)


def _diff_attention(z, p, lam, lam_init, cos, sin, n_ctx):
    b, tt, _ = z.shape
    tq = ROW_TILE
    w = DF_V_DIM
    q0 = RW_COLS // w
    k0 = q0 + DF_HEADS
    v0 = k0 + DF_HEADS
    two = lambda g: jnp.tile(g, 2).reshape(1, w)
    full = lambda shape: pl.BlockSpec(shape, lambda bi, hi, i: (0,) * len(shape))
    return pl.pallas_call(
        functools.partial(_attn_kernel, n_ctx=n_ctx, tq=tq, out_scale=1.0 - lam_init),
        grid=(b, DF_HEADS, tt // tq),
        in_specs=[pl.BlockSpec(memory_space=pltpu.SMEM),
                  pl.BlockSpec((None, tq, w), lambda bi, hi, i: (bi, i, q0 + hi)),
                  pl.BlockSpec((None, tt, w), lambda bi, hi, i: (bi, 0, k0 + hi)),
                  pl.BlockSpec((None, tt, w), lambda bi, hi, i: (bi, 0, v0 + hi)),
                  pl.BlockSpec((tq, w), lambda bi, hi, i: (i, 0)),
                  pl.BlockSpec((tq, w), lambda bi, hi, i: (i, 0)),
                  full((tt, w)), full((tt, w)), full((1, w)), full((1, w)), full((1, w)), full((w, w))],
        out_specs=pl.BlockSpec((None, tq, w), lambda bi, hi, i: (bi, i, hi)),
        out_shape=jax.ShapeDtypeStruct((b, tt, DF_WIDTH), BF16),
        scratch_shapes=[pltpu.VMEM((tt, w), BF16), pltpu.VMEM((tt, w), BF16)],
        compiler_params=_cparams(("parallel", "parallel", "arbitrary")),
        name="diff_attention",
    )(lam.reshape(1).astype(F32), z, z, z, cos, sin, cos, sin, two(p["qn"]), two(p["kn"]),
      p["subln"].reshape(1, w), _group_ones(w, DF_HEAD_DIM))


def _mix_out_kernel(yf_ref, yb_ref, bonus_ref, gate_ref, ob_ref, gnw_ref, gnb_ref, seg_ref, w_ref, h_ref, mod_ref, o_ref):
    seg = seg_ref[...]
    y = yf_ref[...] + yb_ref[...]
    mu = _group_sum(y, seg) * (1.0 / RW_HEAD_DIM)
    d = y - mu
    var = _group_sum(d * d, seg) * (1.0 / RW_HEAD_DIM)
    yn = d * lax.rsqrt(var + RW_GN_EPS) * gnw_ref[...] + gnb_ref[...]
    ya = (yn + bonus_ref[...]) * gate_ref[...]
    mix = _dot(ya.astype(BF16), w_ref[:RW_WIDTH, :]) + _dot(ob_ref[...], w_ref[RW_WIDTH:, :])
    o_ref[...] = h_ref[...] + mod_ref[2:3, :] * mix


def _mix_out(yf, yb, bonus, gate, ob, p, h, mods, n_ctx):
    b, tt, d = h.shape
    c = RW_WIDTH
    tm = ROW_TILE
    tok = lambda width: pl.BlockSpec((None, tm, width), lambda bi, i: (bi, i, 0))
    full = lambda shape: pl.BlockSpec(shape, lambda bi, i: (0,) * len(shape))
    return pl.pallas_call(
        _mix_out_kernel,
        grid=(b, tt // tm),
        in_specs=[tok(c), tok(c), tok(c), tok(c), tok(DF_WIDTH), full((1, c)), full((1, c)), full((c, c)),
                  full((c + DF_WIDTH, d)), tok(d),
                  pl.BlockSpec((None, None, 6, d), lambda bi, i: (bi, _kind(i, n_ctx), 0, 0))],
        out_specs=tok(d),
        out_shape=jax.ShapeDtypeStruct((b, tt, d), F32),
        compiler_params=_cparams(("parallel", "parallel")),
        name="mix_out_residual",
    )(yf, yb, bonus, gate, ob, p["gn_w"].reshape(1, c), p["gn_b"].reshape(1, c), _group_ones(c, RW_HEAD_DIM),
      p["w_out"].astype(BF16), h, mods)


def _sgu_kernel(h_ref, mod_ref, g_ref, w_in_ref, b_in_ref, lng_ref, lnb_ref, ws_ref, bst_ref, w_out_ref, o_ref):
    x = h_ref[...]
    tm = x.shape[0]
    y = _norm_mod(x, g_ref[...], mod_ref[1:2, :], mod_ref[0:1, :])
    z = _dot(y.astype(BF16), w_in_ref[...]) + b_in_ref[...]
    z = 0.5 * z * (1.0 + lax.erf(z * (2.0 ** -0.5)))
    u = z[:, :SGU_WIDTH]
    v = z[:, SGU_WIDTH:]
    mu = jnp.mean(v, axis=-1, keepdims=True)
    var = jnp.mean(jnp.square(v - mu), axis=-1, keepdims=True)
    v = ((v - mu) * lax.rsqrt(var + LN_EPS) * lng_ref[...] + lnb_ref[...]).astype(BF16)
    rows = []
    for ci in range(tm // SGU_CHUNK):
        cols = []
        for gi in range(SGU_GROUPS):
            vb = v[ci * SGU_CHUNK:(ci + 1) * SGU_CHUNK, gi * SGU_GROUP_DIM:(gi + 1) * SGU_GROUP_DIM]
            cols.append(_dot(ws_ref[gi], vb) + bst_ref[:, gi:gi + 1])
        rows.append(jnp.concatenate(cols, axis=1))
    sv = jnp.concatenate(rows, axis=0)
    o_ref[...] = x + mod_ref[2:3, :] * _dot((u * sv).astype(BF16), w_out_ref[...])


def _sgu_layer(h, mods, g, p, n_ctx):
    b, tt, d = h.shape
    tm = ROW_TILE
    full = lambda shape: pl.BlockSpec(shape, lambda bi, i: (0,) * len(shape))
    return pl.pallas_call(
        _sgu_kernel,
        grid=(b, tt // tm),
        in_specs=[pl.BlockSpec((None, tm, d), lambda bi, i: (bi, i, 0)),
                  pl.BlockSpec((None, None, 6, d), lambda bi, i: (bi, _kind(i, n_ctx), 0, 0)),
                  full((1, d)), full((d, 2 * SGU_WIDTH)), full((1, 2 * SGU_WIDTH)),
                  full((1, SGU_WIDTH)), full((1, SGU_WIDTH)),
                  full((SGU_GROUPS, SGU_CHUNK, SGU_CHUNK)), full((SGU_CHUNK, SGU_GROUPS)),
                  full((SGU_WIDTH, d))],
        out_specs=pl.BlockSpec((None, tm, d), lambda bi, i: (bi, i, 0)),
        out_shape=jax.ShapeDtypeStruct((b, tt, d), F32),
        compiler_params=_cparams(("parallel", "parallel")),
        name="sgu_layer",
    )(h, mods, g.reshape(1, d), p["w_in"].astype(BF16), p["b_in"].reshape(1, -1),
      p["ln_g"].reshape(1, -1), p["ln_b"].reshape(1, -1), p["ws"].astype(BF16), p["bs"].T,
      p["w_out"].astype(BF16))


def _router_kernel(h_ref, mod_ref, g_ref, rwh_ref, rwl_ref, rb_ref, f_ref, idx_ref, wt_ref, rank_ref, cnt_ref, base_ref):
    first = jnp.logical_and(pl.program_id(0) == 0, pl.program_id(1) == 0)

    @pl.when(first)
    def _():
        base_ref[...] = jnp.zeros_like(base_ref)

    x = h_ref[...]
    tm = x.shape[0]
    f = _norm_mod(x, g_ref[...], mod_ref[4:5, :], mod_ref[3:4, :])
    fh = f.astype(BF16)
    fl = (f - fh.astype(F32)).astype(BF16)
    f_ref[...] = fh
    logits = _dot(fh, rwh_ref[...]) + _dot(fl, rwh_ref[...]) + _dot(fh, rwl_ref[...]) + rb_ref[...]
    lane = lax.broadcasted_iota(jnp.int32, (tm, LANES), 1)
    tri = (lax.broadcasted_iota(jnp.int32, (tm, tm), 1) < lax.broadcasted_iota(jnp.int32, (tm, tm), 0))
    vals, sels = [], []
    idx_out = jnp.zeros((tm, LANES), jnp.int32)
    l = logits
    for kk in range(TOP_K):
        m = jnp.max(l, axis=-1, keepdims=True)
        idx = jnp.min(jnp.where(l == m, lane, LANES), axis=-1, keepdims=True)
        sel = lane == idx
        l = jnp.where(sel, -jnp.inf, l)
        vals.append(m)
        sels.append(sel)
        idx_out = jnp.where(lane == kk, idx, idx_out)
    es = [jnp.exp(vv - vals[0]) for vv in vals]
    den = es[0] + es[1] + es[2] + es[3]
    wt_out = jnp.zeros((tm, LANES), F32)
    for kk in range(TOP_K):
        wt_out = jnp.where(lane == kk, es[kk] / den, wt_out)
    onehot = jnp.zeros((tm, LANES), F32)
    for sel in sels:
        onehot = onehot + sel.astype(F32)
    ahead = _dot(tri.astype(BF16), onehot.astype(BF16)) + base_ref[...]
    rank_out = jnp.zeros((tm, LANES), jnp.int32)
    for kk in range(TOP_K):
        rk = jnp.sum(jnp.where(sels[kk], ahead, 0.0), axis=-1, keepdims=True).astype(jnp.int32)
        rank_out = jnp.where(lane == kk, rk, rank_out)
    idx_ref[...] = idx_out
    wt_ref[...] = wt_out
    rank_ref[...] = rank_out
    base_ref[...] = base_ref[...] + jnp.sum(onehot, axis=0, keepdims=True)
    cnt_ref[...] = base_ref[...]


def _router(h, mods, g, rt_w, rt_b, n_ctx, b0, b):
    _, tt, d = h.shape
    tm = ROW_TILE
    rw = jnp.zeros((d, LANES), F32).at[:, :N_EXPERTS].set(rt_w)
    rw_hi = rw.astype(BF16)
    rw_lo = (rw - rw_hi.astype(F32)).astype(BF16)
    rb = jnp.full((1, LANES), -jnp.inf, F32).at[0, :N_EXPERTS].set(rt_b)
    tok = lambda width: pl.BlockSpec((None, tm, width), lambda bi, i: (bi, i, 0))
    return pl.pallas_call(
        _router_kernel,
        grid=(b, tt // tm),
        in_specs=[pl.BlockSpec((None, tm, d), lambda bi, i: (b0 + bi, i, 0)),
                  pl.BlockSpec((None, None, 6, d), lambda bi, i: (b0 + bi, _kind(i, n_ctx), 0, 0)),
                  pl.BlockSpec((1, d), lambda bi, i: (0, 0)),
                  pl.BlockSpec((d, LANES), lambda bi, i: (0, 0)),
                  pl.BlockSpec((d, LANES), lambda bi, i: (0, 0)),
                  pl.BlockSpec((1, LANES), lambda bi, i: (0, 0))],
        out_specs=[tok(d), tok(LANES), tok(LANES), tok(LANES), pl.BlockSpec((1, LANES), lambda bi, i: (0, 0))],
        out_shape=[jax.ShapeDtypeStruct((b, tt, d), BF16),
                   jax.ShapeDtypeStruct((b, tt, LANES), jnp.int32),
                   jax.ShapeDtypeStruct((b, tt, LANES), F32),
                   jax.ShapeDtypeStruct((b, tt, LANES), jnp.int32),
                   jax.ShapeDtypeStruct((1, LANES), F32)],
        scratch_shapes=[pltpu.VMEM((1, LANES), F32)],
        compiler_params=_cparams(("arbitrary", "arbitrary")),
        name="moe_router",
    )(h, mods, g.reshape(1, d), rw_hi, rw_lo, rb)


def _expert_kernel(te_ref, nv_ref, x_ref, w1_ref, b1_ref, w2_ref, b2_ref, o_ref, w1b_ref, w2b_ref):
    j = pl.program_id(0)
    active = j < nv_ref[0]
    new_expert = jnp.logical_or(j == 0, te_ref[j] != te_ref[jnp.maximum(j - 1, 0)])

    @pl.when(jnp.logical_and(active, new_expert))
    def _():
        w1b_ref[...] = w1_ref[...].astype(BF16)
        w2b_ref[...] = w2_ref[...].astype(BF16)

    @pl.when(active)
    def _():
        f = w2b_ref.shape[0]
        fh = f // 2
        x = x_ref[...]
        y = b2_ref[...]
        for c in range(2):
            lo, hi = c * fh, (c + 1) * fh
            glu = _dot(x, w1b_ref[:, lo:hi]) + b1_ref[:, lo:hi]
            lin = _dot(x, w1b_ref[:, f + lo:f + hi]) + b1_ref[:, f + lo:f + hi]
            glu = jnp.minimum(glu, SWIGLU_LIMIT)
            lin = jnp.clip(lin, -SWIGLU_LIMIT, SWIGLU_LIMIT)
            act = glu * jax.nn.sigmoid(SWIGLU_ALPHA * glu) * (lin + 1.0)
            y = y + _dot(act.astype(BF16), w2b_ref[lo:hi, :])
        o_ref[...] = y.astype(o_ref.dtype)

    @pl.when(jnp.logical_not(active))
    def _():
        o_ref[...] = jnp.zeros_like(o_ref)


def _experts(xs, tile_expert, n_valid, w1, b1, w2, b2, layer):
    s, d = xs.shape
    depth, e, _, f2 = w1.shape
    f = f2 // 2
    tm = EXPERT_ROW_TILE
    grid_spec = pltpu.PrefetchScalarGridSpec(
        num_scalar_prefetch=2,
        grid=(s // tm,),
        in_specs=[pl.BlockSpec((tm, d), lambda j, te, nv: (j, 0)),
                  pl.BlockSpec((None, None, d, f2), lambda j, te, nv: (layer, te[j], 0, 0)),
                  pl.BlockSpec((None, None, 1, f2), lambda j, te, nv: (layer, te[j], 0, 0)),
                  pl.BlockSpec((None, None, f, d), lambda j, te, nv: (layer, te[j], 0, 0)),
                  pl.BlockSpec((None, None, 1, d), lambda j, te, nv: (layer, te[j], 0, 0))],
        out_specs=pl.BlockSpec((tm, d), lambda j, te, nv: (j, 0)),
        scratch_shapes=[pltpu.VMEM((d, f2), BF16), pltpu.VMEM((f, d), BF16)],
    )
    return pl.pallas_call(
        _expert_kernel,
        grid_spec=grid_spec,
        out_shape=jax.ShapeDtypeStruct((s, d), BF16),
        compiler_params=_cparams(("arbitrary",)),
        name="moe_experts",
    )(tile_expert, n_valid, xs, w1, b1.reshape(depth, e, 1, f2), w2, b2.reshape(depth, e, 1, d))


def _combine_kernel(p_ref, wt_ref, h_ref, mod_ref, o_ref):
    tm = h_ref.shape[0]
    row = lax.broadcasted_iota(jnp.int32, (tm, TOP_K * tm), 0)
    col = lax.broadcasted_iota(jnp.int32, (tm, TOP_K * tm), 1)
    k_of = col - TOP_K * row
    g = jnp.zeros((tm, TOP_K * tm), F32)
    for k in range(TOP_K):
        g = jnp.where(k_of == k, wt_ref[:, k:k + 1], g)
    g_hi = g.astype(BF16)
    g_lo = (g - g_hi.astype(F32)).astype(BF16)
    p = p_ref[...]
    o_ref[...] = h_ref[...] + mod_ref[5:6, :] * (_dot(g_hi, p) + _dot(g_lo, p))


def _combine(picked, wt, h, mods, n_ctx, b0, b):
    _, tt, d = h.shape
    tm = ROW_TILE
    return pl.pallas_call(
        _combine_kernel,
        grid=(b, tt // tm),
        in_specs=[pl.BlockSpec((None, TOP_K * tm, d), lambda bi, i: (bi, i, 0)),
                  pl.BlockSpec((None, tm, LANES), lambda bi, i: (bi, i, 0)),
                  pl.BlockSpec((None, tm, d), lambda bi, i: (b0 + bi, i, 0)),
                  pl.BlockSpec((None, None, 6, d), lambda bi, i: (b0 + bi, _kind(i, n_ctx), 0, 0))],
        out_specs=pl.BlockSpec((None, tm, d), lambda bi, i: (bi, i, 0)),
        out_shape=jax.ShapeDtypeStruct((b, tt, d), F32),
        compiler_params=_cparams(("parallel", "parallel")),
        name="moe_combine",
    )(picked, wt, h, mods)


def _moe(h, mods, g, rt_w, rt_b, w1, b1, w2, b2, layer, n_ctx, b0, b):
    _, tt, d = h.shape
    n = b * tt
    tm = EXPERT_ROW_TILE
    f, idx, wt, rank, cnt = _router(h, mods, g, rt_w, rt_b, n_ctx, b0, b)
    top_i = idx.reshape(n, LANES)[:, :TOP_K]
    rank = rank.reshape(n, LANES)[:, :TOP_K]
    counts = cnt[0, :N_EXPERTS].astype(jnp.int32)
    padded = ((counts + tm - 1) // tm) * tm
    p_end = jnp.cumsum(padded)
    p_off = p_end - padded
    u_off = jnp.cumsum(counts) - counts
    slot = (p_off[top_i] + rank).reshape(-1)
    n_tiles = (n * TOP_K) // tm + N_EXPERTS
    s_rows = n_tiles * tm
    _, sorted_pair = lax.sort_key_val(slot, jnp.arange(n * TOP_K, dtype=jnp.int32))
    srow = jnp.arange(s_rows, dtype=jnp.int32)
    expert_of = lambda rows: jnp.minimum(jnp.sum((rows[:, None] >= p_end[None, :]).astype(jnp.int32), axis=1),
                                         N_EXPERTS - 1)
    row_e = expert_of(srow)
    within = srow - p_off[row_e]
    valid = within < counts[row_e]
    src = jnp.clip(u_off[row_e] + within, 0, n * TOP_K - 1)
    pair = sorted_pair[src]
    row_token = jnp.where(valid, pair // TOP_K, 0)
    n_valid = (p_end[-1] // tm).astype(jnp.int32).reshape(1)
    tile_start = jnp.arange(n_tiles, dtype=jnp.int32) * tm
    tile_expert = expert_of(tile_start)
    last_e = tile_expert[jnp.maximum(n_valid[0] - 1, 0)]
    tile_expert = jnp.where(tile_start < p_end[-1], tile_expert, last_e)
    xs = jnp.take(f.reshape(n, d), row_token, axis=0, mode="clip")
    ys = _experts(xs, tile_expert, n_valid, w1, b1, w2, b2, layer)
    picked = jnp.take(ys, slot, axis=0, mode="clip").reshape(b, tt * TOP_K, d)
    return _combine(picked, wt, h, mods, n_ctx, b0, b)


def _rope_tables(n_tokens, n_ctx):
    rows = n_tokens // GRID_W
    row = jnp.repeat(jnp.arange(rows, dtype=F32), GRID_W)
    col = jnp.tile(jnp.arange(GRID_W, dtype=F32), rows)
    axis_dim = DF_HEAD_DIM // 2
    inv_freq = ROPE_THETA ** (-jnp.arange(0, axis_dim, 2, dtype=F32) / axis_dim)
    ar = row[:, None] * inv_freq
    ac = col[:, None] * inv_freq
    ang = jnp.concatenate([ar, ar, ac, ac], axis=-1)
    cos = jnp.concatenate([jnp.ones((n_ctx, DF_HEAD_DIM), F32), jnp.cos(ang)], axis=0)
    sin = jnp.concatenate([jnp.zeros((n_ctx, DF_HEAD_DIM), F32), jnp.sin(ang)], axis=0)
    return jnp.tile(cos, (1, 2)), jnp.tile(sin, (1, 2))


def _even_layer(h, mods, g1, p, li, n_ctx, rope):
    z = _nm_mm(h, mods, g1, p["w_in"].astype(BF16), n_ctx)
    r, v, kk, kd0, kd1, lw0, lw1, bb0, bb1, gate, bonus = _rw_features(z, p, n_ctx)
    yf, yb = _rw_scan(r, v, kk, kd0, lw0, bb0, kd1, lw1, bb1, n_ctx)
    lam_init = 0.8 - 0.6 * math.exp(-0.3 * li)
    lp = p["lam"].astype(F32)
    lam = jnp.exp(jnp.sum(lp[0] * lp[1])) - jnp.exp(jnp.sum(lp[2] * lp[3])) + lam_init
    ob = _diff_attention(z, p, lam, lam_init, rope[0], rope[1], n_ctx)
    return _mix_out(yf, yb, bonus, gate, ob, p, h, mods, n_ctx)


def kernel(x, c, ctx, c_ctx, norm1_g, norm2_g, ada_w, ada_b, ev_w_in, ev_w_out, rw_mu, rw_w0, rw_w2, rw_a0, rw_a2, rw_g2, rw_kk, rw_ka, rw_rk, rw_gn_w, rw_gn_b, df_qn, df_kn, df_lam, df_subln, sg_w_in, sg_b_in, sg_ln_g, sg_ln_b, sg_ws, sg_bs, sg_w_out, rt_w, rt_b, ex_w1, ex_b1, ex_w2, ex_b2):
    b, t, d = x.shape
    n_ctx = ctx.shape[1]
    depth = ada_w.shape[0]
    assert d == D_MODEL and t % ROW_TILE == 0 and n_ctx % ROW_TILE == 0 and t % GRID_W == 0
    rope = _rope_tables(t, n_ctx)
    r_pad = -(-(b + 1) // 8) * 8
    cs = jnp.zeros((r_pad, d), F32).at[:b].set(c).at[b].set(c_ctx)
    ada = _ada_table(cs, ada_w, ada_b).reshape(depth, r_pad, 6, d)
    h = jnp.concatenate([ctx, x], axis=1)
    for l in range(depth):
        ctx_out = any(m % 2 == 0 for m in range(l + 1, depth))
        j = l // 2
        mods = jnp.stack([jnp.broadcast_to(ada[l, b], (b, 6, d)), ada[l, :b]], axis=1)
        if l % 2 == 0:
            p = {"w_in": ev_w_in[j], "w_out": ev_w_out[j], "mu": rw_mu[j], "w0": rw_w0[j], "w2": rw_w2[j],
                 "a0": rw_a0[j], "a2": rw_a2[j], "g2": rw_g2[j], "k_k": rw_kk[j], "k_a": rw_ka[j],
                 "r_k": rw_rk[j], "gn_w": rw_gn_w[j], "gn_b": rw_gn_b[j], "qn": df_qn[j], "kn": df_kn[j],
                 "lam": df_lam[j], "subln": df_subln[j]}
            h = _even_layer(h, mods, norm1_g[l], p, l, n_ctx, rope)
        else:
            p = {"w_in": sg_w_in[j], "b_in": sg_b_in[j], "ln_g": sg_ln_g[j], "ln_b": sg_ln_b[j],
                 "ws": sg_ws[j], "bs": sg_bs[j], "w_out": sg_w_out[j]}
            h = _sgu_layer(h, mods, norm1_g[l], p, n_ctx)
        if n_ctx > 0 and not ctx_out:
            h = h[:, n_ctx:]
            n_ctx = 0
        h = _moe(h, mods, norm2_g[l], rt_w[l], rt_b[l], ex_w1, ex_b1, ex_w2, ex_b2, l, n_ctx, 0, b)
    return h[:, n_ctx:]
```

```python
import functools
import math

import jax
import jax.numpy as jnp
from jax import lax
from jax.experimental import pallas as pl
from jax.experimental.pallas import tpu as pltpu

F32 = jnp.float32
BF16 = jnp.bfloat16
HIGHEST = lax.Precision.HIGHEST

D_MODEL = 1024
GRID_W = 64
RMS_EPS = 1e-6
LN_EPS = 1e-5

RW_HEAD_DIM = 64
RW_WIDTH = 512
RW_HEADS = RW_WIDTH // RW_HEAD_DIM
DECAY_LORA = 64
ICLR_LORA = 64
GATE_LORA = 128
RW_COLS = 3 * RW_WIDTH + 2 * DECAY_LORA + 2 * ICLR_LORA + GATE_LORA
RW_GN_EPS = 64e-5

DF_HEAD_DIM = 64
DF_V_DIM = 128
DF_WIDTH = 512
DF_HEADS = DF_WIDTH // DF_V_DIM
DF_QK = DF_HEADS * 2 * DF_HEAD_DIM
ROPE_THETA = 10000.0
EVEN_COLS = RW_COLS + 2 * DF_QK + DF_WIDTH

SGU_CHUNK = 128
SGU_WIDTH = 1024
SGU_GROUPS = 8
SGU_GROUP_DIM = SGU_WIDTH // SGU_GROUPS

N_EXPERTS = 32
TOP_K = 4
SWIGLU_LIMIT = 7.0
SWIGLU_ALPHA = 1.702

LANES = 128
ROW_TILE = 256
SCAN_CHUNK = 64
EXPERT_ROW_TILE = 512
MOE_GROUPS = 2
VMEM_LIMIT = 56 * 1024 * 1024


def _cparams(sem):
    return pltpu.CompilerParams(dimension_semantics=sem, vmem_limit_bytes=VMEM_LIMIT)


def _dot(a, b, precision=None):
    return jnp.dot(a, b, preferred_element_type=F32, precision=precision)


def _dot_nt(a, b, precision=None):
    return lax.dot_general(a, b, (((1,), (1,)), ((), ())), preferred_element_type=F32, precision=precision)


def _dot_tn(a, b, precision=None):
    return lax.dot_general(a, b, (((0,), (0,)), ((), ())), preferred_element_type=F32, precision=precision)


def _norm_mod(x, g, scale, shift):
    ms = jnp.mean(x * x, axis=-1, keepdims=True)
    return (x * lax.rsqrt(ms + RMS_EPS) * g) * (1.0 + scale) + shift


def _ada_kernel(x_ref, w_ref, b_ref, o_ref):
    x = x_ref[...]
    s = x * jax.nn.sigmoid(x)
    o_ref[...] = _dot(s, w_ref[...], HIGHEST) + b_ref[...]


def _ada_table(cs, ada_w, ada_b):
    depth, d, n = ada_w.shape
    r = cs.shape[0]
    tn = 1024
    return pl.pallas_call(
        _ada_kernel,
        grid=(depth, n // tn),
        in_specs=[pl.BlockSpec((r, d), lambda l, j: (0, 0)),
                  pl.BlockSpec((None, d, tn), lambda l, j: (l, 0, j)),
                  pl.BlockSpec((None, 1, tn), lambda l, j: (l, 0, j))],
        out_specs=pl.BlockSpec((None, r, tn), lambda l, j: (l, 0, j)),
        out_shape=jax.ShapeDtypeStruct((depth, r, n), F32),
        compiler_params=_cparams(("parallel", "parallel")),
        name="ada_table",
    )(cs, ada_w, ada_b.reshape(depth, 1, n))


def _kind(i, n_ctx):
    return (i * ROW_TILE >= n_ctx).astype(jnp.int32) if n_ctx > 0 else 1


def _nm_mm_kernel(h_ref, mod_ref, g_ref, w_ref, o_ref):
    y = _norm_mod(h_ref[...], g_ref[...], mod_ref[1:2, :], mod_ref[0:1, :])
    o_ref[...] = _dot(y.astype(BF16), w_ref[...]).astype(o_ref.dtype)


def _nm_mm(h, mods, g, w, n_ctx, out_dtype=F32):
    b, tt, d = h.shape
    n = w.shape[1]
    tm = ROW_TILE
    return pl.pallas_call(
        _nm_mm_kernel,
        grid=(b, tt // tm),
        in_specs=[pl.BlockSpec((None, tm, d), lambda bi, i: (bi, i, 0)),
                  pl.BlockSpec((None, None, 6, d), lambda bi, i: (bi, _kind(i, n_ctx), 0, 0)),
                  pl.BlockSpec((1, d), lambda bi, i: (0, 0)),
                  pl.BlockSpec((d, n), lambda bi, i: (0, 0))],
        out_specs=pl.BlockSpec((None, tm, n), lambda bi, i: (bi, i, 0)),
        out_shape=jax.ShapeDtypeStruct((b, tt, n), out_dtype),
        compiler_params=_cparams(("parallel", "parallel")),
        name="norm_mod_proj",
    )(h, mods, g.reshape(1, d), w)


def _group_ones(width, group):
    idx = jnp.arange(width) // group
    return (idx[:, None] == idx[None, :]).astype(BF16)


def _split_hi_lo(w):
    hi = w.astype(BF16)
    return jnp.stack([hi, (w - hi.astype(F32)).astype(BF16)])


def _dot3(x, w_hi, w_lo):
    xh = x.astype(BF16)
    xl = (x - xh.astype(F32)).astype(BF16)
    return _dot(xh, w_hi) + _dot(xl, w_hi) + _dot(xh, w_lo)


def _group_sum(x, ones_bf16):
    hi = x.astype(BF16)
    lo = (x - hi.astype(F32)).astype(BF16)
    return _dot(hi, ones_bf16) + _dot(lo, ones_bf16)


def _softplus(x):
    return jnp.maximum(x, 0.0) + jnp.log1p(jnp.exp(-jnp.abs(x)))


def _rw_feat_kernel(z_ref, zprev_ref, znext_ref, mu_ref, w0_ref, w2_ref, a0_ref, a2_ref, g2_ref, kk_ref, ka_ref,
                    rk_ref, seg_ref,
                    r_ref, v_ref, kko_ref, kd0_ref, kd1_ref, lw0_ref, lw1_ref, bb0_ref, bb1_ref, g_ref, bonus_ref,
                    *, n_ctx, tt):
    c = RW_WIDTH
    tm = z_ref.shape[0]
    i = pl.program_id(1)
    z = z_ref[...]
    seq_first = jnp.logical_or(i * tm == 0, i * tm == n_ctx)
    seq_last = jnp.logical_or((i + 1) * tm == n_ctx, (i + 1) * tm == tt)
    prev_row = jnp.where(seq_first, 0.0, zprev_ref[7:8, :])
    next_row = jnp.where(seq_last, 0.0, znext_ref[0:1, :])
    rows = lax.broadcasted_iota(jnp.int32, z.shape, 0)
    zp = jnp.where(rows == 0, prev_row, pltpu.roll(z, 1, 0))
    zn = jnp.where(rows == tm - 1, next_row, pltpu.roll(z, tm - 1, 0))
    za = z + mu_ref[0:1, :] * (zp - z) + mu_ref[1:2, :] * (zn - z)
    r = za[:, :c]
    k = za[:, c:2 * c]
    v = za[:, 2 * c:3 * c]
    wl = za[:, 3 * c:3 * c + 2 * DECAY_LORA]
    al = za[:, 3 * c + 2 * DECAY_LORA:3 * c + 2 * DECAY_LORA + 2 * ICLR_LORA]
    gl = za[:, 3 * c + 2 * DECAY_LORA + 2 * ICLR_LORA:]
    r_ref[...] = r.astype(r_ref.dtype)
    v_ref[...] = v.astype(v_ref.dtype)
    twl = jnp.tanh(wl)
    seg = seg_ref[...]
    kkr = k * kk_ref[...]
    kk = kkr / jnp.maximum(jnp.sqrt(_group_sum(kkr * kkr, seg)), 1e-12)
    kko_ref[...] = kk.astype(kko_ref.dtype)
    lw_refs = (lw0_ref, lw1_ref)
    bb_refs = (bb0_ref, bb1_ref)
    kd_refs = (kd0_ref, kd1_ref)
    kd_sum = None
    for d in range(2):
        wd = w0_ref[d:d + 1, :] + _dot3(twl[:, d * DECAY_LORA:(d + 1) * DECAY_LORA], w2_ref[0, d], w2_ref[1, d])
        w = -_softplus(-wd) - 0.5
        lw_refs[d][...] = -jnp.exp(w)
        a = jax.nn.sigmoid(a0_ref[d:d + 1, :] + _dot3(al[:, d * ICLR_LORA:(d + 1) * ICLR_LORA], a2_ref[0, d], a2_ref[1, d]))
        bb_refs[d][...] = (kk * a).astype(bb_refs[d].dtype)
        kd = k * (1.0 + (a - 1.0) * ka_ref[...])
        kd_refs[d][...] = kd.astype(kd_refs[d].dtype)
        kd_sum = kd if kd_sum is None else kd_sum + kd
    g_ref[...] = _dot3(jax.nn.sigmoid(gl), g2_ref[0], g2_ref[1])
    bonus_ref[...] = _group_sum(r * kd_sum * rk_ref[...], seg) * v


def _rw_features(z, p, n_ctx):
    b, tt, _ = z.shape
    c = RW_WIDTH
    tm = ROW_TILE
    hb = tm // 8
    last_halo = tt // 8 - 1
    full = lambda shape: pl.BlockSpec(shape, lambda bi, i: (0,) * len(shape))
    out_spec = pl.BlockSpec((None, tm, c), lambda bi, i: (bi, i, 0))
    return pl.pallas_call(
        functools.partial(_rw_feat_kernel, n_ctx=n_ctx, tt=tt),
        grid=(b, tt // tm),
        in_specs=[pl.BlockSpec((None, tm, RW_COLS), lambda bi, i: (bi, i, 0)),
                  pl.BlockSpec((None, 8, RW_COLS), lambda bi, i: (bi, jnp.maximum(i * hb - 1, 0), 0)),
                  pl.BlockSpec((None, 8, RW_COLS), lambda bi, i: (bi, jnp.minimum((i + 1) * hb, last_halo), 0)),
                  full((2, RW_COLS)), full((2, c)), full((2, 2, DECAY_LORA, c)), full((2, c)), full((2, 2, ICLR_LORA, c)),
                  full((2, GATE_LORA, c)), full((1, c)), full((1, c)), full((1, c)), full((c, c))],
        out_specs=[out_spec] * 11,
        out_shape=[jax.ShapeDtypeStruct((b, tt, c), dt) for dt in (BF16,) * 5 + (F32, F32, BF16, BF16, F32, F32)],
        compiler_params=_cparams(("parallel", "parallel")),
        name="rwkv_features",
    )(z, z, z, p["mu"], p["w0"], _split_hi_lo(p["w2"]), p["a0"], _split_hi_lo(p["a2"]), _split_hi_lo(p["g2"]), p["k_k"].reshape(1, c), p["k_a"].reshape(1, c),
      p["r_k"].reshape(1, c), _group_ones(c, RW_HEAD_DIM))


def _cumsum_rows(tri_bf16, x):
    hi = x.astype(BF16)
    r1 = x - hi.astype(F32)
    mid = r1.astype(BF16)
    lo = (r1 - mid.astype(F32)).astype(BF16)
    return _dot(tri_bf16, hi) + _dot(tri_bf16, mid) + _dot(tri_bf16, lo)


def _scan_masks(reverse):
    L = SCAN_CHUNK
    row = lax.broadcasted_iota(jnp.int32, (L, L), 0)
    col = lax.broadcasted_iota(jnp.int32, (L, L), 1)
    incl = (col >= row) if reverse else (col <= row)
    strict = (col > row) if reverse else (col < row)
    levels = []
    for lg in range(int(math.log2(L))):
        same_pair = jnp.right_shift(row, lg + 1) == jnp.right_shift(col, lg + 1)
        other_half = jnp.right_shift(row, lg) != jnp.right_shift(col, lg)
        levels.append(jnp.logical_and(jnp.logical_and(same_pair, other_half), strict))
    return incl, strict, row == col, levels


def _scan_operands(r_ref, v_ref, kk_ref, kd_ref, lw_ref, bb_ref, incl, reverse):
    L = SCAN_CHUNK
    lw = lw_ref[...]
    cum = _cumsum_rows(incl.astype(BF16), lw)
    ctot = cum[0:1, :] if reverse else cum[L - 1:L, :]
    g_inv = jnp.exp(-cum)
    g_end = jnp.exp(ctot - cum)
    kd = kd_ref[...].astype(F32)
    bb = bb_ref[...].astype(F32)
    r_f = r_ref[...].astype(F32) * jnp.exp(cum)
    return dict(a=(-kk_ref[...].astype(F32) * jnp.exp(cum - lw)).astype(BF16), b=(bb * g_inv).astype(BF16),
                k=(kd * g_inv).astype(BF16), r_f=r_f, r=r_f.astype(BF16), b_e=(bb * g_end).astype(BF16),
                k_e=(kd * g_end).astype(BF16), v=v_ref[...].astype(BF16), g_tot=jnp.exp(ctot))


def _scan_kernel(rf_ref, vf_ref, kkf_ref, kdf_ref, lwf_ref, bbf_ref, rb_ref, vb_ref, kkb_ref, kdb_ref, lwb_ref, bbb_ref,
                 yf_ref, yb_ref, s_ref):
    L = SCAN_CHUNK
    n = RW_HEAD_DIM

    @pl.when(pl.program_id(1) == 0)
    def _():
        s_ref[...] = jnp.zeros_like(s_ref)

    masks = (_scan_masks(False), _scan_masks(True))
    ops = (_scan_operands(rf_ref, vf_ref, kkf_ref, kdf_ref, lwf_ref, bbf_ref, masks[0][0], False),
           _scan_operands(rb_ref, vb_ref, kkb_ref, kdb_ref, lwb_ref, bbb_ref, masks[1][0], True))
    y_refs = (yf_ref, yb_ref)
    chains = [(d, h) for h in range(RW_HEADS) for d in range(2)]
    cut = lambda d, h, name: ops[d][name][:, h * n:(h + 1) * n]

    gram = [_dot_nt(jnp.concatenate([cut(d, h, "a"), cut(d, h, "r")], axis=0),
                    jnp.concatenate([cut(d, h, "b"), cut(d, h, "k")], axis=0)) for d, h in chains]
    n_ab = [g[:L, :L] for g in gram]
    lower = [jnp.concatenate([jnp.where(masks[d][1], g[:L, L:], 0.0), jnp.where(masks[d][0], g[L:, L:], 0.0)],
                             axis=0).astype(BF16) for (d, h), g in zip(chains, gram)]
    m_rb = [jnp.where(masks[d][0], g[L:, :L], 0.0).astype(BF16) for (d, h), g in zip(chains, gram)]
    nv = [_dot(lo, cut(d, h, "v")) for (d, h), lo in zip(chains, lower)]
    t_inv = [jnp.where(masks[d][2], 1.0, jnp.where(masks[d][3][0], nab, 0.0)) for (d, h), nab in zip(chains, n_ab)]
    for lv in range(1, len(masks[0][3])):
        tb = [t.astype(BF16) for t in t_inv]
        half = [_dot(t, jnp.where(masks[d][3][lv], nab, 0.0).astype(BF16)).astype(BF16)
                for (d, h), t, nab in zip(chains, tb, n_ab)]
        t_inv = [t + _dot(hf, t16) for t, hf, t16 in zip(t_inv, half, tb)]
    au = [_dot(t.astype(BF16), jnp.concatenate([cut(d, h, "a"), x[:L].astype(BF16)], axis=1)).astype(BF16)
          for (d, h), t, x in zip(chains, t_inv, nv)]
    ry = [_dot(m, x) for m, x in zip(m_rb, au)]
    pq = [_dot_tn(x, cut(d, h, "b_e")) for (d, h), x in zip(chains, au)]
    vk = [_dot_tn(cut(d, h, "v"), cut(d, h, "k_e")) for d, h in chains]
    for i, (d, h) in enumerate(chains):
        sl = slice(h * n, (h + 1) * n)
        r_hat = (ops[d]["r_f"][:, sl] + ry[i][:, :n]).astype(BF16)
        y_hat = ry[i][:, n:] + nv[i][L:]
        p_mat = jnp.where(masks[d][2], ops[d]["g_tot"][:, sl], 0.0) + pq[i][:n]
        q_mat = pq[i][n:] + vk[i]
        s0b = s_ref[d, h].astype(BF16)
        y_refs[d][:, sl] = _dot_nt(r_hat, s0b) + y_hat
        s_ref[d, h] = _dot(s0b, p_mat.astype(BF16)) + q_mat


def _rw_scan(r, v, kk, kd0, lw0, bb0, kd1, lw1, bb1, n_ctx):
    b, tt, c = r.shape
    L = SCAN_CHUNK
    nc = tt // L
    ncc = n_ctx // L
    fwd = pl.BlockSpec((None, L, c), lambda bi, i: (bi, i, 0))
    bwd = pl.BlockSpec((None, L, c), lambda bi, i: (bi, jnp.where(i < ncc, ncc - 1 - i, nc - 1 - (i - ncc)), 0))
    return pl.pallas_call(
        _scan_kernel,
        grid=(b, nc),
        in_specs=[fwd] * 6 + [bwd] * 6,
        out_specs=[fwd, bwd],
        out_shape=[jax.ShapeDtypeStruct((b, tt, c), F32)] * 2,
        scratch_shapes=[pltpu.VMEM((2, RW_HEADS, RW_HEAD_DIM, RW_HEAD_DIM), F32)],
        compiler_params=_cparams(("parallel", "arbitrary")),
        name="rwkv_scan",
    )(r, v, kk, kd0, lw0, bb0, r, v, kk, kd1, lw1, bb1)


def _qk_prep(x, gain, cos, sin, ones):
    xn = x * lax.rsqrt(_group_sum(x * x, ones) * (1.0 / DF_HEAD_DIM) + RMS_EPS) * gain
    lane = lax.broadcasted_iota(jnp.int32, x.shape, 1)
    quarter = DF_HEAD_DIM // 4
    rot = jnp.where(jnp.bitwise_and(lane, 2 * quarter - 1) < quarter, -pltpu.roll(xn, x.shape[1] - quarter, 1), pltpu.roll(xn, quarter, 1))
    return xn * cos + rot * sin


def _attn_kernel(lam_ref, q_ref, k_ref, v_ref, cosq_ref, sinq_ref, cosk_ref, sink_ref, qn_ref, kn_ref, sub_ref, ones_ref,
                 o_ref, kb_ref, vb_ref, *, n_ctx, tq, out_scale):
    e = DF_HEAD_DIM
    tt = k_ref.shape[0]
    qi = pl.program_id(2)
    ones = ones_ref[...]

    @pl.when(qi == 0)
    def _():
        kb_ref[...] = _qk_prep(k_ref[...], kn_ref[...], cosk_ref[...], sink_ref[...], ones).astype(BF16)
        vb_ref[...] = v_ref[...].astype(BF16)

    q = (_qk_prep(q_ref[...], qn_ref[...], cosq_ref[...], sinq_ref[...], ones)
         * (DF_HEAD_DIM ** -0.5 * math.log2(math.e))).astype(BF16)

    def attend(kv_len):
        k = kb_ref[:kv_len, :]
        v = vb_ref[:kv_len, :]
        outs = []
        for m in range(2):
            s = _dot_nt(q[:, m * e:(m + 1) * e], k[:, m * e:(m + 1) * e])
            p = jnp.exp2(s - jnp.max(s, axis=-1, keepdims=True))
            l = jnp.sum(p, axis=-1, keepdims=True)
            outs.append(_dot(p.astype(BF16), v) / l)
        o = outs[0] - lam_ref[0] * outs[1]
        o = o * lax.rsqrt(jnp.mean(o * o, axis=-1, keepdims=True) + RMS_EPS) * sub_ref[...] * out_scale
        o_ref[...] = o.astype(o_ref.dtype)

    if n_ctx > 0:
        pl.when(qi * tq < n_ctx)(lambda: attend(n_ctx))
        pl.when(qi * tq >= n_ctx)(lambda: attend(tt))
    else:
        attend(tt)


def _diff_attention(z, p, lam, lam_init, cos, sin, n_ctx):
    b, tt, _ = z.shape
    tq = ROW_TILE
    w = DF_V_DIM
    q0 = RW_COLS // w
    k0 = q0 + DF_HEADS
    v0 = k0 + DF_HEADS
    two = lambda g: jnp.tile(g, 2).reshape(1, w)
    full = lambda shape: pl.BlockSpec(shape, lambda bi, hi, i: (0,) * len(shape))
    return pl.pallas_call(
        functools.partial(_attn_kernel, n_ctx=n_ctx, tq=tq, out_scale=1.0 - lam_init),
        grid=(b, DF_HEADS, tt // tq),
        in_specs=[pl.BlockSpec(memory_space=pltpu.SMEM),
                  pl.BlockSpec((None, tq, w), lambda bi, hi, i: (bi, i, q0 + hi)),
                  pl.BlockSpec((None, tt, w), lambda bi, hi, i: (bi, 0, k0 + hi)),
                  pl.BlockSpec((None, tt, w), lambda bi, hi, i: (bi, 0, v0 + hi)),
                  pl.BlockSpec((tq, w), lambda bi, hi, i: (i, 0)),
                  pl.BlockSpec((tq, w), lambda bi, hi, i: (i, 0)),
                  full((tt, w)), full((tt, w)), full((1, w)), full((1, w)), full((1, w)), full((w, w))],
        out_specs=pl.BlockSpec((None, tq, w), lambda bi, hi, i: (bi, i, hi)),
        out_shape=jax.ShapeDtypeStruct((b, tt, DF_WIDTH), BF16),
        scratch_shapes=[pltpu.VMEM((tt, w), BF16), pltpu.VMEM((tt, w), BF16)],
        compiler_params=_cparams(("parallel", "parallel", "arbitrary")),
        name="diff_attention",
    )(lam.reshape(1).astype(F32), z, z, z, cos, sin, cos, sin, two(p["qn"]), two(p["kn"]),
      p["subln"].reshape(1, w), _group_ones(w, DF_HEAD_DIM))


def _mix_out_kernel(yf_ref, yb_ref, bonus_ref, gate_ref, ob_ref, gnw_ref, gnb_ref, seg_ref, w_ref, h_ref, mod_ref, o_ref):
    seg = seg_ref[...]
    y = yf_ref[...] + yb_ref[...]
    mu = _group_sum(y, seg) * (1.0 / RW_HEAD_DIM)
    d = y - mu
    var = _group_sum(d * d, seg) * (1.0 / RW_HEAD_DIM)
    yn = d * lax.rsqrt(var + RW_GN_EPS) * gnw_ref[...] + gnb_ref[...]
    ya = (yn + bonus_ref[...]) * gate_ref[...]
    mix = _dot(ya.astype(BF16), w_ref[:RW_WIDTH, :]) + _dot(ob_ref[...], w_ref[RW_WIDTH:, :])
    o_ref[...] = h_ref[...] + mod_ref[2:3, :] * mix


def _mix_out(yf, yb, bonus, gate, ob, p, h, mods, n_ctx):
    b, tt, d = h.shape
    c = RW_WIDTH
    tm = ROW_TILE
    tok = lambda width: pl.BlockSpec((None, tm, width), lambda bi, i: (bi, i, 0))
    full = lambda shape: pl.BlockSpec(shape, lambda bi, i: (0,) * len(shape))
    return pl.pallas_call(
        _mix_out_kernel,
        grid=(b, tt // tm),
        in_specs=[tok(c), tok(c), tok(c), tok(c), tok(DF_WIDTH), full((1, c)), full((1, c)), full((c, c)),
                  full((c + DF_WIDTH, d)), tok(d),
                  pl.BlockSpec((None, None, 6, d), lambda bi, i: (bi, _kind(i, n_ctx), 0, 0))],
        out_specs=tok(d),
        out_shape=jax.ShapeDtypeStruct((b, tt, d), F32),
        compiler_params=_cparams(("parallel", "parallel")),
        name="mix_out_residual",
    )(yf, yb, bonus, gate, ob, p["gn_w"].reshape(1, c), p["gn_b"].reshape(1, c), _group_ones(c, RW_HEAD_DIM),
      p["w_out"].astype(BF16), h, mods)


def _sgu_kernel(h_ref, mod_ref, g_ref, w_in_ref, b_in_ref, lng_ref, lnb_ref, ws_ref, bst_ref, w_out_ref, o_ref):
    x = h_ref[...]
    tm = x.shape[0]
    y = _norm_mod(x, g_ref[...], mod_ref[1:2, :], mod_ref[0:1, :])
    z = _dot(y.astype(BF16), w_in_ref[...]) + b_in_ref[...]
    z = 0.5 * z * (1.0 + lax.erf(z * (2.0 ** -0.5)))
    u = z[:, :SGU_WIDTH]
    v = z[:, SGU_WIDTH:]
    mu = jnp.mean(v, axis=-1, keepdims=True)
    var = jnp.mean(jnp.square(v - mu), axis=-1, keepdims=True)
    v = ((v - mu) * lax.rsqrt(var + LN_EPS) * lng_ref[...] + lnb_ref[...]).astype(BF16)
    rows = []
    for ci in range(tm // SGU_CHUNK):
        cols = []
        for gi in range(SGU_GROUPS):
            vb = v[ci * SGU_CHUNK:(ci + 1) * SGU_CHUNK, gi * SGU_GROUP_DIM:(gi + 1) * SGU_GROUP_DIM]
            cols.append(_dot(ws_ref[gi], vb) + bst_ref[:, gi:gi + 1])
        rows.append(jnp.concatenate(cols, axis=1))
    sv = jnp.concatenate(rows, axis=0)
    o_ref[...] = x + mod_ref[2:3, :] * _dot((u * sv).astype(BF16), w_out_ref[...])


def _sgu_layer(h, mods, g, p, n_ctx):
    b, tt, d = h.shape
    tm = ROW_TILE
    full = lambda shape: pl.BlockSpec(shape, lambda bi, i: (0,) * len(shape))
    return pl.pallas_call(
        _sgu_kernel,
        grid=(b, tt // tm),
        in_specs=[pl.BlockSpec((None, tm, d), lambda bi, i: (bi, i, 0)),
                  pl.BlockSpec((None, None, 6, d), lambda bi, i: (bi, _kind(i, n_ctx), 0, 0)),
                  full((1, d)), full((d, 2 * SGU_WIDTH)), full((1, 2 * SGU_WIDTH)),
                  full((1, SGU_WIDTH)), full((1, SGU_WIDTH)),
                  full((SGU_GROUPS, SGU_CHUNK, SGU_CHUNK)), full((SGU_CHUNK, SGU_GROUPS)),
                  full((SGU_WIDTH, d))],
        out_specs=pl.BlockSpec((None, tm, d), lambda bi, i: (bi, i, 0)),
        out_shape=jax.ShapeDtypeStruct((b, tt, d), F32),
        compiler_params=_cparams(("parallel", "parallel")),
        name="sgu_layer",
    )(h, mods, g.reshape(1, d), p["w_in"].astype(BF16), p["b_in"].reshape(1, -1),
      p["ln_g"].reshape(1, -1), p["ln_b"].reshape(1, -1), p["ws"].astype(BF16), p["bs"].T,
      p["w_out"].astype(BF16))


def _router_kernel(h_ref, mod_ref, g_ref, rwh_ref, rwl_ref, rb_ref, f_ref, idx_ref, wt_ref, rank_ref, cnt_ref, base_ref,
                   *, group_rows):
    first = jnp.logical_and(pl.program_id(0) % group_rows == 0, pl.program_id(1) == 0)

    @pl.when(first)
    def _():
        base_ref[...] = jnp.zeros_like(base_ref)

    x = h_ref[...]
    tm = x.shape[0]
    f = _norm_mod(x, g_ref[...], mod_ref[4:5, :], mod_ref[3:4, :])
    fh = f.astype(BF16)
    fl = (f - fh.astype(F32)).astype(BF16)
    f_ref[...] = fh
    logits = _dot(fh, rwh_ref[...]) + _dot(fl, rwh_ref[...]) + _dot(fh, rwl_ref[...]) + rb_ref[...]
    lane = lax.broadcasted_iota(jnp.int32, (tm, LANES), 1)
    tri = (lax.broadcasted_iota(jnp.int32, (tm, tm), 1) < lax.broadcasted_iota(jnp.int32, (tm, tm), 0))
    vals, sels = [], []
    idx_out = jnp.zeros((tm, LANES), jnp.int32)
    l = logits
    for kk in range(TOP_K):
        m = jnp.max(l, axis=-1, keepdims=True)
        idx = jnp.min(jnp.where(l == m, lane, LANES), axis=-1, keepdims=True)
        sel = lane == idx
        l = jnp.where(sel, -jnp.inf, l)
        vals.append(m)
        sels.append(sel)
        idx_out = jnp.where(lane == kk, idx, idx_out)
    es = [jnp.exp(vv - vals[0]) for vv in vals]
    den = es[0] + es[1] + es[2] + es[3]
    wt_out = jnp.zeros((tm, LANES), F32)
    for kk in range(TOP_K):
        wt_out = jnp.where(lane == kk, es[kk] / den, wt_out)
    onehot = jnp.zeros((tm, LANES), F32)
    for sel in sels:
        onehot = onehot + sel.astype(F32)
    ahead = _dot(tri.astype(BF16), onehot.astype(BF16)) + base_ref[...]
    rank_out = jnp.zeros((tm, LANES), jnp.int32)
    for kk in range(TOP_K):
        rk = jnp.sum(jnp.where(sels[kk], ahead, 0.0), axis=-1, keepdims=True).astype(jnp.int32)
        rank_out = jnp.where(lane == kk, rk, rank_out)
    idx_ref[...] = idx_out
    wt_ref[...] = wt_out
    rank_ref[...] = rank_out
    base_ref[...] = base_ref[...] + jnp.sum(onehot, axis=0, keepdims=True)
    cnt_ref[...] = base_ref[...]


def _router(h, mods, g, rt_w, rt_b, n_ctx, n_groups):
    b, tt, d = h.shape
    group_rows = b // n_groups
    tm = ROW_TILE
    rw = jnp.zeros((d, LANES), F32).at[:, :N_EXPERTS].set(rt_w)
    rw_hi = rw.astype(BF16)
    rw_lo = (rw - rw_hi.astype(F32)).astype(BF16)
    rb = jnp.full((1, LANES), -jnp.inf, F32).at[0, :N_EXPERTS].set(rt_b)
    tok = lambda width: pl.BlockSpec((None, tm, width), lambda bi, i: (bi, i, 0))
    return pl.pallas_call(
        functools.partial(_router_kernel, group_rows=group_rows),
        grid=(b, tt // tm),
        in_specs=[pl.BlockSpec((None, tm, d), lambda bi, i: (bi, i, 0)),
                  pl.BlockSpec((None, None, 6, d), lambda bi, i: (bi, _kind(i, n_ctx), 0, 0)),
                  pl.BlockSpec((1, d), lambda bi, i: (0, 0)),
                  pl.BlockSpec((d, LANES), lambda bi, i: (0, 0)),
                  pl.BlockSpec((d, LANES), lambda bi, i: (0, 0)),
                  pl.BlockSpec((1, LANES), lambda bi, i: (0, 0))],
        out_specs=[tok(d), tok(LANES), tok(LANES), tok(LANES),
                   pl.BlockSpec((None, 1, LANES), lambda bi, i: (bi // group_rows, 0, 0))],
        out_shape=[jax.ShapeDtypeStruct((b, tt, d), BF16),
                   jax.ShapeDtypeStruct((b, tt, LANES), jnp.int32),
                   jax.ShapeDtypeStruct((b, tt, LANES), F32),
                   jax.ShapeDtypeStruct((b, tt, LANES), jnp.int32),
                   jax.ShapeDtypeStruct((n_groups, 1, LANES), F32)],
        scratch_shapes=[pltpu.VMEM((1, LANES), F32)],
        compiler_params=_cparams(("arbitrary", "arbitrary")),
        name="moe_router",
    )(h, mods, g.reshape(1, d), rw_hi, rw_lo, rb)


def _expert_kernel(te_ref, nv_ref, x_ref, w1_ref, b1_ref, w2_ref, b2_ref, o_ref, w1b_ref, w2b_ref):
    j = pl.program_id(0)
    active = j < nv_ref[0]
    new_expert = jnp.logical_or(j == 0, te_ref[j] != te_ref[jnp.maximum(j - 1, 0)])

    @pl.when(jnp.logical_and(active, new_expert))
    def _():
        w1b_ref[...] = w1_ref[...].astype(BF16)
        w2b_ref[...] = w2_ref[...].astype(BF16)

    @pl.when(active)
    def _():
        f = w2b_ref.shape[0]
        fh = f // 2
        x = x_ref[...]
        y = b2_ref[...]
        for c in range(2):
            lo, hi = c * fh, (c + 1) * fh
            glu = _dot(x, w1b_ref[:, lo:hi]) + b1_ref[:, lo:hi]
            lin = _dot(x, w1b_ref[:, f + lo:f + hi]) + b1_ref[:, f + lo:f + hi]
            glu = jnp.minimum(glu, SWIGLU_LIMIT)
            lin = jnp.clip(lin, -SWIGLU_LIMIT, SWIGLU_LIMIT)
            act = glu * jax.nn.sigmoid(SWIGLU_ALPHA * glu) * (lin + 1.0)
            y = y + _dot(act.astype(BF16), w2b_ref[lo:hi, :])
        o_ref[...] = y.astype(o_ref.dtype)

    @pl.when(jnp.logical_not(active))
    def _():
        o_ref[...] = jnp.zeros_like(o_ref)


def _experts(xs, tile_expert, n_valid, w1, b1, w2, b2, layer, tile0):
    d = xs.shape[1]
    s = tile_expert.shape[0] * EXPERT_ROW_TILE
    depth, e, _, f2 = w1.shape
    f = f2 // 2
    tm = EXPERT_ROW_TILE
    grid_spec = pltpu.PrefetchScalarGridSpec(
        num_scalar_prefetch=2,
        grid=(s // tm,),
        in_specs=[pl.BlockSpec((tm, d), lambda j, te, nv: (tile0 + j, 0)),
                  pl.BlockSpec((None, None, d, f2), lambda j, te, nv: (layer, te[j], 0, 0)),
                  pl.BlockSpec((None, None, 1, f2), lambda j, te, nv: (layer, te[j], 0, 0)),
                  pl.BlockSpec((None, None, f, d), lambda j, te, nv: (layer, te[j], 0, 0)),
                  pl.BlockSpec((None, None, 1, d), lambda j, te, nv: (layer, te[j], 0, 0))],
        out_specs=pl.BlockSpec((tm, d), lambda j, te, nv: (j, 0)),
        scratch_shapes=[pltpu.VMEM((d, f2), BF16), pltpu.VMEM((f, d), BF16)],
    )
    return pl.pallas_call(
        _expert_kernel,
        grid_spec=grid_spec,
        out_shape=jax.ShapeDtypeStruct((s, d), BF16),
        compiler_params=_cparams(("arbitrary",)),
        name="moe_experts",
    )(tile_expert, n_valid, xs, w1, b1.reshape(depth, e, 1, f2), w2, b2.reshape(depth, e, 1, d))


def _combine_kernel(p_ref, wt_ref, h_ref, mod_ref, o_ref):
    m = p_ref[0].astype(F32) * wt_ref[:, 0:1]
    for k in range(1, TOP_K):
        m = m + p_ref[k].astype(F32) * wt_ref[:, k:k + 1]
    o_ref[...] = h_ref[...] + mod_ref[5:6, :] * m


def _combine(picked, wt, h, mods, n_ctx, b0, b):
    _, tt, d = h.shape
    tm = ROW_TILE
    return pl.pallas_call(
        _combine_kernel,
        grid=(b, tt // tm),
        in_specs=[pl.BlockSpec((TOP_K, None, tm, d), lambda bi, i: (0, bi, i, 0)),
                  pl.BlockSpec((None, tm, LANES), lambda bi, i: (b0 + bi, i, 0)),
                  pl.BlockSpec((None, tm, d), lambda bi, i: (b0 + bi, i, 0)),
                  pl.BlockSpec((None, None, 6, d), lambda bi, i: (b0 + bi, _kind(i, n_ctx), 0, 0))],
        out_specs=pl.BlockSpec((None, tm, d), lambda bi, i: (bi, i, 0)),
        out_shape=jax.ShapeDtypeStruct((b, tt, d), F32),
        compiler_params=_cparams(("parallel", "parallel")),
        name="moe_combine",
    )(picked, wt, h, mods)


def _moe(h, mods, g, rt_w, rt_b, w1, b1, w2, b2, layer, n_ctx):
    b, tt, d = h.shape
    n_groups = MOE_GROUPS if b % MOE_GROUPS == 0 else 1
    gb = b // n_groups
    n = b * tt
    ng = gb * tt
    tm = EXPERT_ROW_TILE
    f, idx, wt, rank, cnt = _router(h, mods, g, rt_w, rt_b, n_ctx, n_groups)
    top_i = idx.reshape(n_groups, ng, LANES)[:, :, :TOP_K]
    rank = rank.reshape(n_groups, ng, LANES)[:, :, :TOP_K]
    counts = cnt[:, 0, :N_EXPERTS].astype(jnp.int32)
    padded = ((counts + tm - 1) // tm) * tm
    p_end = jnp.cumsum(padded, axis=1)
    p_off = p_end - padded
    u_off = jnp.cumsum(counts, axis=1) - counts
    slot = jax.vmap(lambda off, ti: off[ti])(p_off, top_i) + rank
    n_tiles = (ng * TOP_K) // tm + N_EXPERTS
    s_rows = n_tiles * tm
    group_base = (jnp.arange(n_groups, dtype=jnp.int32) * s_rows)[:, None, None]
    _, sorted_pair = lax.sort_key_val((slot + group_base).reshape(-1), jnp.arange(n * TOP_K, dtype=jnp.int32))
    srow = jnp.arange(s_rows, dtype=jnp.int32)
    expert_of = lambda rows: jnp.minimum(
        jnp.sum((rows[None, :, None] >= p_end[:, None, :]).astype(jnp.int32), axis=2), N_EXPERTS - 1)
    row_e = expert_of(srow)
    pick = lambda table, e: jnp.take_along_axis(table, e, axis=1)
    within = srow[None, :] - pick(p_off, row_e)
    valid = within < pick(counts, row_e)
    pair_base = (jnp.arange(n_groups, dtype=jnp.int32) * (ng * TOP_K))[:, None]
    src = jnp.clip(pair_base + pick(u_off, row_e) + within, 0, n * TOP_K - 1)
    row_token = jnp.where(valid, sorted_pair[src] // TOP_K, 0).reshape(-1)
    n_valid = (p_end[:, -1] // tm).astype(jnp.int32)
    tile_start = jnp.arange(n_tiles, dtype=jnp.int32) * tm
    tile_expert = expert_of(tile_start)
    last_e = pick(tile_expert, jnp.maximum(n_valid - 1, 0)[:, None])
    tile_expert = jnp.where(tile_start[None, :] < p_end[:, -1:], tile_expert, last_e)
    xs = jnp.take(f.reshape(n, d), row_token, axis=0, mode="clip")
    outs = []
    for gi in range(n_groups):
        ys = _experts(xs, tile_expert[gi], n_valid[gi:gi + 1], w1, b1, w2, b2, layer, gi * n_tiles)
        picked = jnp.take(ys, slot[gi].T.reshape(-1), axis=0, mode="clip").reshape(TOP_K, gb, tt, d)
        outs.append(_combine(picked, wt, h, mods, n_ctx, gi * gb, gb))
    return outs[0] if n_groups == 1 else jnp.concatenate(outs, axis=0)


def _rope_tables(n_tokens, n_ctx):
    rows = n_tokens // GRID_W
    row = jnp.repeat(jnp.arange(rows, dtype=F32), GRID_W)
    col = jnp.tile(jnp.arange(GRID_W, dtype=F32), rows)
    axis_dim = DF_HEAD_DIM // 2
    inv_freq = ROPE_THETA ** (-jnp.arange(0, axis_dim, 2, dtype=F32) / axis_dim)
    ar = row[:, None] * inv_freq
    ac = col[:, None] * inv_freq
    ang = jnp.concatenate([ar, ar, ac, ac], axis=-1)
    cos = jnp.concatenate([jnp.ones((n_ctx, DF_HEAD_DIM), F32), jnp.cos(ang)], axis=0)
    sin = jnp.concatenate([jnp.zeros((n_ctx, DF_HEAD_DIM), F32), jnp.sin(ang)], axis=0)
    return jnp.tile(cos, (1, 2)), jnp.tile(sin, (1, 2))


def _even_layer(h, mods, g1, p, li, n_ctx, rope):
    z = _nm_mm(h, mods, g1, p["w_in"].astype(BF16), n_ctx)
    r, v, kk, kd0, kd1, lw0, lw1, bb0, bb1, gate, bonus = _rw_features(z, p, n_ctx)
    yf, yb = _rw_scan(r, v, kk, kd0, lw0, bb0, kd1, lw1, bb1, n_ctx)
    lam_init = 0.8 - 0.6 * math.exp(-0.3 * li)
    lp = p["lam"].astype(F32)
    lam = jnp.exp(jnp.sum(lp[0] * lp[1])) - jnp.exp(jnp.sum(lp[2] * lp[3])) + lam_init
    ob = _diff_attention(z, p, lam, lam_init, rope[0], rope[1], n_ctx)
    return _mix_out(yf, yb, bonus, gate, ob, p, h, mods, n_ctx)


def kernel(x, c, ctx, c_ctx, norm1_g, norm2_g, ada_w, ada_b, ev_w_in, ev_w_out, rw_mu, rw_w0, rw_w2, rw_a0, rw_a2, rw_g2, rw_kk, rw_ka, rw_rk, rw_gn_w, rw_gn_b, df_qn, df_kn, df_lam, df_subln, sg_w_in, sg_b_in, sg_ln_g, sg_ln_b, sg_ws, sg_bs, sg_w_out, rt_w, rt_b, ex_w1, ex_b1, ex_w2, ex_b2):
    b, t, d = x.shape
    n_ctx = ctx.shape[1]
    depth = ada_w.shape[0]
    assert d == D_MODEL and t % ROW_TILE == 0 and n_ctx % ROW_TILE == 0 and t % GRID_W == 0
    rope = _rope_tables(t, n_ctx)
    r_pad = -(-(b + 1) // 8) * 8
    cs = jnp.zeros((r_pad, d), F32).at[:b].set(c).at[b].set(c_ctx)
    ada = _ada_table(cs, ada_w, ada_b).reshape(depth, r_pad, 6, d)
    h = jnp.concatenate([ctx, x], axis=1)
    for l in range(depth):
        ctx_out = any(m % 2 == 0 for m in range(l + 1, depth))
        j = l // 2
        mods = jnp.stack([jnp.broadcast_to(ada[l, b], (b, 6, d)), ada[l, :b]], axis=1)
        if l % 2 == 0:
            p = {"w_in": ev_w_in[j], "w_out": ev_w_out[j], "mu": rw_mu[j], "w0": rw_w0[j], "w2": rw_w2[j],
                 "a0": rw_a0[j], "a2": rw_a2[j], "g2": rw_g2[j], "k_k": rw_kk[j], "k_a": rw_ka[j],
                 "r_k": rw_rk[j], "gn_w": rw_gn_w[j], "gn_b": rw_gn_b[j], "qn": df_qn[j], "kn": df_kn[j],
                 "lam": df_lam[j], "subln": df_subln[j]}
            h = _even_layer(h, mods, norm1_g[l], p, l, n_ctx, rope)
        else:
            p = {"w_in": sg_w_in[j], "b_in": sg_b_in[j], "ln_g": sg_ln_g[j], "ln_b": sg_ln_b[j],
                 "ws": sg_ws[j], "bs": sg_bs[j], "w_out": sg_w_out[j]}
            h = _sgu_layer(h, mods, norm1_g[l], p, n_ctx)
        if n_ctx > 0 and not ctx_out:
            h = h[:, n_ctx:]
            n_ctx = 0
        h = _moe(h, mods, norm2_g[l], rt_w[l], rt_b[l], ex_w1, ex_b1, ex_w2, ex_b2, l, n_ctx)
    return h[:, n_ctx:]
```

```python
import functools
import math

import jax
import jax.numpy as jnp
from jax import lax
from jax.experimental import pallas as pl
from jax.experimental.pallas import tpu as pltpu

F32 = jnp.float32
BF16 = jnp.bfloat16
HIGHEST = lax.Precision.HIGHEST

D_MODEL = 1024
GRID_W = 64
RMS_EPS = 1e-6
LN_EPS = 1e-5

RW_HEAD_DIM = 64
RW_WIDTH = 512
RW_HEADS = RW_WIDTH // RW_HEAD_DIM
DECAY_LORA = 64
ICLR_LORA = 64
GATE_LORA = 128
RW_COLS = 3 * RW_WIDTH + 2 * DECAY_LORA + 2 * ICLR_LORA + GATE_LORA
RW_GN_EPS = 64e-5

DF_HEAD_DIM = 64
DF_V_DIM = 128
DF_WIDTH = 512
DF_HEADS = DF_WIDTH // DF_V_DIM
DF_QK = DF_HEADS * 2 * DF_HEAD_DIM
ROPE_THETA = 10000.0
EVEN_COLS = RW_COLS + 2 * DF_QK + DF_WIDTH

SGU_CHUNK = 128
SGU_WIDTH = 1024
SGU_GROUPS = 8
SGU_GROUP_DIM = SGU_WIDTH // SGU_GROUPS

N_EXPERTS = 32
TOP_K = 4
SWIGLU_LIMIT = 7.0
SWIGLU_ALPHA = 1.702

LANES = 128
ROW_TILE = 256
SCAN_CHUNK = 128
EXPERT_ROW_TILE = 512
VMEM_LIMIT = 56 * 1024 * 1024


def _cparams(sem):
    return pltpu.CompilerParams(dimension_semantics=sem, vmem_limit_bytes=VMEM_LIMIT)


def _dot(a, b, precision=None):
    return jnp.dot(a, b, preferred_element_type=F32, precision=precision)


def _dot_nt(a, b, precision=None):
    return lax.dot_general(a, b, (((1,), (1,)), ((), ())), preferred_element_type=F32, precision=precision)


def _dot_tn(a, b, precision=None):
    return lax.dot_general(a, b, (((0,), (0,)), ((), ())), preferred_element_type=F32, precision=precision)


def _norm_mod(x, g, scale, shift):
    ms = jnp.mean(x * x, axis=-1, keepdims=True)
    return (x * lax.rsqrt(ms + RMS_EPS) * g) * (1.0 + scale) + shift


def _ada_kernel(x_ref, w_ref, b_ref, o_ref):
    x = x_ref[...]
    s = x * jax.nn.sigmoid(x)
    o_ref[...] = _dot(s, w_ref[...], HIGHEST) + b_ref[...]


def _ada_table(cs, ada_w, ada_b):
    depth, d, n = ada_w.shape
    r = cs.shape[0]
    tn = 1024
    return pl.pallas_call(
        _ada_kernel,
        grid=(depth, n // tn),
        in_specs=[pl.BlockSpec((r, d), lambda l, j: (0, 0)),
                  pl.BlockSpec((None, d, tn), lambda l, j: (l, 0, j)),
                  pl.BlockSpec((None, 1, tn), lambda l, j: (l, 0, j))],
        out_specs=pl.BlockSpec((None, r, tn), lambda l, j: (l, 0, j)),
        out_shape=jax.ShapeDtypeStruct((depth, r, n), F32),
        compiler_params=_cparams(("parallel", "parallel")),
        name="ada_table",
    )(cs, ada_w, ada_b.reshape(depth, 1, n))


def _kind(i, n_ctx):
    return (i * ROW_TILE >= n_ctx).astype(jnp.int32) if n_ctx > 0 else 1


def _nm_mm_kernel(h_ref, mod_ref, g_ref, w_ref, o_ref):
    y = _norm_mod(h_ref[...], g_ref[...], mod_ref[1:2, :], mod_ref[0:1, :])
    o_ref[...] = _dot(y.astype(BF16), w_ref[...]).astype(o_ref.dtype)


def _nm_mm(h, mods, g, w, n_ctx, out_dtype=F32):
    b, tt, d = h.shape
    n = w.shape[1]
    tm = ROW_TILE
    return pl.pallas_call(
        _nm_mm_kernel,
        grid=(b, tt // tm),
        in_specs=[pl.BlockSpec((None, tm, d), lambda bi, i: (bi, i, 0)),
                  pl.BlockSpec((None, None, 6, d), lambda bi, i: (bi, _kind(i, n_ctx), 0, 0)),
                  pl.BlockSpec((1, d), lambda bi, i: (0, 0)),
                  pl.BlockSpec((d, n), lambda bi, i: (0, 0))],
        out_specs=pl.BlockSpec((None, tm, n), lambda bi, i: (bi, i, 0)),
        out_shape=jax.ShapeDtypeStruct((b, tt, n), out_dtype),
        compiler_params=_cparams(("parallel", "parallel")),
        name="norm_mod_proj",
    )(h, mods, g.reshape(1, d), w)


def _group_ones(width, group):
    idx = jnp.arange(width) // group
    return (idx[:, None] == idx[None, :]).astype(BF16)


def _split_hi_lo(w):
    hi = w.astype(BF16)
    return jnp.stack([hi, (w - hi.astype(F32)).astype(BF16)])


def _dot3(x, w_hi, w_lo):
    xh = x.astype(BF16)
    xl = (x - xh.astype(F32)).astype(BF16)
    return _dot(xh, w_hi) + _dot(xl, w_hi) + _dot(xh, w_lo)


def _group_sum(x, ones_bf16):
    hi = x.astype(BF16)
    lo = (x - hi.astype(F32)).astype(BF16)
    return _dot(hi, ones_bf16) + _dot(lo, ones_bf16)


def _softplus(x):
    return jnp.maximum(x, 0.0) + jnp.log1p(jnp.exp(-jnp.abs(x)))


def _rw_feat_kernel(z_ref, zprev_ref, znext_ref, mu_ref, w0_ref, w2_ref, a0_ref, a2_ref, g2_ref, kk_ref, ka_ref,
                    rk_ref, seg_ref,
                    r_ref, v_ref, kko_ref, kd0_ref, kd1_ref, lw0_ref, lw1_ref, bb0_ref, bb1_ref, g_ref, bonus_ref,
                    *, n_ctx, tt):
    c = RW_WIDTH
    tm = z_ref.shape[0]
    i = pl.program_id(1)
    z = z_ref[...]
    seq_first = jnp.logical_or(i * tm == 0, i * tm == n_ctx)
    seq_last = jnp.logical_or((i + 1) * tm == n_ctx, (i + 1) * tm == tt)
    prev_row = jnp.where(seq_first, 0.0, zprev_ref[7:8, :])
    next_row = jnp.where(seq_last, 0.0, znext_ref[0:1, :])
    rows = lax.broadcasted_iota(jnp.int32, z.shape, 0)
    zp = jnp.where(rows == 0, prev_row, pltpu.roll(z, 1, 0))
    zn = jnp.where(rows == tm - 1, next_row, pltpu.roll(z, tm - 1, 0))
    za = z + mu_ref[0:1, :] * (zp - z) + mu_ref[1:2, :] * (zn - z)
    r = za[:, :c]
    k = za[:, c:2 * c]
    v = za[:, 2 * c:3 * c]
    wl = za[:, 3 * c:3 * c + 2 * DECAY_LORA]
    al = za[:, 3 * c + 2 * DECAY_LORA:3 * c + 2 * DECAY_LORA + 2 * ICLR_LORA]
    gl = za[:, 3 * c + 2 * DECAY_LORA + 2 * ICLR_LORA:]
    r_ref[...] = r.astype(r_ref.dtype)
    v_ref[...] = v.astype(v_ref.dtype)
    twl = jnp.tanh(wl)
    seg = seg_ref[...]
    kkr = k * kk_ref[...]
    kk = kkr / jnp.maximum(jnp.sqrt(_group_sum(kkr * kkr, seg)), 1e-12)
    kko_ref[...] = kk.astype(kko_ref.dtype)
    lw_refs = (lw0_ref, lw1_ref)
    bb_refs = (bb0_ref, bb1_ref)
    kd_refs = (kd0_ref, kd1_ref)
    kd_sum = None
    for d in range(2):
        wd = w0_ref[d:d + 1, :] + _dot3(twl[:, d * DECAY_LORA:(d + 1) * DECAY_LORA], w2_ref[0, d], w2_ref[1, d])
        w = -_softplus(-wd) - 0.5
        lw_refs[d][...] = -jnp.exp(w)
        a = jax.nn.sigmoid(a0_ref[d:d + 1, :] + _dot3(al[:, d * ICLR_LORA:(d + 1) * ICLR_LORA], a2_ref[0, d], a2_ref[1, d]))
        bb_refs[d][...] = (kk * a).astype(bb_refs[d].dtype)
        kd = k * (1.0 + (a - 1.0) * ka_ref[...])
        kd_refs[d][...] = kd.astype(kd_refs[d].dtype)
        kd_sum = kd if kd_sum is None else kd_sum + kd
    g_ref[...] = _dot3(jax.nn.sigmoid(gl), g2_ref[0], g2_ref[1])
    bonus_ref[...] = _group_sum(r * kd_sum * rk_ref[...], seg) * v


def _rw_features(z, p, n_ctx):
    b, tt, _ = z.shape
    c = RW_WIDTH
    tm = ROW_TILE
    hb = tm // 8
    last_halo = tt // 8 - 1
    full = lambda shape: pl.BlockSpec(shape, lambda bi, i: (0,) * len(shape))
    out_spec = pl.BlockSpec((None, tm, c), lambda bi, i: (bi, i, 0))
    return pl.pallas_call(
        functools.partial(_rw_feat_kernel, n_ctx=n_ctx, tt=tt),
        grid=(b, tt // tm),
        in_specs=[pl.BlockSpec((None, tm, RW_COLS), lambda bi, i: (bi, i, 0)),
                  pl.BlockSpec((None, 8, RW_COLS), lambda bi, i: (bi, jnp.maximum(i * hb - 1, 0), 0)),
                  pl.BlockSpec((None, 8, RW_COLS), lambda bi, i: (bi, jnp.minimum((i + 1) * hb, last_halo), 0)),
                  full((2, RW_COLS)), full((2, c)), full((2, 2, DECAY_LORA, c)), full((2, c)), full((2, 2, ICLR_LORA, c)),
                  full((2, GATE_LORA, c)), full((1, c)), full((1, c)), full((1, c)), full((c, c))],
        out_specs=[out_spec] * 11,
        out_shape=[jax.ShapeDtypeStruct((b, tt, c), dt) for dt in (BF16,) * 5 + (F32, F32, BF16, BF16, F32, F32)],
        compiler_params=_cparams(("parallel", "parallel")),
        name="rwkv_features",
    )(z, z, z, p["mu"], p["w0"], _split_hi_lo(p["w2"]), p["a0"], _split_hi_lo(p["a2"]), _split_hi_lo(p["g2"]), p["k_k"].reshape(1, c), p["k_a"].reshape(1, c),
      p["r_k"].reshape(1, c), _group_ones(c, RW_HEAD_DIM))


def _cumsum_rows(tri_bf16, x):
    hi = x.astype(BF16)
    r1 = x - hi.astype(F32)
    mid = r1.astype(BF16)
    lo = (r1 - mid.astype(F32)).astype(BF16)
    return _dot(tri_bf16, hi) + _dot(tri_bf16, mid) + _dot(tri_bf16, lo)


def _scan_masks(reverse):
    L = SCAN_CHUNK
    row = lax.broadcasted_iota(jnp.int32, (L, L), 0)
    col = lax.broadcasted_iota(jnp.int32, (L, L), 1)
    incl = (col >= row) if reverse else (col <= row)
    strict = (col > row) if reverse else (col < row)
    levels = []
    for lg in range(int(math.log2(L))):
        same_pair = jnp.right_shift(row, lg + 1) == jnp.right_shift(col, lg + 1)
        other_half = jnp.right_shift(row, lg) != jnp.right_shift(col, lg)
        levels.append(jnp.logical_and(jnp.logical_and(same_pair, other_half), strict))
    return incl, strict, row == col, levels


def _scan_operands(r_ref, v_ref, kk_ref, kd_ref, lw_ref, bb_ref, incl, reverse):
    L = SCAN_CHUNK
    lw = lw_ref[...]
    cum = _cumsum_rows(incl.astype(BF16), lw)
    ctot = cum[0:1, :] if reverse else cum[L - 1:L, :]
    g_inv = jnp.exp(-cum)
    g_end = jnp.exp(ctot - cum)
    kd = kd_ref[...].astype(F32)
    bb = bb_ref[...].astype(F32)
    r_f = r_ref[...].astype(F32) * jnp.exp(cum)
    return dict(a=(-kk_ref[...].astype(F32) * jnp.exp(cum - lw)).astype(BF16), b=(bb * g_inv).astype(BF16),
                k=(kd * g_inv).astype(BF16), r_f=r_f, r=r_f.astype(BF16), b_e=(bb * g_end).astype(BF16),
                k_e=(kd * g_end).astype(BF16), v=v_ref[...].astype(BF16), g_tot=jnp.exp(ctot))


def _scan_kernel(rf_ref, vf_ref, kkf_ref, kdf_ref, lwf_ref, bbf_ref, rb_ref, vb_ref, kkb_ref, kdb_ref, lwb_ref, bbb_ref,
                 yf_ref, yb_ref, s_ref):
    L = SCAN_CHUNK
    n = RW_HEAD_DIM

    @pl.when(pl.program_id(1) == 0)
    def _():
        s_ref[...] = jnp.zeros_like(s_ref)

    masks = (_scan_masks(False), _scan_masks(True))
    ops = (_scan_operands(rf_ref, vf_ref, kkf_ref, kdf_ref, lwf_ref, bbf_ref, masks[0][0], False),
           _scan_operands(rb_ref, vb_ref, kkb_ref, kdb_ref, lwb_ref, bbb_ref, masks[1][0], True))
    y_refs = (yf_ref, yb_ref)
    eye_n = lax.broadcasted_iota(jnp.int32, (n, n), 0) == lax.broadcasted_iota(jnp.int32, (n, n), 1)
    chains = [(d, h) for h in range(RW_HEADS) for d in range(2)]
    cut = lambda d, h, name: ops[d][name][:, h * n:(h + 1) * n]

    gram = [_dot_nt(jnp.concatenate([cut(d, h, "a"), cut(d, h, "r")], axis=0),
                    jnp.concatenate([cut(d, h, "b"), cut(d, h, "k")], axis=0)) for d, h in chains]
    n_ab = [g[:L, :L] for g in gram]
    lower = [jnp.concatenate([jnp.where(masks[d][1], g[:L, L:], 0.0), jnp.where(masks[d][0], g[L:, L:], 0.0)],
                             axis=0).astype(BF16) for (d, h), g in zip(chains, gram)]
    m_rb = [jnp.where(masks[d][0], g[L:, :L], 0.0).astype(BF16) for (d, h), g in zip(chains, gram)]
    nv = [_dot(lo, cut(d, h, "v")) for (d, h), lo in zip(chains, lower)]
    t_inv = [jnp.where(masks[d][2], 1.0, jnp.where(masks[d][3][0], nab, 0.0)) for (d, h), nab in zip(chains, n_ab)]
    for lv in range(1, len(masks[0][3])):
        tb = [t.astype(BF16) for t in t_inv]
        half = [_dot(t, jnp.where(masks[d][3][lv], nab, 0.0).astype(BF16)).astype(BF16)
                for (d, h), t, nab in zip(chains, tb, n_ab)]
        t_inv = [t + _dot(hf, t16) for t, hf, t16 in zip(t_inv, half, tb)]
    au = [_dot(t.astype(BF16), jnp.concatenate([cut(d, h, "a"), x[:L].astype(BF16)], axis=1)).astype(BF16)
          for (d, h), t, x in zip(chains, t_inv, nv)]
    ry = [_dot(m, x) for m, x in zip(m_rb, au)]
    pq = [_dot_tn(x, cut(d, h, "b_e")) for (d, h), x in zip(chains, au)]
    vk = [_dot_tn(cut(d, h, "v"), cut(d, h, "k_e")) for d, h in chains]
    for i, (d, h) in enumerate(chains):
        sl = slice(h * n, (h + 1) * n)
        r_hat = (ops[d]["r_f"][:, sl] + ry[i][:, :n]).astype(BF16)
        y_hat = ry[i][:, n:] + nv[i][L:]
        p_mat = jnp.where(eye_n, ops[d]["g_tot"][:, sl], 0.0) + pq[i][:n]
        q_mat = pq[i][n:] + vk[i]
        s0b = s_ref[d, h].astype(BF16)
        y_refs[d][:, sl] = _dot_nt(r_hat, s0b) + y_hat
        s_ref[d, h] = _dot(s0b, p_mat.astype(BF16)) + q_mat


def _rw_scan(r, v, kk, kd0, lw0, bb0, kd1, lw1, bb1, n_ctx):
    b, tt, c = r.shape
    L = SCAN_CHUNK
    nc = tt // L
    ncc = n_ctx // L
    fwd = pl.BlockSpec((None, L, c), lambda bi, i: (bi, i, 0))
    bwd = pl.BlockSpec((None, L, c), lambda bi, i: (bi, jnp.where(i < ncc, ncc - 1 - i, nc - 1 - (i - ncc)), 0))
    return pl.pallas_call(
        _scan_kernel,
        grid=(b, nc),
        in_specs=[fwd] * 6 + [bwd] * 6,
        out_specs=[fwd, bwd],
        out_shape=[jax.ShapeDtypeStruct((b, tt, c), F32)] * 2,
        scratch_shapes=[pltpu.VMEM((2, RW_HEADS, RW_HEAD_DIM, RW_HEAD_DIM), F32)],
        compiler_params=_cparams(("parallel", "arbitrary")),
        name="rwkv_scan",
    )(r, v, kk, kd0, lw0, bb0, r, v, kk, kd1, lw1, bb1)


def _qk_prep(x, gain, cos, sin, ones):
    xn = x * lax.rsqrt(_group_sum(x * x, ones) * (1.0 / DF_HEAD_DIM) + RMS_EPS) * gain
    lane = lax.broadcasted_iota(jnp.int32, x.shape, 1)
    quarter = DF_HEAD_DIM // 4
    rot = jnp.where(jnp.bitwise_and(lane, 2 * quarter - 1) < quarter, -pltpu.roll(xn, x.shape[1] - quarter, 1), pltpu.roll(xn, quarter, 1))
    return xn * cos + rot * sin


def _attn_kernel(lam_ref, q_ref, k_ref, v_ref, cosq_ref, sinq_ref, cosk_ref, sink_ref, qn_ref, kn_ref, sub_ref, ones_ref,
                 o_ref, kb_ref, vb_ref, *, n_ctx, tq, out_scale):
    e = DF_HEAD_DIM
    tt = k_ref.shape[0]
    qi = pl.program_id(2)
    ones = ones_ref[...]

    @pl.when(qi == 0)
    def _():
        kb_ref[...] = _qk_prep(k_ref[...], kn_ref[...], cosk_ref[...], sink_ref[...], ones).astype(BF16)
        vb_ref[...] = v_ref[...].astype(BF16)

    q = (_qk_prep(q_ref[...], qn_ref[...], cosq_ref[...], sinq_ref[...], ones)
         * (DF_HEAD_DIM ** -0.5 * math.log2(math.e))).astype(BF16)

    def attend(kv_len):
        k = kb_ref[:kv_len, :]
        v = vb_ref[:kv_len, :]
        outs = []
        for m in range(2):
            s = _dot_nt(q[:, m * e:(m + 1) * e], k[:, m * e:(m + 1) * e])
            p = jnp.exp2(s - jnp.max(s, axis=-1, keepdims=True))
            l = jnp.sum(p, axis=-1, keepdims=True)
            outs.append(_dot(p.astype(BF16), v) / l)
        o = outs[0] - lam_ref[0] * outs[1]
        o = o * lax.rsqrt(jnp.mean(o * o, axis=-1, keepdims=True) + RMS_EPS) * sub_ref[...] * out_scale
        o_ref[...] = o.astype(o_ref.dtype)

    if n_ctx > 0:
        pl.when(qi * tq < n_ctx)(lambda: attend(n_ctx))
        pl.when(qi * tq >= n_ctx)(lambda: attend(tt))
    else:
        attend(tt)


def _diff_attention(z, p, lam, lam_init, cos, sin, n_ctx):
    b, tt, _ = z.shape
    tq = ROW_TILE
    w = DF_V_DIM
    q0 = RW_COLS // w
    k0 = q0 + DF_HEADS
    v0 = k0 + DF_HEADS
    two = lambda g: jnp.tile(g, 2).reshape(1, w)
    full = lambda shape: pl.BlockSpec(shape, lambda bi, hi, i: (0,) * len(shape))
    return pl.pallas_call(
        functools.partial(_attn_kernel, n_ctx=n_ctx, tq=tq, out_scale=1.0 - lam_init),
        grid=(b, DF_HEADS, tt // tq),
        in_specs=[pl.BlockSpec(memory_space=pltpu.SMEM),
                  pl.BlockSpec((None, tq, w), lambda bi, hi, i: (bi, i, q0 + hi)),
                  pl.BlockSpec((None, tt, w), lambda bi, hi, i: (bi, 0, k0 + hi)),
                  pl.BlockSpec((None, tt, w), lambda bi, hi, i: (bi, 0, v0 + hi)),
                  pl.BlockSpec((tq, w), lambda bi, hi, i: (i, 0)),
                  pl.BlockSpec((tq, w), lambda bi, hi, i: (i, 0)),
                  full((tt, w)), full((tt, w)), full((1, w)), full((1, w)), full((1, w)), full((w, w))],
        out_specs=pl.BlockSpec((None, tq, w), lambda bi, hi, i: (bi, i, hi)),
        out_shape=jax.ShapeDtypeStruct((b, tt, DF_WIDTH), BF16),
        scratch_shapes=[pltpu.VMEM((tt, w), BF16), pltpu.VMEM((tt, w), BF16)],
        compiler_params=_cparams(("parallel", "parallel", "arbitrary")),
        name="diff_attention",
    )(lam.reshape(1).astype(F32), z, z, z, cos, sin, cos, sin, two(p["qn"]), two(p["kn"]),
      p["subln"].reshape(1, w), _group_ones(w, DF_HEAD_DIM))


def _mix_out_kernel(yf_ref, yb_ref, bonus_ref, gate_ref, ob_ref, gnw_ref, gnb_ref, seg_ref, w_ref, h_ref, mod_ref, o_ref):
    seg = seg_ref[...]
    y = yf_ref[...] + yb_ref[...]
    mu = _group_sum(y, seg) * (1.0 / RW_HEAD_DIM)
    d = y - mu
    var = _group_sum(d * d, seg) * (1.0 / RW_HEAD_DIM)
    yn = d * lax.rsqrt(var + RW_GN_EPS) * gnw_ref[...] + gnb_ref[...]
    ya = (yn + bonus_ref[...]) * gate_ref[...]
    mix = _dot(ya.astype(BF16), w_ref[:RW_WIDTH, :]) + _dot(ob_ref[...], w_ref[RW_WIDTH:, :])
    o_ref[...] = h_ref[...] + mod_ref[2:3, :] * mix


def _mix_out(yf, yb, bonus, gate, ob, p, h, mods, n_ctx):
    b, tt, d = h.shape
    c = RW_WIDTH
    tm = ROW_TILE
    tok = lambda width: pl.BlockSpec((None, tm, width), lambda bi, i: (bi, i, 0))
    full = lambda shape: pl.BlockSpec(shape, lambda bi, i: (0,) * len(shape))
    return pl.pallas_call(
        _mix_out_kernel,
        grid=(b, tt // tm),
        in_specs=[tok(c), tok(c), tok(c), tok(c), tok(DF_WIDTH), full((1, c)), full((1, c)), full((c, c)),
                  full((c + DF_WIDTH, d)), tok(d),
                  pl.BlockSpec((None, None, 6, d), lambda bi, i: (bi, _kind(i, n_ctx), 0, 0))],
        out_specs=tok(d),
        out_shape=jax.ShapeDtypeStruct((b, tt, d), F32),
        compiler_params=_cparams(("parallel", "parallel")),
        name="mix_out_residual",
    )(yf, yb, bonus, gate, ob, p["gn_w"].reshape(1, c), p["gn_b"].reshape(1, c), _group_ones(c, RW_HEAD_DIM),
      p["w_out"].astype(BF16), h, mods)


def _sgu_kernel(h_ref, mod_ref, g_ref, w_in_ref, b_in_ref, lng_ref, lnb_ref, ws_ref, bst_ref, w_out_ref, o_ref):
    x = h_ref[...]
    tm = x.shape[0]
    y = _norm_mod(x, g_ref[...], mod_ref[1:2, :], mod_ref[0:1, :])
    z = _dot(y.astype(BF16), w_in_ref[...]) + b_in_ref[...]
    z = 0.5 * z * (1.0 + lax.erf(z * (2.0 ** -0.5)))
    u = z[:, :SGU_WIDTH]
    v = z[:, SGU_WIDTH:]
    mu = jnp.mean(v, axis=-1, keepdims=True)
    var = jnp.mean(jnp.square(v - mu), axis=-1, keepdims=True)
    v = ((v - mu) * lax.rsqrt(var + LN_EPS) * lng_ref[...] + lnb_ref[...]).astype(BF16)
    rows = []
    for ci in range(tm // SGU_CHUNK):
        cols = []
        for gi in range(SGU_GROUPS):
            vb = v[ci * SGU_CHUNK:(ci + 1) * SGU_CHUNK, gi * SGU_GROUP_DIM:(gi + 1) * SGU_GROUP_DIM]
            cols.append(_dot(ws_ref[gi], vb) + bst_ref[:, gi:gi + 1])
        rows.append(jnp.concatenate(cols, axis=1))
    sv = jnp.concatenate(rows, axis=0)
    o_ref[...] = x + mod_ref[2:3, :] * _dot((u * sv).astype(BF16), w_out_ref[...])


def _sgu_layer(h, mods, g, p, n_ctx):
    b, tt, d = h.shape
    tm = ROW_TILE
    full = lambda shape: pl.BlockSpec(shape, lambda bi, i: (0,) * len(shape))
    return pl.pallas_call(
        _sgu_kernel,
        grid=(b, tt // tm),
        in_specs=[pl.BlockSpec((None, tm, d), lambda bi, i: (bi, i, 0)),
                  pl.BlockSpec((None, None, 6, d), lambda bi, i: (bi, _kind(i, n_ctx), 0, 0)),
                  full((1, d)), full((d, 2 * SGU_WIDTH)), full((1, 2 * SGU_WIDTH)),
                  full((1, SGU_WIDTH)), full((1, SGU_WIDTH)),
                  full((SGU_GROUPS, SGU_CHUNK, SGU_CHUNK)), full((SGU_CHUNK, SGU_GROUPS)),
                  full((SGU_WIDTH, d))],
        out_specs=pl.BlockSpec((None, tm, d), lambda bi, i: (bi, i, 0)),
        out_shape=jax.ShapeDtypeStruct((b, tt, d), F32),
        compiler_params=_cparams(("parallel", "parallel")),
        name="sgu_layer",
    )(h, mods, g.reshape(1, d), p["w_in"].astype(BF16), p["b_in"].reshape(1, -1),
      p["ln_g"].reshape(1, -1), p["ln_b"].reshape(1, -1), p["ws"].astype(BF16), p["bs"].T,
      p["w_out"].astype(BF16))


def _router_kernel(h_ref, mod_ref, g_ref, rwh_ref, rwl_ref, rb_ref, f_ref, idx_ref, wt_ref, rank_ref, cnt_ref, base_ref):
    first = jnp.logical_and(pl.program_id(0) == 0, pl.program_id(1) == 0)

    @pl.when(first)
    def _():
        base_ref[...] = jnp.zeros_like(base_ref)

    x = h_ref[...]
    tm = x.shape[0]
    f = _norm_mod(x, g_ref[...], mod_ref[4:5, :], mod_ref[3:4, :])
    fh = f.astype(BF16)
    fl = (f - fh.astype(F32)).astype(BF16)
    f_ref[...] = fh
    logits = _dot(fh, rwh_ref[...]) + _dot(fl, rwh_ref[...]) + _dot(fh, rwl_ref[...]) + rb_ref[...]
    lane = lax.broadcasted_iota(jnp.int32, (tm, LANES), 1)
    tri = (lax.broadcasted_iota(jnp.int32, (tm, tm), 1) < lax.broadcasted_iota(jnp.int32, (tm, tm), 0))
    vals, sels = [], []
    idx_out = jnp.zeros((tm, LANES), jnp.int32)
    l = logits
    for kk in range(TOP_K):
        m = jnp.max(l, axis=-1, keepdims=True)
        idx = jnp.min(jnp.where(l == m, lane, LANES), axis=-1, keepdims=True)
        sel = lane == idx
        l = jnp.where(sel, -jnp.inf, l)
        vals.append(m)
        sels.append(sel)
        idx_out = jnp.where(lane == kk, idx, idx_out)
    es = [jnp.exp(vv - vals[0]) for vv in vals]
    den = es[0] + es[1] + es[2] + es[3]
    wt_out = jnp.zeros((tm, LANES), F32)
    for kk in range(TOP_K):
        wt_out = jnp.where(lane == kk, es[kk] / den, wt_out)
    onehot = jnp.zeros((tm, LANES), F32)
    for sel in sels:
        onehot = onehot + sel.astype(F32)
    ahead = _dot(tri.astype(BF16), onehot.astype(BF16)) + base_ref[...]
    rank_out = jnp.zeros((tm, LANES), jnp.int32)
    for kk in range(TOP_K):
        rk = jnp.sum(jnp.where(sels[kk], ahead, 0.0), axis=-1, keepdims=True).astype(jnp.int32)
        rank_out = jnp.where(lane == kk, rk, rank_out)
    idx_ref[...] = idx_out
    wt_ref[...] = wt_out
    rank_ref[...] = rank_out
    base_ref[...] = base_ref[...] + jnp.sum(onehot, axis=0, keepdims=True)
    cnt_ref[...] = base_ref[...]


def _router(h, mods, g, rt_w, rt_b, n_ctx, b0, b):
    _, tt, d = h.shape
    tm = ROW_TILE
    rw = jnp.zeros((d, LANES), F32).at[:, :N_EXPERTS].set(rt_w)
    rw_hi = rw.astype(BF16)
    rw_lo = (rw - rw_hi.astype(F32)).astype(BF16)
    rb = jnp.full((1, LANES), -jnp.inf, F32).at[0, :N_EXPERTS].set(rt_b)
    tok = lambda width: pl.BlockSpec((None, tm, width), lambda bi, i: (bi, i, 0))
    return pl.pallas_call(
        _router_kernel,
        grid=(b, tt // tm),
        in_specs=[pl.BlockSpec((None, tm, d), lambda bi, i: (b0 + bi, i, 0)),
                  pl.BlockSpec((None, None, 6, d), lambda bi, i: (b0 + bi, _kind(i, n_ctx), 0, 0)),
                  pl.BlockSpec((1, d), lambda bi, i: (0, 0)),
                  pl.BlockSpec((d, LANES), lambda bi, i: (0, 0)),
                  pl.BlockSpec((d, LANES), lambda bi, i: (0, 0)),
                  pl.BlockSpec((1, LANES), lambda bi, i: (0, 0))],
        out_specs=[tok(d), tok(LANES), tok(LANES), tok(LANES), pl.BlockSpec((1, LANES), lambda bi, i: (0, 0))],
        out_shape=[jax.ShapeDtypeStruct((b, tt, d), BF16),
                   jax.ShapeDtypeStruct((b, tt, LANES), jnp.int32),
                   jax.ShapeDtypeStruct((b, tt, LANES), F32),
                   jax.ShapeDtypeStruct((b, tt, LANES), jnp.int32),
                   jax.ShapeDtypeStruct((1, LANES), F32)],
        scratch_shapes=[pltpu.VMEM((1, LANES), F32)],
        compiler_params=_cparams(("arbitrary", "arbitrary")),
        name="moe_router",
    )(h, mods, g.reshape(1, d), rw_hi, rw_lo, rb)


def _expert_kernel(te_ref, nv_ref, x_ref, w1_ref, b1_ref, w2_ref, b2_ref, o_ref, w1b_ref, w2b_ref):
    j = pl.program_id(0)
    active = j < nv_ref[0]
    new_expert = jnp.logical_or(j == 0, te_ref[j] != te_ref[jnp.maximum(j - 1, 0)])

    @pl.when(jnp.logical_and(active, new_expert))
    def _():
        w1b_ref[...] = w1_ref[...].astype(BF16)
        w2b_ref[...] = w2_ref[...].astype(BF16)

    @pl.when(active)
    def _():
        f = w2b_ref.shape[0]
        fh = f // 2
        x = x_ref[...]
        y = b2_ref[...]
        for c in range(2):
            lo, hi = c * fh, (c + 1) * fh
            glu = _dot(x, w1b_ref[:, lo:hi]) + b1_ref[:, lo:hi]
            lin = _dot(x, w1b_ref[:, f + lo:f + hi]) + b1_ref[:, f + lo:f + hi]
            glu = jnp.minimum(glu, SWIGLU_LIMIT)
            lin = jnp.clip(lin, -SWIGLU_LIMIT, SWIGLU_LIMIT)
            act = glu * jax.nn.sigmoid(SWIGLU_ALPHA * glu) * (lin + 1.0)
            y = y + _dot(act.astype(BF16), w2b_ref[lo:hi, :])
        o_ref[...] = y.astype(o_ref.dtype)

    @pl.when(jnp.logical_not(active))
    def _():
        o_ref[...] = jnp.zeros_like(o_ref)


def _experts(xs, tile_expert, n_valid, w1, b1, w2, b2, layer):
    s, d = xs.shape
    depth, e, _, f2 = w1.shape
    f = f2 // 2
    tm = EXPERT_ROW_TILE
    grid_spec = pltpu.PrefetchScalarGridSpec(
        num_scalar_prefetch=2,
        grid=(s // tm,),
        in_specs=[pl.BlockSpec((tm, d), lambda j, te, nv: (j, 0)),
                  pl.BlockSpec((None, None, d, f2), lambda j, te, nv: (layer, te[j], 0, 0)),
                  pl.BlockSpec((None, None, 1, f2), lambda j, te, nv: (layer, te[j], 0, 0)),
                  pl.BlockSpec((None, None, f, d), lambda j, te, nv: (layer, te[j], 0, 0)),
                  pl.BlockSpec((None, None, 1, d), lambda j, te, nv: (layer, te[j], 0, 0))],
        out_specs=pl.BlockSpec((tm, d), lambda j, te, nv: (j, 0)),
        scratch_shapes=[pltpu.VMEM((d, f2), BF16), pltpu.VMEM((f, d), BF16)],
    )
    return pl.pallas_call(
        _expert_kernel,
        grid_spec=grid_spec,
        out_shape=jax.ShapeDtypeStruct((s, d), BF16),
        compiler_params=_cparams(("arbitrary",)),
        name="moe_experts",
    )(tile_expert, n_valid, xs, w1, b1.reshape(depth, e, 1, f2), w2, b2.reshape(depth, e, 1, d))


def _combine_kernel(p_ref, wt_ref, h_ref, mod_ref, o_ref):
    m = p_ref[0].astype(F32) * wt_ref[:, 0:1]
    for k in range(1, TOP_K):
        m = m + p_ref[k].astype(F32) * wt_ref[:, k:k + 1]
    o_ref[...] = h_ref[...] + mod_ref[5:6, :] * m


def _combine(picked, wt, h, mods, n_ctx, b0, b):
    _, tt, d = h.shape
    tm = ROW_TILE
    return pl.pallas_call(
        _combine_kernel,
        grid=(b, tt // tm),
        in_specs=[pl.BlockSpec((TOP_K, None, tm, d), lambda bi, i: (0, bi, i, 0)),
                  pl.BlockSpec((None, tm, LANES), lambda bi, i: (bi, i, 0)),
                  pl.BlockSpec((None, tm, d), lambda bi, i: (b0 + bi, i, 0)),
                  pl.BlockSpec((None, None, 6, d), lambda bi, i: (b0 + bi, _kind(i, n_ctx), 0, 0))],
        out_specs=pl.BlockSpec((None, tm, d), lambda bi, i: (bi, i, 0)),
        out_shape=jax.ShapeDtypeStruct((b, tt, d), F32),
        compiler_params=_cparams(("parallel", "parallel")),
        name="moe_combine",
    )(picked, wt, h, mods)


def _moe(h, mods, g, rt_w, rt_b, w1, b1, w2, b2, layer, n_ctx, b0, b):
    _, tt, d = h.shape
    n = b * tt
    tm = EXPERT_ROW_TILE
    f, idx, wt, rank, cnt = _router(h, mods, g, rt_w, rt_b, n_ctx, b0, b)
    top_i = idx.reshape(n, LANES)[:, :TOP_K]
    rank = rank.reshape(n, LANES)[:, :TOP_K]
    counts = cnt[0, :N_EXPERTS].astype(jnp.int32)
    padded = ((counts + tm - 1) // tm) * tm
    p_end = jnp.cumsum(padded)
    p_off = p_end - padded
    u_off = jnp.cumsum(counts) - counts
    slot = (p_off[top_i] + rank).reshape(-1)
    n_tiles = (n * TOP_K) // tm + N_EXPERTS
    s_rows = n_tiles * tm
    _, sorted_pair = lax.sort_key_val(slot, jnp.arange(n * TOP_K, dtype=jnp.int32))
    srow = jnp.arange(s_rows, dtype=jnp.int32)
    expert_of = lambda rows: jnp.minimum(jnp.sum((rows[:, None] >= p_end[None, :]).astype(jnp.int32), axis=1),
                                         N_EXPERTS - 1)
    row_e = expert_of(srow)
    within = srow - p_off[row_e]
    valid = within < counts[row_e]
    src = jnp.clip(u_off[row_e] + within, 0, n * TOP_K - 1)
    pair = sorted_pair[src]
    row_token = jnp.where(valid, pair // TOP_K, 0)
    n_valid = (p_end[-1] // tm).astype(jnp.int32).reshape(1)
    tile_start = jnp.arange(n_tiles, dtype=jnp.int32) * tm
    tile_expert = expert_of(tile_start)
    last_e = tile_expert[jnp.maximum(n_valid[0] - 1, 0)]
    tile_expert = jnp.where(tile_start < p_end[-1], tile_expert, last_e)
    xs = jnp.take(f.reshape(n, d), row_token, axis=0, mode="clip")
    ys = _experts(xs, tile_expert, n_valid, w1, b1, w2, b2, layer)
    picked = jnp.take(ys, slot.reshape(n, TOP_K).T.reshape(-1), axis=0, mode="clip").reshape(TOP_K, b, tt, d)
    return _combine(picked, wt, h, mods, n_ctx, b0, b)


def _rope_tables(n_tokens, n_ctx):
    rows = n_tokens // GRID_W
    row = jnp.repeat(jnp.arange(rows, dtype=F32), GRID_W)
    col = jnp.tile(jnp.arange(GRID_W, dtype=F32), rows)
    axis_dim = DF_HEAD_DIM // 2
    inv_freq = ROPE_THETA ** (-jnp.arange(0, axis_dim, 2, dtype=F32) / axis_dim)
    ar = row[:, None] * inv_freq
    ac = col[:, None] * inv_freq
    ang = jnp.concatenate([ar, ar, ac, ac], axis=-1)
    cos = jnp.concatenate([jnp.ones((n_ctx, DF_HEAD_DIM), F32), jnp.cos(ang)], axis=0)
    sin = jnp.concatenate([jnp.zeros((n_ctx, DF_HEAD_DIM), F32), jnp.sin(ang)], axis=0)
    return jnp.tile(cos, (1, 2)), jnp.tile(sin, (1, 2))


def _even_layer(h, mods, g1, p, li, n_ctx, rope):
    z = _nm_mm(h, mods, g1, p["w_in"].astype(BF16), n_ctx)
    r, v, kk, kd0, kd1, lw0, lw1, bb0, bb1, gate, bonus = _rw_features(z, p, n_ctx)
    yf, yb = _rw_scan(r, v, kk, kd0, lw0, bb0, kd1, lw1, bb1, n_ctx)
    lam_init = 0.8 - 0.6 * math.exp(-0.3 * li)
    lp = p["lam"].astype(F32)
    lam = jnp.exp(jnp.sum(lp[0] * lp[1])) - jnp.exp(jnp.sum(lp[2] * lp[3])) + lam_init
    ob = _diff_attention(z, p, lam, lam_init, rope[0], rope[1], n_ctx)
    return _mix_out(yf, yb, bonus, gate, ob, p, h, mods, n_ctx)


def kernel(x, c, ctx, c_ctx, norm1_g, norm2_g, ada_w, ada_b, ev_w_in, ev_w_out, rw_mu, rw_w0, rw_w2, rw_a0, rw_a2, rw_g2, rw_kk, rw_ka, rw_rk, rw_gn_w, rw_gn_b, df_qn, df_kn, df_lam, df_subln, sg_w_in, sg_b_in, sg_ln_g, sg_ln_b, sg_ws, sg_bs, sg_w_out, rt_w, rt_b, ex_w1, ex_b1, ex_w2, ex_b2):
    b, t, d = x.shape
    n_ctx = ctx.shape[1]
    depth = ada_w.shape[0]
    assert d == D_MODEL and t % ROW_TILE == 0 and n_ctx % ROW_TILE == 0 and t % GRID_W == 0
    rope = _rope_tables(t, n_ctx)
    r_pad = -(-(b + 1) // 8) * 8
    cs = jnp.zeros((r_pad, d), F32).at[:b].set(c).at[b].set(c_ctx)
    ada = _ada_table(cs, ada_w, ada_b).reshape(depth, r_pad, 6, d)
    h = jnp.concatenate([ctx, x], axis=1)
    for l in range(depth):
        ctx_out = any(m % 2 == 0 for m in range(l + 1, depth))
        j = l // 2
        mods = jnp.stack([jnp.broadcast_to(ada[l, b], (b, 6, d)), ada[l, :b]], axis=1)
        if l % 2 == 0:
            p = {"w_in": ev_w_in[j], "w_out": ev_w_out[j], "mu": rw_mu[j], "w0": rw_w0[j], "w2": rw_w2[j],
                 "a0": rw_a0[j], "a2": rw_a2[j], "g2": rw_g2[j], "k_k": rw_kk[j], "k_a": rw_ka[j],
                 "r_k": rw_rk[j], "gn_w": rw_gn_w[j], "gn_b": rw_gn_b[j], "qn": df_qn[j], "kn": df_kn[j],
                 "lam": df_lam[j], "subln": df_subln[j]}
            h = _even_layer(h, mods, norm1_g[l], p, l, n_ctx, rope)
        else:
            p = {"w_in": sg_w_in[j], "b_in": sg_b_in[j], "ln_g": sg_ln_g[j], "ln_b": sg_ln_b[j],
                 "ws": sg_ws[j], "bs": sg_bs[j], "w_out": sg_w_out[j]}
            h = _sgu_layer(h, mods, norm1_g[l], p, n_ctx)
        if n_ctx > 0 and not ctx_out:
            h = h[:, n_ctx:]
            n_ctx = 0
        h = _moe(h, mods, norm2_g[l], rt_w[l], rt_b[l], ex_w1, ex_b1, ex_w2, ex_b2, l, n_ctx, 0, b)
    return h[:, n_ctx:]
```
